```python
import math
import jax
import jax.numpy as jnp
from jax import lax
import numpy as np

D_MODEL = 2048
BATCH = 8
SEQ = 2048
DEPTH = 1
DEC_BATCH = 8
DEC_SEQ = 16
PAST_LEN = 1024

CHUNK = 64
D_MIX = D_MODEL
D_A = D_MIX // 2
D_B = D_MIX - D_A
A_HEADS = 4
A_DV = D_A // A_HEADS
A_DQK = A_DV // 2
QK_W = 2 * A_HEADS * A_DQK
CONV_W = 4
B_GROUP = 16
B_GROUPS = D_B // B_GROUP
B_STATE = 64
D_FF = -(-8 * D_MODEL // (3 * 256)) * 256
N_IN = QK_W + 2 * D_A + 2 * A_HEADS + D_B
EPS = 1e-6

kernel_name = 'hymba_mlstm_s5_stream_step'


def rmsnorm(x, g):
    xf = x.astype(jnp.float32)
    y = xf * lax.rsqrt(jnp.mean(xf * xf, axis=-1, keepdims=True) + EPS)
    return (y * g.astype(jnp.float32)).astype(x.dtype)


def causal_conv(x, buf, w, b):
    seqlen = x.shape[1]
    xp = jnp.concatenate([buf.astype(x.dtype), x], axis=1)
    out = b
    for j in range(CONV_W):
        out = out + xp[:, j:j + seqlen] * w[j]
    return jax.nn.silu(out), xp[:, xp.shape[1] - (CONV_W - 1):]


def mlstm_chunk(c0, n0, m0, q, k, v, li, lf):
    seqlen = q.shape[1]
    li = jnp.swapaxes(li, 1, 2)
    fcum = jnp.cumsum(jnp.swapaxes(lf, 1, 2), axis=-1)
    causal = jnp.tril(jnp.ones((seqlen, seqlen), dtype=bool))
    dmat = jnp.where(causal, fcum[..., :, None] - fcum[..., None, :] + li[..., None, :], -jnp.inf)
    inter = fcum + m0[..., None]
    m_t = jnp.maximum(jnp.max(dmat, axis=-1), inter)
    s = jnp.einsum('bthd,bshd->bhts', q, k) * jnp.exp(dmat - m_t[..., None])
    g_inter = jnp.exp(inter - m_t)
    num = jnp.einsum('bhts,bshv->bthv', s, v) + jnp.einsum('bht,bhvd,bthd->bthv', g_inter, c0, q)
    den = jnp.sum(s, axis=-1) + g_inter * jnp.einsum('bhd,bthd->bht', n0, q)
    scale = jnp.maximum(jnp.abs(den), jnp.exp(-m_t))
    h = num / jnp.swapaxes(scale, 1, 2)[..., None]
    f_last = fcum[..., -1]
    lw = f_last[..., None] - fcum + li
    m_new = jnp.maximum(f_last + m0, jnp.max(lw, axis=-1))
    ws = jnp.exp(lw - m_new[..., None])
    decay = jnp.exp(f_last + m0 - m_new)
    c_new = decay[..., None, None] * c0 + jnp.einsum('bhs,bshv,bshd->bhvd', ws, v, k)
    n_new = decay[..., None] * n0 + jnp.einsum('bhs,bshd->bhd', ws, k)
    return c_new, n_new, m_new, h


def s5_discretise(lam_re, lam_im, log_dt, b_re, b_im):
    f32 = jnp.float32
    lam_re = lam_re.astype(f32)
    lam_im = lam_im.astype(f32)
    b_re = b_re.astype(f32)
    b_im = b_im.astype(f32)
    dt = jnp.exp(log_dt.astype(f32))[:, None]
    mag = jnp.exp(lam_re * dt)
    ang = lam_im * dt
    lb_re = mag * jnp.cos(ang)
    lb_im = mag * jnp.sin(ang)
    den = lam_re * lam_re + lam_im * lam_im
    nr = lb_re - 1.0
    f_re = (nr * lam_re + lb_im * lam_im) / den
    f_im = (lb_im * lam_re - nr * lam_im) / den
    bb_re = f_re[..., None] * b_re - f_im[..., None] * b_im
    bb_im = f_re[..., None] * b_im + f_im[..., None] * b_re
    return lb_re, lb_im, bb_re, bb_im


def _affine_combine(e1, e2):
    a1r, a1i, b1r, b1i = e1
    a2r, a2i, b2r, b2i = e2
    return (a1r * a2r - a1i * a2i,
            a1r * a2i + a1i * a2r,
            a2r * b1r - a2i * b1i + b2r,
            a2r * b1i + a2i * b1r + b2i)


def s5_chunk(x_re, x_im, u, lb_re, lb_im, bb_re, bb_im, c_re, c_im, d_skip):
    bu_re = jnp.einsum('blgi,gpi->blgp', u, bb_re)
    bu_im = jnp.einsum('blgi,gpi->blgp', u, bb_im)
    bu_re = bu_re.at[:, 0].add(lb_re * x_re - lb_im * x_im)
    bu_im = bu_im.at[:, 0].add(lb_re * x_im + lb_im * x_re)
    a_re = jnp.broadcast_to(lb_re, bu_re.shape)
    a_im = jnp.broadcast_to(lb_im, bu_im.shape)
    _, _, s_re, s_im = lax.associative_scan(_affine_combine, (a_re, a_im, bu_re, bu_im), axis=1)
    y = (jnp.einsum('gip,blgp->blgi', c_re, s_re)
         - jnp.einsum('gip,blgp->blgi', c_im, s_im) + d_skip * u)
    return s_re[:, -1], s_im[:, -1], y


def token_mixers(h, conv_buf, c0, n0, m0, sr0, si0, p):
    f32 = jnp.float32
    bt, seqlen, _ = h.shape
    cl = min(seqlen, CHUNK)
    nc = seqlen // cl
    proj = h @ p['w_in']
    o1 = QK_W
    o2 = o1 + D_A
    o3 = o2 + D_A
    o4 = o3 + A_HEADS
    o5 = o4 + A_HEADS
    qk, conv_new = causal_conv(proj[..., :o1], conv_buf, p['w_conv'], p['b_conv'])
    qk = qk.astype(f32)
    q = qk[..., :QK_W // 2].reshape(bt, seqlen, A_HEADS, A_DQK) * (A_DQK ** -0.5)
    k = qk[..., QK_W // 2:].reshape(bt, seqlen, A_HEADS, A_DQK)
    v = proj[..., o1:o2].astype(f32).reshape(bt, seqlen, A_HEADS, A_DV)
    og = proj[..., o2:o3].astype(f32).reshape(bt, seqlen, A_HEADS, A_DV)
    li = proj[..., o3:o4].astype(f32) + p['b_igate'].astype(f32)
    lf = jax.nn.log_sigmoid(proj[..., o4:o5].astype(f32) + p['b_fgate'].astype(f32))
    u = proj[..., o5:].astype(f32).reshape(bt, seqlen, B_GROUPS, B_GROUP)
    d_skip = p['s5_d'].astype(f32).reshape(B_GROUPS, B_GROUP)
    c_re = p['s5_c_re'].astype(f32)
    c_im = p['s5_c_im'].astype(f32)

    def to_chunks(a):
        return jnp.moveaxis(a.reshape((bt, nc, cl) + a.shape[2:]), 1, 0)

    xs = tuple(to_chunks(a) for a in (q, k, v, li, lf, u))

    def step(carry, xc):
        c, n, m, sr, si = carry
        qc, kc, vc, lic, lfc, uc = xc
        c, n, m, hc = mlstm_chunk(c, n, m, qc, kc, vc, lic, lfc)
        sr, si, yc = s5_chunk(sr, si, uc, p['lb_re'], p['lb_im'], p['bb_re'], p['bb_im'],
                              c_re, c_im, d_skip)
        return (c, n, m, sr, si), (hc, yc)

    init = (c0.astype(f32), n0.astype(f32), m0.astype(f32), sr0.astype(f32), si0.astype(f32))
    (c1, n1, m1, sr1, si1), (hs, ys) = lax.scan(step, init, xs)
    hs = jnp.moveaxis(hs, 0, 1).reshape(bt, seqlen, A_HEADS, A_DV)
    ys = jnp.moveaxis(ys, 0, 1).reshape(bt, seqlen, D_B)
    g_mh = p['g_mh'].astype(f32).reshape(A_HEADS, A_DV)
    hn = hs * lax.rsqrt(jnp.mean(hs * hs, axis=-1, keepdims=True) + EPS) * g_mh
    out_a = (hn * jax.nn.sigmoid(og)).reshape(bt, seqlen, D_A)
    z = jax.nn.gelu(ys)
    gate = jax.nn.sigmoid((z.astype(h.dtype) @ p['w_glu'] + p['b_glu']).astype(f32))
    out_b = z * gate
    mix = jnp.concatenate([out_a, out_b], axis=-1).astype(h.dtype) @ p['w_out']
    return mix, (conv_new, c1, n1, m1, sr1, si1)


def layer(x, conv_buf, c0, n0, m0, sr0, si0, p):
    hm = rmsnorm(x, p['g_pre_mix'])
    mix, st = token_mixers(hm, conv_buf, c0, n0, m0, sr0, si0, p)
    x = x + rmsnorm(mix, p['g_post_mix'])
    hf = rmsnorm(x, p['g_pre_ffn'])
    ff = (jax.nn.silu(hf @ p['w_gate']) * (hf @ p['w_up'])) @ p['w_down']
    x = x + rmsnorm(ff, p['g_post_ffn'])
    return x, st


def setup_inputs(seed: int = 0) -> dict:
    key = jax.random.key(seed)
    ks = jax.random.split(key, 32)
    f32 = jnp.float32

    def nrm(k, shape, s):
        return jax.random.normal(k, shape, f32) * s

    L = DEPTH
    n_idx = jnp.arange(B_STATE, dtype=f32)
    return {
        'x_prompt': nrm(ks[0], (BATCH, SEQ, D_MODEL), 1.0),
        'x_sample': nrm(ks[1], (DEC_BATCH, DEC_SEQ, D_MODEL), 1.0),
        'state_conv': nrm(ks[2], (L, DEC_BATCH, CONV_W - 1, QK_W), 1.0),
        'state_mlstm_c': nrm(ks[3], (L, DEC_BATCH, A_HEADS, A_DV, A_DQK), 0.5),
        'state_mlstm_n': nrm(ks[4], (L, DEC_BATCH, A_HEADS, A_DQK), 0.5),
        'state_mlstm_m': nrm(ks[5], (L, DEC_BATCH, A_HEADS), 1.0),
        'state_s5_re': nrm(ks[6], (L, DEC_BATCH, B_GROUPS, B_STATE), 0.5),
        'state_s5_im': nrm(ks[7], (L, DEC_BATCH, B_GROUPS, B_STATE), 0.5),
        'g_pre_mix': 1.0 + nrm(ks[8], (L, D_MODEL), 0.02),
        'w_in': nrm(ks[9], (L, D_MODEL, N_IN), D_MODEL ** -0.5),
        'w_conv': nrm(ks[10], (L, CONV_W, QK_W), CONV_W ** -0.5),
        'b_conv': nrm(ks[11], (L, QK_W), 0.02),
        'b_igate': nrm(ks[12], (L, A_HEADS), 0.1),
        'b_fgate': jnp.linspace(3.0, 6.0, A_HEADS, dtype=f32)[None] + nrm(ks[13], (L, A_HEADS), 0.1),
        'g_mh': 1.0 + nrm(ks[14], (L, D_A), 0.02),
        's5_lam_re': -0.5 + nrm(ks[15], (L, B_GROUPS, B_STATE), 0.01),
        's5_lam_im': math.pi * n_idx + nrm(ks[16], (L, B_GROUPS, B_STATE), 0.01),
        's5_log_dt': jax.random.uniform(ks[17], (L, B_GROUPS), f32, math.log(1e-3), math.log(1e-1)),
        's5_b_re': nrm(ks[18], (L, B_GROUPS, B_STATE, B_GROUP), (2 * B_GROUP) ** -0.5),
        's5_b_im': nrm(ks[19], (L, B_GROUPS, B_STATE, B_GROUP), (2 * B_GROUP) ** -0.5),
        's5_c_re': nrm(ks[20], (L, B_GROUPS, B_GROUP, B_STATE), B_STATE ** -0.5),
        's5_c_im': nrm(ks[21], (L, B_GROUPS, B_GROUP, B_STATE), B_STATE ** -0.5),
        's5_d': nrm(ks[22], (L, D_B), 1.0),
        'w_glu': nrm(ks[23], (L, D_B, D_B), D_B ** -0.5),
        'b_glu': nrm(ks[24], (L, D_B), 0.02),
        'w_out': nrm(ks[25], (L, D_MIX, D_MODEL), D_MIX ** -0.5),
        'g_post_mix': 1.0 + nrm(ks[26], (L, D_MODEL), 0.02),
        'g_pre_ffn': 1.0 + nrm(ks[27], (L, D_MODEL), 0.02),
        'w_gate': nrm(ks[28], (L, D_MODEL, D_FF), D_MODEL ** -0.5),
        'w_up': nrm(ks[29], (L, D_MODEL, D_FF), D_MODEL ** -0.5),
        'w_down': nrm(ks[30], (L, D_FF, D_MODEL), D_FF ** -0.5),
        'g_post_ffn': 1.0 + nrm(ks[31], (L, D_MODEL), 0.02),
    }


def reference(x_prompt, x_sample, state_conv, state_mlstm_c, state_mlstm_n, state_mlstm_m,
              state_s5_re, state_s5_im, g_pre_mix, w_in, w_conv, b_conv, b_igate, b_fgate,
              g_mh, s5_lam_re, s5_lam_im, s5_log_dt, s5_b_re, s5_b_im, s5_c_re, s5_c_im,
              s5_d, w_glu, b_glu, w_out, g_post_mix, g_pre_ffn, w_gate, w_up, w_down,
              g_post_ffn):
    f32 = jnp.float32
    yp = x_prompt
    ys = x_sample
    bp = x_prompt.shape[0]
    new_p = [[] for _ in range(6)]
    new_s = [[] for _ in range(6)]
    for l in range(DEPTH):
        lb_re, lb_im, bb_re, bb_im = s5_discretise(s5_lam_re[l], s5_lam_im[l], s5_log_dt[l],
                                                   s5_b_re[l], s5_b_im[l])
        p = {
            'g_pre_mix': g_pre_mix[l], 'w_in': w_in[l], 'w_conv': w_conv[l], 'b_conv': b_conv[l],
            'b_igate': b_igate[l], 'b_fgate': b_fgate[l], 'g_mh': g_mh[l],
            'lb_re': lb_re, 'lb_im': lb_im, 'bb_re': bb_re, 'bb_im': bb_im,
            's5_c_re': s5_c_re[l], 's5_c_im': s5_c_im[l], 's5_d': s5_d[l],
            'w_glu': w_glu[l], 'b_glu': b_glu[l], 'w_out': w_out[l], 'g_post_mix': g_post_mix[l],
            'g_pre_ffn': g_pre_ffn[l], 'w_gate': w_gate[l], 'w_up': w_up[l], 'w_down': w_down[l],
            'g_post_ffn': g_post_ffn[l],
        }
        yp, st_p = layer(yp,
                         jnp.zeros((bp, CONV_W - 1, QK_W), yp.dtype),
                         jnp.zeros((bp, A_HEADS, A_DV, A_DQK), f32),
                         jnp.zeros((bp, A_HEADS, A_DQK), f32),
                         jnp.zeros((bp, A_HEADS), f32),
                         jnp.zeros((bp, B_GROUPS, B_STATE), f32),
                         jnp.zeros((bp, B_GROUPS, B_STATE), f32),
                         p)
        ys, st_s = layer(ys, state_conv[l], state_mlstm_c[l], state_mlstm_n[l], state_mlstm_m[l],
                         state_s5_re[l], state_s5_im[l], p)
        for i in range(6):
            new_p[i].append(st_p[i])
            new_s[i].append(st_s[i])
    conv_p, c_p, n_p, m_p, s5re_p, s5im_p = [jnp.stack(a) for a in new_p]
    conv_s, c_s, n_s, m_s, s5re_s, s5im_s = [jnp.stack(a) for a in new_s]
    return (yp, ys, conv_p, c_p, n_p, m_p, s5re_p, s5im_p,
            conv_s, c_s, n_s, m_s, s5re_s, s5im_s)
```

```python
import functools

import numpy as np
import jax
import jax.numpy as jnp
from jax import lax
from jax.experimental import pallas as pl
from jax.experimental.pallas import tpu as pltpu

F32 = jnp.float32
BF16 = jnp.bfloat16

D_MODEL = 2048
D_A = 1024
D_B = 1024
A_HEADS = 4
A_DV = 256
A_DQK = 128
QK_W = 1024
CONV_W = 4
B_GROUP = 16
B_GROUPS = 64
B_STATE = 64
S5_N = B_GROUPS * B_STATE
D_FF = 5632
EPS = 1e-6

SUBLANES = 8
LANES = 128
GROUPS_PER_BLOCK = LANES // B_GROUP
N_BLOCKS = B_GROUPS // GROUPS_PER_BLOCK
STATE_PER_BLOCK = GROUPS_PER_BLOCK * B_STATE
PERM_T = 16
VMEM_LIMIT = 56 * 1024 * 1024


def _sigmoid(x):
    return 1.0 / (1.0 + jnp.exp(-x))


def _gelu_tanh(x):
    c = np.sqrt(2 / np.pi).astype(np.float32)
    return x * (0.5 * (1.0 + jnp.tanh(c * (x + 0.044715 * (x ** 3)))))


def _rms(x, g):
    return x * lax.rsqrt(jnp.mean(x * x, axis=-1, keepdims=True) + EPS) * g


def _dot(a, b):
    return jnp.dot(a, b, preferred_element_type=F32)


def _resident(shape):
    zeros = (0,) * len(shape)
    return pl.BlockSpec(shape, lambda *_: zeros, pipeline_mode=pl.Buffered(1))


def _disc_kernel(lam_re_ref, lam_im_ref, logdt_ref, bre_ref, bim_ref,
                 lbre_ref, lbim_ref, bbre_ref, bbim_ref):
    lam_re = lam_re_ref[...]
    lam_im = lam_im_ref[...]
    dt = jnp.exp(logdt_ref[...])
    mag = jnp.exp(lam_re * dt)
    ang = lam_im * dt
    lb_re = mag * jnp.cos(ang)
    lb_im = mag * jnp.sin(ang)
    den = lam_re * lam_re + lam_im * lam_im
    nr = lb_re - 1.0
    f_re = (nr * lam_re + lb_im * lam_im) / den
    f_im = (lb_im * lam_re - nr * lam_im) / den
    lbre_ref[...] = lb_re
    lbim_ref[...] = lb_im
    b_re = bre_ref[...]
    b_im = bim_ref[...]
    bbre_ref[...] = f_re[None] * b_re - f_im[None] * b_im
    bbim_ref[...] = f_re[None] * b_im + f_im[None] * b_re


def _discretise(lam_re, lam_im, log_dt, bt_re, bt_im):
    gp = jax.ShapeDtypeStruct((B_GROUPS, B_STATE), F32)
    igp = jax.ShapeDtypeStruct((B_GROUP, B_GROUPS, B_STATE), F32)
    return pl.pallas_call(
        _disc_kernel, out_shape=(gp, gp, igp, igp), name="s5_discretise",
    )(lam_re, lam_im, log_dt.reshape(B_GROUPS, 1), bt_re, bt_im)


def _inproj_kernel(x_ref, gpre_ref, wqk_ref, wv_ref, wog_ref, wu_ref, wgt_ref, gb_ref,
                   qk_out, v_out, og_out, u_out, g_out):
    hm = _rms(x_ref[...], gpre_ref[...]).astype(BF16)
    qk_out[...] = _dot(hm, wqk_ref[...])
    v_out[...] = _dot(hm, wv_ref[...]).astype(BF16)
    og_out[...] = _dot(hm, wog_ref[...])
    u_out[...] = _dot(hm, wu_ref[...])
    z = lax.dot_general(wgt_ref[...], hm, (((1,), (1,)), ((), ())),
                        preferred_element_type=F32) + gb_ref[...]
    log_sig = jnp.minimum(z, 0.0) - jnp.log1p(jnp.exp(-jnp.abs(z)))
    row = lax.broadcasted_iota(jnp.int32, z.shape, 0)
    g_out[...] = jnp.where(row >= SUBLANES, log_sig, z)


def _in_proj(x2d, g_pre, wqk, wv, wog, wu, wgt, gbias, tm):
    t = x2d.shape[0]
    tok = lambda i: (i, 0)
    return pl.pallas_call(
        _inproj_kernel,
        grid=(t // tm,),
        in_specs=[
            pl.BlockSpec((tm, D_MODEL), tok),
            _resident((1, D_MODEL)),
            _resident((D_MODEL, QK_W)),
            _resident((D_MODEL, D_A)),
            _resident((D_MODEL, D_A)),
            _resident((D_MODEL, D_B)),
            _resident((2 * SUBLANES, D_MODEL)),
            _resident((2 * SUBLANES, 1)),
        ],
        out_specs=[
            pl.BlockSpec((tm, QK_W), tok),
            pl.BlockSpec((tm, D_A), tok),
            pl.BlockSpec((tm, D_A), tok),
            pl.BlockSpec((tm, D_B), tok),
            pl.BlockSpec((2 * SUBLANES, tm), lambda i: (0, i)),
        ],
        out_shape=(
            jax.ShapeDtypeStruct((t, QK_W), F32),
            jax.ShapeDtypeStruct((t, D_A), BF16),
            jax.ShapeDtypeStruct((t, D_A), F32),
            jax.ShapeDtypeStruct((t, D_B), F32),
            jax.ShapeDtypeStruct((2 * SUBLANES, t), F32),
        ),
        compiler_params=pltpu.CompilerParams(
            dimension_semantics=("arbitrary",), vmem_limit_bytes=VMEM_LIMIT),
        name="in_proj",
    )(x2d, g_pre, wqk, wv, wog, wu, wgt, gbias)


def _mlstm_kernel(qk_ref, v_ref, og_ref, g_ref, tail0_ref, c0_ref, n0_ref, m0_ref,
                  wconv_ref, bconv_ref, gmh_ref,
                  outa_ref, convo_ref, co_ref, no_ref, mo_ref,
                  xp, ct, n_s, m_s, *, chunk, valid, n_chunks):
    L = chunk
    c = pl.program_id(1)

    @pl.when(c == 0)
    def _():
        xp[0:SUBLANES, :] = tail0_ref[0]
        for h in range(A_HEADS):
            ct[h] = c0_ref[0, h].T
        n_s[...] = n0_ref[0]
        m_s[...] = m0_ref[0]

    xp[SUBLANES:SUBLANES + L, :] = qk_ref[0]
    acc = bconv_ref[...]
    for j in range(CONV_W):
        lo = SUBLANES - (CONV_W - 1) + j
        acc = acc + xp[lo:lo + L, :] * wconv_ref[j:j + 1, :]
    qk = acc * _sigmoid(acc)

    @pl.when(c == n_chunks - 1)
    def _():
        convo_ref[0] = xp[valid:valid + SUBLANES, :]

    xp[0:SUBLANES, :] = xp[L:L + SUBLANES, :]

    g = g_ref[...]
    li = g[0:SUBLANES]
    lf = g[SUBLANES:2 * SUBLANES]
    lane = lax.broadcasted_iota(jnp.int32, (SUBLANES, L), 1)
    if valid < L:
        li = jnp.where(lane < valid, li, -jnp.inf)
        lf = jnp.where(lane < valid, lf, 0.0)
    fc = lf
    sh = 1
    while sh < L:
        fc = fc + jnp.where(lane >= sh, pltpu.roll(fc, sh, axis=1), 0.0)
        sh *= 2
    m0 = m_s[:, 0:1]
    f_last = fc[:, L - 1:L]
    gr = li - fc
    lw = f_last + gr
    m_new = jnp.maximum(f_last + m0, jnp.max(lw, axis=1, keepdims=True))
    ws = jnp.exp(lw - m_new)
    decay = jnp.exp(f_last + m0 - m_new)

    row_i = lax.broadcasted_iota(jnp.int32, (L, L), 0)
    col_i = lax.broadcasted_iota(jnp.int32, (L, L), 1)
    tri = col_i <= row_i
    eye = col_i == row_i
    head_row = lax.broadcasted_iota(jnp.int32, (SUBLANES, A_DQK), 0)
    n_old = n_s[...]
    n_new = n_old
    q_all = qk[:, :QK_W // 2] * (A_DQK ** -0.5)
    for h in range(A_HEADS):
        qf = q_all[:, A_DQK * h:A_DQK * (h + 1)]
        kf = qk[:, QK_W // 2 + A_DQK * h:QK_W // 2 + A_DQK * (h + 1)]
        qb = qf.astype(BF16)
        kb = kf.astype(BF16)
        vb = v_ref[0, :, A_DV * h:A_DV * (h + 1)]
        fc_h = fc[h:h + 1, :]
        fc_col = jnp.sum(jnp.where(eye, fc_h, 0.0), axis=1, keepdims=True)
        m0_h = m0[h:h + 1, :]
        dmat = jnp.where(tri, fc_col + gr[h:h + 1, :], -jnp.inf)
        inter = fc_col + m0_h
        m_t = jnp.maximum(jnp.max(dmat, axis=1, keepdims=True), inter)
        s = lax.dot_general(qb, kb, (((1,), (1,)), ((), ())),
                            preferred_element_type=F32) * jnp.exp(dmat - m_t)
        g_int = jnp.exp(inter - m_t)
        ct_h = ct[h]
        num = _dot(s.astype(BF16), vb) + g_int * _dot(qb, ct_h.astype(BF16))
        den = (jnp.sum(s, axis=1, keepdims=True)
               + g_int * jnp.sum(qf * n_old[h:h + 1, :], axis=1, keepdims=True))
        scale = jnp.maximum(jnp.abs(den), jnp.exp(-m_t))
        hh = num / scale
        hn = _rms(hh, gmh_ref[:, A_DV * h:A_DV * (h + 1)])
        og = og_ref[0, :, A_DV * h:A_DV * (h + 1)]
        outa_ref[0, :, A_DV * h:A_DV * (h + 1)] = (hn * _sigmoid(og)).astype(BF16)
        kw = (kf.T * ws[h:h + 1, :]).astype(BF16)
        ct[h] = decay[h:h + 1, :] * ct_h + _dot(kw, vb)
        n_upd = decay * n_old + _dot(ws, kf)
        n_new = jnp.where(head_row == h, n_upd, n_new)
    n_s[...] = n_new
    m_s[...] = jnp.broadcast_to(m_new, m_s.shape)

    @pl.when(c == n_chunks - 1)
    def _():
        for h in range(A_HEADS):
            co_ref[0, h] = ct[h].T
        no_ref[0] = n_s[...]
        mo_ref[0] = m_s[...]


def _mlstm(qk_raw, v, og, gates, tail0, c0, n0, m0, wconv, bconv, gmh, *, chunk, valid):
    b, s_len, _ = qk_raw.shape
    n_chunks = s_len // chunk
    seq = lambda i, c: (i, c, 0)
    per_b3 = lambda i, c: (i, 0, 0)
    kernel = functools.partial(_mlstm_kernel, chunk=chunk, valid=valid, n_chunks=n_chunks)
    return pl.pallas_call(
        kernel,
        grid=(b, n_chunks),
        in_specs=[
            pl.BlockSpec((1, chunk, QK_W), seq),
            pl.BlockSpec((1, chunk, D_A), seq),
            pl.BlockSpec((1, chunk, D_A), seq),
            pl.BlockSpec((2 * SUBLANES, chunk), lambda i, c: (0, i * n_chunks + c)),
            pl.BlockSpec((1, SUBLANES, QK_W), per_b3),
            pl.BlockSpec((1, A_HEADS, A_DV, A_DQK), lambda i, c: (i, 0, 0, 0)),
            pl.BlockSpec((1, SUBLANES, A_DQK), per_b3),
            pl.BlockSpec((1, SUBLANES, LANES), per_b3),
            pl.BlockSpec((CONV_W, QK_W), lambda i, c: (0, 0)),
            pl.BlockSpec((1, QK_W), lambda i, c: (0, 0)),
            pl.BlockSpec((1, D_A), lambda i, c: (0, 0)),
        ],
        out_specs=[
            pl.BlockSpec((1, chunk, D_A), seq),
            pl.BlockSpec((1, SUBLANES, QK_W), per_b3),
            pl.BlockSpec((1, A_HEADS, A_DV, A_DQK), lambda i, c: (i, 0, 0, 0)),
            pl.BlockSpec((1, SUBLANES, A_DQK), per_b3),
            pl.BlockSpec((1, SUBLANES, LANES), per_b3),
        ],
        out_shape=(
            jax.ShapeDtypeStruct((b, s_len, D_A), BF16),
            jax.ShapeDtypeStruct((b, SUBLANES, QK_W), F32),
            jax.ShapeDtypeStruct((b, A_HEADS, A_DV, A_DQK), F32),
            jax.ShapeDtypeStruct((b, SUBLANES, A_DQK), F32),
            jax.ShapeDtypeStruct((b, SUBLANES, LANES), F32),
        ),
        scratch_shapes=[
            pltpu.VMEM((chunk + SUBLANES, QK_W), F32),
            pltpu.VMEM((A_HEADS, A_DQK, A_DV), F32),
            pltpu.VMEM((SUBLANES, A_DQK), F32),
            pltpu.VMEM((SUBLANES, LANES), F32),
        ],
        compiler_params=pltpu.CompilerParams(
            dimension_semantics=("arbitrary", "arbitrary"), vmem_limit_bytes=VMEM_LIMIT),
        name="mlstm",
    )(qk_raw, v, og, gates, tail0, c0, n0, m0, wconv, bconv, gmh)


def _s5_kernel(u_ref, s0re_ref, s0im_ref, lbre_ref, lbim_ref, wb_ref, wcre_ref, wcim_ref,
               d_ref, perm_ref, permt_ref, wglu_ref, bglu_ref,
               outb_ref, sre_out, sim_out,
               bure, buim, sre, sim, ytb, ybt, *, tb, n_steps):
    step = pl.program_id(0)
    batch = u_ref.shape[0]
    rows_j = batch * PERM_T
    n_j = tb // PERM_T

    @pl.when(step == 0)
    def _():
        sre[...] = s0re_ref[...]
        sim[...] = s0im_ref[...]

    for j in range(n_j):
        xj = jnp.concatenate(
            [u_ref[b, PERM_T * j:PERM_T * (j + 1), :] for b in range(batch)], axis=0)
        xt = _dot(perm_ref[...], xj.astype(BF16)).astype(BF16)
        rows = slice(rows_j * j, rows_j * (j + 1))
        for k in range(N_BLOCKS):
            r = _dot(xt[:, LANES * k:LANES * (k + 1)], wb_ref[k])
            cols = slice(STATE_PER_BLOCK * k, STATE_PER_BLOCK * (k + 1))
            bure[rows, cols] = r[:, :STATE_PER_BLOCK]
            buim[rows, cols] = r[:, STATE_PER_BLOCK:]

    strip = 8 * LANES
    for st in range(S5_N // strip):
        cols = slice(strip * st, strip * (st + 1))
        a_re = jnp.broadcast_to(lbre_ref[:, cols], (batch, strip))
        a_im = jnp.broadcast_to(lbim_ref[:, cols], (batch, strip))

        def body(t, carry, cols=cols, a_re=a_re, a_im=a_im):
            s_r, s_i = carry
            r0 = pl.multiple_of(t * batch, batch)
            n_r = a_re * s_r - a_im * s_i + bure[pl.ds(r0, batch), cols]
            n_i = a_re * s_i + a_im * s_r + buim[pl.ds(r0, batch), cols]
            bure[pl.ds(r0, batch), cols] = n_r
            buim[pl.ds(r0, batch), cols] = n_i
            return n_r, n_i

        s_r, s_i = lax.fori_loop(0, tb, body, (sre[:, cols], sim[:, cols]), unroll=4)
        sre[:, cols] = s_r
        sim[:, cols] = s_i

    for k in range(N_BLOCKS):
        cols = slice(STATE_PER_BLOCK * k, STATE_PER_BLOCK * (k + 1))
        ytb[:, LANES * k:LANES * (k + 1)] = (
            _dot(bure[:, cols].astype(BF16), wcre_ref[k])
            - _dot(buim[:, cols].astype(BF16), wcim_ref[k]))

    for j in range(n_j):
        yj = ytb[rows_j * j:rows_j * (j + 1), :]
        hi = yj.astype(BF16)
        r1 = yj - hi.astype(F32)
        mid = r1.astype(BF16)
        lo = (r1 - mid.astype(F32)).astype(BF16)
        pt = permt_ref[...]
        yb = _dot(pt, hi) + _dot(pt, mid) + _dot(pt, lo)
        for b in range(batch):
            ybt[tb * b + PERM_T * j:tb * b + PERM_T * (j + 1), :] = (
                yb[PERM_T * b:PERM_T * (b + 1), :])

    u2 = jnp.concatenate([u_ref[b] for b in range(batch)], axis=0)
    z = _gelu_tanh(ybt[...] + d_ref[...] * u2)
    gate = _sigmoid(_dot(z.astype(BF16), wglu_ref[...]) + bglu_ref[...])
    res = (z * gate).astype(BF16)
    for b in range(batch):
        outb_ref[b] = res[tb * b:tb * (b + 1), :]

    @pl.when(step == n_steps - 1)
    def _():
        sre_out[...] = sre[...]
        sim_out[...] = sim[...]


def _s5(u, s0re, s0im, lbre, lbim, wb, wcre, wcim, d_skip, perm, permt, wglu, bglu, *, tb):
    b, s_len, _ = u.shape
    n_steps = s_len // tb
    rows = b * tb
    kernel = functools.partial(_s5_kernel, tb=tb, n_steps=n_steps)
    state = jax.ShapeDtypeStruct((b, S5_N), F32)
    return pl.pallas_call(
        kernel,
        grid=(n_steps,),
        in_specs=[
            pl.BlockSpec((b, tb, D_B), lambda i: (0, i, 0)),
            _resident((b, S5_N)),
            _resident((b, S5_N)),
            _resident((1, S5_N)),
            _resident((1, S5_N)),
            _resident((N_BLOCKS, LANES, 2 * STATE_PER_BLOCK)),
            _resident((N_BLOCKS, STATE_PER_BLOCK, LANES)),
            _resident((N_BLOCKS, STATE_PER_BLOCK, LANES)),
            _resident((1, D_B)),
            _resident((b * PERM_T, b * PERM_T)),
            _resident((b * PERM_T, b * PERM_T)),
            _resident((D_B, D_B)),
            _resident((1, D_B)),
        ],
        out_specs=[
            pl.BlockSpec((b, tb, D_B), lambda i: (0, i, 0)),
            pl.BlockSpec((b, S5_N), lambda i: (0, 0)),
            pl.BlockSpec((b, S5_N), lambda i: (0, 0)),
        ],
        out_shape=(jax.ShapeDtypeStruct((b, s_len, D_B), BF16), state, state),
        scratch_shapes=[
            pltpu.VMEM((rows, S5_N), F32),
            pltpu.VMEM((rows, S5_N), F32),
            pltpu.VMEM((b, S5_N), F32),
            pltpu.VMEM((b, S5_N), F32),
            pltpu.VMEM((rows, D_B), F32),
            pltpu.VMEM((rows, D_B), F32),
        ],
        compiler_params=pltpu.CompilerParams(
            dimension_semantics=("arbitrary",), vmem_limit_bytes=VMEM_LIMIT),
        name="s5",
    )(u, s0re, s0im, lbre, lbim, wb, wcre, wcim, d_skip, perm, permt, wglu, bglu)


def _outproj_kernel(a_ref, b_ref, x_ref, wo_ref, gpost_ref, gffn_ref, x1_ref, hf_ref):
    mix = _dot(a_ref[...], wo_ref[0:D_A, :]) + _dot(b_ref[...], wo_ref[D_A:D_A + D_B, :])
    x1 = x_ref[...] + _rms(mix, gpost_ref[...])
    x1_ref[...] = x1
    hf_ref[...] = _rms(x1, gffn_ref[...]).astype(BF16)


def _out_proj(a, b, x2d, wo, g_post, g_ffn, tm):
    t = x2d.shape[0]
    tok = lambda i: (i, 0)
    return pl.pallas_call(
        _outproj_kernel,
        grid=(t // tm,),
        in_specs=[
            pl.BlockSpec((tm, D_A), tok),
            pl.BlockSpec((tm, D_B), tok),
            pl.BlockSpec((tm, D_MODEL), tok),
            _resident((D_A + D_B, D_MODEL)),
            _resident((1, D_MODEL)),
            _resident((1, D_MODEL)),
        ],
        out_specs=[pl.BlockSpec((tm, D_MODEL), tok), pl.BlockSpec((tm, D_MODEL), tok)],
        out_shape=(jax.ShapeDtypeStruct((t, D_MODEL), F32),
                   jax.ShapeDtypeStruct((t, D_MODEL), BF16)),
        compiler_params=pltpu.CompilerParams(
            dimension_semantics=("arbitrary",), vmem_limit_bytes=VMEM_LIMIT),
        name="out_proj",
    )(a, b, x2d, wo, g_post, g_ffn)


def _ffn_kernel(hf_ref, wg_ref, wu_ref, wd_ref, x1_ref, gp_ref, out_ref, acc, *, n_f):
    f = pl.program_id(1)
    hf = hf_ref[...]
    a = _dot(hf, wg_ref[...])
    act = (a * _sigmoid(a) * _dot(hf, wu_ref[...])).astype(BF16)
    part = _dot(act, wd_ref[...])

    @pl.when(f == 0)
    def _():
        acc[...] = part

    @pl.when(f > 0)
    def _():
        acc[...] += part

    @pl.when(f == n_f - 1)
    def _():
        out_ref[...] = x1_ref[...] + _rms(acc[...], gp_ref[...])


def _ffn(hf, wg, wu, wd, x1, g_post, tm, tf):
    t = hf.shape[0]
    n_f = D_FF // tf
    tok = lambda i, f: (i, 0)
    return pl.pallas_call(
        functools.partial(_ffn_kernel, n_f=n_f),
        grid=(t // tm, n_f),
        in_specs=[
            pl.BlockSpec((tm, D_MODEL), tok),
            pl.BlockSpec((D_MODEL, tf), lambda i, f: (0, f)),
            pl.BlockSpec((D_MODEL, tf), lambda i, f: (0, f)),
            pl.BlockSpec((tf, D_MODEL), lambda i, f: (f, 0)),
            pl.BlockSpec((tm, D_MODEL), tok),
            pl.BlockSpec((1, D_MODEL), lambda i, f: (0, 0)),
        ],
        out_specs=pl.BlockSpec((tm, D_MODEL), tok),
        out_shape=jax.ShapeDtypeStruct((t, D_MODEL), F32),
        scratch_shapes=[pltpu.VMEM((tm, D_MODEL), F32)],
        compiler_params=pltpu.CompilerParams(
            dimension_semantics=("arbitrary", "arbitrary"), vmem_limit_bytes=VMEM_LIMIT),
        name="ffn",
    )(hf, wg, wu, wd, x1, g_post)


def _block_diag_in(bbt):
    x = bbt.reshape(B_GROUP, N_BLOCKS, GROUPS_PER_BLOCK, B_STATE)
    x = jnp.transpose(x, (1, 2, 0, 3))
    eye = jnp.eye(GROUPS_PER_BLOCK, dtype=bool)[None, :, None, :, None]
    y = jnp.where(eye, x[:, :, :, None, :], 0.0)
    return y.reshape(N_BLOCKS, LANES, STATE_PER_BLOCK)


def _block_diag_out(c):
    x = c.reshape(N_BLOCKS, GROUPS_PER_BLOCK, B_GROUP, B_STATE)
    x = jnp.transpose(x, (0, 1, 3, 2))
    eye = jnp.eye(GROUPS_PER_BLOCK, dtype=bool)[None, :, None, :, None]
    y = jnp.where(eye, x[:, :, :, None, :], 0.0)
    return y.reshape(N_BLOCKS, STATE_PER_BLOCK, LANES)


def _row_permutation(batch):
    n = batch * PERM_T
    src = np.arange(n)
    b, t = src // PERM_T, src % PERM_T
    p = np.zeros((n, n), np.float32)
    p[t * batch + b, src] = 1.0
    return jnp.asarray(p, BF16), jnp.asarray(p.T, BF16)


def _pad_rows_front(x, rows):
    pad = [(0, 0)] * x.ndim
    pad[-2] = (rows - x.shape[-2], 0)
    return jnp.pad(x, pad)


def _pad_rows_back(x, rows):
    pad = [(0, 0)] * x.ndim
    pad[-2] = (0, rows - x.shape[-2])
    return jnp.pad(x, pad)


def _layer(x, conv0, c0, n0, m0, sre0, sim0, p, *, tm, chunk, tb, tf):
    b, s_len, _ = x.shape
    t = b * s_len
    x2d = x.reshape(t, D_MODEL)
    qk_raw, v, og, u, gates = _in_proj(
        x2d, p["g_pre_mix"], p["wqk"], p["wv"], p["wog"], p["wu"], p["wgt"], p["gbias"], tm)

    assert s_len % chunk == 0 or s_len < chunk
    s_pad = -(-s_len // chunk) * chunk
    def seq3(a):
        a = a.reshape(b, s_len, a.shape[-1])
        return a if s_pad == s_len else _pad_rows_back(a, s_pad)
    g3 = gates.reshape(2 * SUBLANES, b, s_len)
    if s_pad != s_len:
        g3 = jnp.pad(g3, ((0, 0), (0, 0), (0, s_pad - s_len)))
    out_a, conv_o, c_o, n_o, m_o = _mlstm(
        seq3(qk_raw), seq3(v), seq3(og), g3.reshape(2 * SUBLANES, b * s_pad),
        _pad_rows_front(conv0, SUBLANES), c0, _pad_rows_back(n0, SUBLANES),
        jnp.broadcast_to(_pad_rows_back(m0[..., None], SUBLANES), (b, SUBLANES, LANES)),
        p["w_conv"], p["b_conv"], p["g_mh"], chunk=chunk, valid=min(s_len, chunk))
    out_a = out_a[:, :s_len].reshape(t, D_A)

    out_b, sre_o, sim_o = _s5(
        u.reshape(b, s_len, D_B), sre0.reshape(b, S5_N), sim0.reshape(b, S5_N),
        p["lbre"], p["lbim"], p["wb"], p["wcre"], p["wcim"], p["s5_d"],
        p["perm"], p["permt"], p["w_glu"], p["b_glu"], tb=tb)

    x1, hf = _out_proj(out_a, out_b.reshape(t, D_B), x2d, p["w_out"],
                       p["g_post_mix"], p["g_pre_ffn"], tm)
    y = _ffn(hf, p["w_gate"], p["w_up"], p["w_down"], x1, p["g_post_ffn"], tm, tf)

    states = (
        conv_o[:, SUBLANES - (CONV_W - 1):, :],
        c_o,
        n_o[:, :A_HEADS, :],
        m_o[:, :A_HEADS, 0],
        sre_o.reshape(b, B_GROUPS, B_STATE),
        sim_o.reshape(b, B_GROUPS, B_STATE),
    )
    return y.reshape(b, s_len, D_MODEL), states


def kernel(x_prompt, x_sample, state_conv, state_mlstm_c, state_mlstm_n, state_mlstm_m, state_s5_re, state_s5_im, g_pre_mix, w_in, w_conv, b_conv, b_igate, b_fgate, g_mh, s5_lam_re, s5_lam_im, s5_log_dt, s5_b_re, s5_b_im, s5_c_re, s5_c_im, s5_d, w_glu, b_glu, w_out, g_post_mix, g_pre_ffn, w_gate, w_up, w_down, g_post_ffn):
    l = 0
    bp = x_prompt.shape[0]
    lbre, lbim, bbt_re, bbt_im = _discretise(
        s5_lam_re[l], s5_lam_im[l], s5_log_dt[l],
        jnp.transpose(s5_b_re[l], (2, 0, 1)), jnp.transpose(s5_b_im[l], (2, 0, 1)))

    o1 = QK_W
    o2 = o1 + D_A
    o3 = o2 + D_A
    o4 = o3 + A_HEADS
    o5 = o4 + A_HEADS
    w = w_in[l]
    zrows = jnp.zeros((SUBLANES - A_HEADS, D_MODEL), F32)
    zb = jnp.zeros((SUBLANES - A_HEADS,), F32)
    perm, permt = _row_permutation(bp)
    row = lambda a: a.reshape(1, -1)
    p = {
        "g_pre_mix": row(g_pre_mix[l]),
        "wqk": w[:, :o1].astype(BF16),
        "wv": w[:, o1:o2].astype(BF16),
        "wog": w[:, o2:o3].astype(BF16),
        "wu": w[:, o5:].astype(BF16),
        "wgt": jnp.concatenate([w[:, o3:o4].T, zrows, w[:, o4:o5].T, zrows], axis=0).astype(BF16),
        "gbias": jnp.concatenate([b_igate[l], zb, b_fgate[l], zb]).reshape(2 * SUBLANES, 1),
        "w_conv": w_conv[l],
        "b_conv": row(b_conv[l]),
        "g_mh": row(g_mh[l]),
        "lbre": lbre.reshape(1, S5_N),
        "lbim": lbim.reshape(1, S5_N),
        "wb": jnp.concatenate([_block_diag_in(bbt_re), _block_diag_in(bbt_im)], axis=-1).astype(BF16),
        "wcre": _block_diag_out(s5_c_re[l]).astype(BF16),
        "wcim": _block_diag_out(s5_c_im[l]).astype(BF16),
        "s5_d": row(s5_d[l]),
        "perm": perm,
        "permt": permt,
        "w_glu": w_glu[l].astype(BF16),
        "b_glu": row(b_glu[l]),
        "w_out": w_out[l].astype(BF16),
        "g_post_mix": row(g_post_mix[l]),
        "g_pre_ffn": row(g_pre_ffn[l]),
        "w_gate": w_gate[l].astype(BF16),
        "w_up": w_up[l].astype(BF16),
        "w_down": w_down[l].astype(BF16),
        "g_post_ffn": row(g_post_ffn[l]),
    }

    yp, st_p = _layer(
        x_prompt,
        jnp.zeros((bp, CONV_W - 1, QK_W), F32),
        jnp.zeros((bp, A_HEADS, A_DV, A_DQK), F32),
        jnp.zeros((bp, A_HEADS, A_DQK), F32),
        jnp.zeros((bp, A_HEADS), F32),
        jnp.zeros((bp, B_GROUPS, B_STATE), F32),
        jnp.zeros((bp, B_GROUPS, B_STATE), F32),
        p, tm=512, chunk=128, tb=64, tf=512)
    ys, st_s = _layer(
        x_sample, state_conv[l], state_mlstm_c[l], state_mlstm_n[l], state_mlstm_m[l],
        state_s5_re[l], state_s5_im[l],
        p, tm=128, chunk=128, tb=16, tf=512)
    return (yp, ys) + tuple(a[None] for a in st_p) + tuple(a[None] for a in st_s)
```

```python
import functools

import numpy as np
import jax
import jax.numpy as jnp
from jax import lax
from jax.experimental import pallas as pl
from jax.experimental.pallas import tpu as pltpu

F32 = jnp.float32
BF16 = jnp.bfloat16

D_MODEL = 2048
D_A = 1024
D_B = 1024
A_HEADS = 4
A_DV = 256
A_DQK = 128
QK_W = 1024
CONV_W = 4
B_GROUP = 16
B_GROUPS = 64
B_STATE = 64
S5_N = B_GROUPS * B_STATE
D_FF = 5632
EPS = 1e-6

SUBLANES = 8
LANES = 128
GROUPS_PER_BLOCK = LANES // B_GROUP
N_BLOCKS = B_GROUPS // GROUPS_PER_BLOCK
STATE_PER_BLOCK = GROUPS_PER_BLOCK * B_STATE
PERM_T = 16
VMEM_LIMIT = 56 * 1024 * 1024


def _sigmoid(x):
    return 1.0 / (1.0 + jnp.exp(-x))


def _gelu_tanh(x):
    c = np.sqrt(2 / np.pi).astype(np.float32)
    return x * (0.5 * (1.0 + jnp.tanh(c * (x + 0.044715 * (x ** 3)))))


def _rms(x, g):
    return x * lax.rsqrt(jnp.mean(x * x, axis=-1, keepdims=True) + EPS) * g


def _dot(a, b):
    return jnp.dot(a, b, preferred_element_type=F32)


def _row_blocks(rows, n_sub):
    assert rows % n_sub == 0
    size = rows // n_sub
    return [slice(size * r, size * (r + 1)) for r in range(n_sub)]


def _resident(shape):
    zeros = (0,) * len(shape)
    return pl.BlockSpec(shape, lambda *_: zeros, pipeline_mode=pl.Buffered(1))


def _disc_kernel(lam_re_ref, lam_im_ref, logdt_ref, bre_ref, bim_ref,
                 lbre_ref, lbim_ref, bbre_ref, bbim_ref):
    lam_re = lam_re_ref[...]
    lam_im = lam_im_ref[...]
    dt = jnp.exp(logdt_ref[...])
    mag = jnp.exp(lam_re * dt)
    ang = lam_im * dt
    lb_re = mag * jnp.cos(ang)
    lb_im = mag * jnp.sin(ang)
    den = lam_re * lam_re + lam_im * lam_im
    nr = lb_re - 1.0
    f_re = (nr * lam_re + lb_im * lam_im) / den
    f_im = (lb_im * lam_re - nr * lam_im) / den
    lbre_ref[...] = lb_re
    lbim_ref[...] = lb_im
    b_re = bre_ref[...]
    b_im = bim_ref[...]
    bbre_ref[...] = f_re[None] * b_re - f_im[None] * b_im
    bbim_ref[...] = f_re[None] * b_im + f_im[None] * b_re


def _discretise(lam_re, lam_im, log_dt, bt_re, bt_im):
    gp = jax.ShapeDtypeStruct((B_GROUPS, B_STATE), F32)
    igp = jax.ShapeDtypeStruct((B_GROUP, B_GROUPS, B_STATE), F32)
    return pl.pallas_call(
        _disc_kernel, out_shape=(gp, gp, igp, igp), name="s5_discretise",
    )(lam_re, lam_im, log_dt.reshape(B_GROUPS, 1), bt_re, bt_im)


def _inproj_kernel(x_ref, gpre_ref, wqk_ref, wv_ref, wog_ref, wu_ref, wgt_ref, gb_ref,
                   qk_out, v_out, og_out, u_out, g_out, *, n_sub):
    for rows in _row_blocks(x_ref.shape[0], n_sub):
        hm = _rms(x_ref[rows, :], gpre_ref[...]).astype(BF16)
        qk_out[rows, :] = _dot(hm, wqk_ref[...])
        v_out[rows, :] = _dot(hm, wv_ref[...]).astype(BF16)
        og_out[rows, :] = _dot(hm, wog_ref[...])
        u_out[rows, :] = _dot(hm, wu_ref[...])
        z = lax.dot_general(wgt_ref[...], hm, (((1,), (1,)), ((), ())),
                            preferred_element_type=F32) + gb_ref[...]
        log_sig = jnp.minimum(z, 0.0) - jnp.log1p(jnp.exp(-jnp.abs(z)))
        row = lax.broadcasted_iota(jnp.int32, z.shape, 0)
        g_out[:, rows] = jnp.where(row >= SUBLANES, log_sig, z)


def _in_proj(x2d, g_pre, wqk, wv, wog, wu, wgt, gbias, tm, n_sub):
    t = x2d.shape[0]
    tok = lambda i: (i, 0)
    return pl.pallas_call(
        functools.partial(_inproj_kernel, n_sub=n_sub),
        grid=(t // tm,),
        in_specs=[
            pl.BlockSpec((tm, D_MODEL), tok),
            _resident((1, D_MODEL)),
            _resident((D_MODEL, QK_W)),
            _resident((D_MODEL, D_A)),
            _resident((D_MODEL, D_A)),
            _resident((D_MODEL, D_B)),
            _resident((2 * SUBLANES, D_MODEL)),
            _resident((2 * SUBLANES, 1)),
        ],
        out_specs=[
            pl.BlockSpec((tm, QK_W), tok),
            pl.BlockSpec((tm, D_A), tok),
            pl.BlockSpec((tm, D_A), tok),
            pl.BlockSpec((tm, D_B), tok),
            pl.BlockSpec((2 * SUBLANES, tm), lambda i: (0, i)),
        ],
        out_shape=(
            jax.ShapeDtypeStruct((t, QK_W), F32),
            jax.ShapeDtypeStruct((t, D_A), BF16),
            jax.ShapeDtypeStruct((t, D_A), F32),
            jax.ShapeDtypeStruct((t, D_B), F32),
            jax.ShapeDtypeStruct((2 * SUBLANES, t), F32),
        ),
        compiler_params=pltpu.CompilerParams(
            dimension_semantics=("arbitrary",), vmem_limit_bytes=VMEM_LIMIT),
        name="in_proj",
    )(x2d, g_pre, wqk, wv, wog, wu, wgt, gbias)


def _mlstm_kernel(qk_ref, v_ref, og_ref, g_ref, tail0_ref, c0_ref, n0_ref, m0_ref,
                  wconv_ref, bconv_ref, gmh_ref,
                  outa_ref, convo_ref, co_ref, no_ref, mo_ref,
                  xp, ct, n_s, m_s, *, chunk, valid, n_chunks):
    L = chunk
    c = pl.program_id(1)

    @pl.when(c == 0)
    def _():
        xp[0:SUBLANES, :] = tail0_ref[0]
        for h in range(A_HEADS):
            ct[h] = c0_ref[0, h].T
        n_s[...] = n0_ref[0]
        m_s[...] = m0_ref[0]

    xp[SUBLANES:SUBLANES + L, :] = qk_ref[0]
    acc = bconv_ref[...]
    for j in range(CONV_W):
        lo = SUBLANES - (CONV_W - 1) + j
        acc = acc + xp[lo:lo + L, :] * wconv_ref[j:j + 1, :]
    qk = acc * _sigmoid(acc)

    @pl.when(c == n_chunks - 1)
    def _():
        convo_ref[0] = xp[valid:valid + SUBLANES, :]

    xp[0:SUBLANES, :] = xp[L:L + SUBLANES, :]

    g = g_ref[...]
    li = g[0:SUBLANES]
    lf = g[SUBLANES:2 * SUBLANES]
    lane = lax.broadcasted_iota(jnp.int32, (SUBLANES, L), 1)
    if valid < L:
        li = jnp.where(lane < valid, li, -jnp.inf)
        lf = jnp.where(lane < valid, lf, 0.0)
    fc = lf
    sh = 1
    while sh < L:
        fc = fc + jnp.where(lane >= sh, pltpu.roll(fc, sh, axis=1), 0.0)
        sh *= 2
    m0 = m_s[:, 0:1]
    f_last = fc[:, L - 1:L]
    gr = li - fc
    lw = f_last + gr
    m_new = jnp.maximum(f_last + m0, jnp.max(lw, axis=1, keepdims=True))
    ws = jnp.exp(lw - m_new)
    decay = jnp.exp(f_last + m0 - m_new)

    row_i = lax.broadcasted_iota(jnp.int32, (L, L), 0)
    col_i = lax.broadcasted_iota(jnp.int32, (L, L), 1)
    tri = col_i <= row_i
    eye = col_i == row_i
    head_row = lax.broadcasted_iota(jnp.int32, (SUBLANES, A_DQK), 0)
    n_old = n_s[...]
    n_new = n_old
    q_all = qk[:, :QK_W // 2] * (A_DQK ** -0.5)
    for h in range(A_HEADS):
        qf = q_all[:, A_DQK * h:A_DQK * (h + 1)]
        kf = qk[:, QK_W // 2 + A_DQK * h:QK_W // 2 + A_DQK * (h + 1)]
        qb = qf.astype(BF16)
        kb = kf.astype(BF16)
        vb = v_ref[0, :, A_DV * h:A_DV * (h + 1)]
        fc_h = fc[h:h + 1, :]
        fc_col = jnp.sum(jnp.where(eye, fc_h, 0.0), axis=1, keepdims=True)
        m0_h = m0[h:h + 1, :]
        dmat = jnp.where(tri, fc_col + gr[h:h + 1, :], -jnp.inf)
        inter = fc_col + m0_h
        m_t = jnp.maximum(jnp.max(dmat, axis=1, keepdims=True), inter)
        s = lax.dot_general(qb, kb, (((1,), (1,)), ((), ())),
                            preferred_element_type=F32) * jnp.exp(dmat - m_t)
        g_int = jnp.exp(inter - m_t)
        ct_h = ct[h]
        num = _dot(s.astype(BF16), vb) + g_int * _dot(qb, ct_h.astype(BF16))
        den = (jnp.sum(s, axis=1, keepdims=True)
               + g_int * jnp.sum(qf * n_old[h:h + 1, :], axis=1, keepdims=True))
        scale = jnp.maximum(jnp.abs(den), jnp.exp(-m_t))
        hh = num / scale
        hn = _rms(hh, gmh_ref[:, A_DV * h:A_DV * (h + 1)])
        og = og_ref[0, :, A_DV * h:A_DV * (h + 1)]
        outa_ref[0, :, A_DV * h:A_DV * (h + 1)] = (hn * _sigmoid(og)).astype(BF16)
        kw = (kf.T * ws[h:h + 1, :]).astype(BF16)
        ct[h] = decay[h:h + 1, :] * ct_h + _dot(kw, vb)
        n_upd = decay * n_old + _dot(ws, kf)
        n_new = jnp.where(head_row == h, n_upd, n_new)
    n_s[...] = n_new
    m_s[...] = jnp.broadcast_to(m_new, m_s.shape)

    @pl.when(c == n_chunks - 1)
    def _():
        for h in range(A_HEADS):
            co_ref[0, h] = ct[h].T
        no_ref[0] = n_s[...]
        mo_ref[0] = m_s[...]


def _mlstm(qk_raw, v, og, gates, tail0, c0, n0, m0, wconv, bconv, gmh, *, chunk, valid):
    b, s_len, _ = qk_raw.shape
    n_chunks = s_len // chunk
    seq = lambda i, c: (i, c, 0)
    per_b3 = lambda i, c: (i, 0, 0)
    kernel = functools.partial(_mlstm_kernel, chunk=chunk, valid=valid, n_chunks=n_chunks)
    return pl.pallas_call(
        kernel,
        grid=(b, n_chunks),
        in_specs=[
            pl.BlockSpec((1, chunk, QK_W), seq),
            pl.BlockSpec((1, chunk, D_A), seq),
            pl.BlockSpec((1, chunk, D_A), seq),
            pl.BlockSpec((2 * SUBLANES, chunk), lambda i, c: (0, i * n_chunks + c)),
            pl.BlockSpec((1, SUBLANES, QK_W), per_b3),
            pl.BlockSpec((1, A_HEADS, A_DV, A_DQK), lambda i, c: (i, 0, 0, 0)),
            pl.BlockSpec((1, SUBLANES, A_DQK), per_b3),
            pl.BlockSpec((1, SUBLANES, LANES), per_b3),
            pl.BlockSpec((CONV_W, QK_W), lambda i, c: (0, 0)),
            pl.BlockSpec((1, QK_W), lambda i, c: (0, 0)),
            pl.BlockSpec((1, D_A), lambda i, c: (0, 0)),
        ],
        out_specs=[
            pl.BlockSpec((1, chunk, D_A), seq),
            pl.BlockSpec((1, SUBLANES, QK_W), per_b3),
            pl.BlockSpec((1, A_HEADS, A_DV, A_DQK), lambda i, c: (i, 0, 0, 0)),
            pl.BlockSpec((1, SUBLANES, A_DQK), per_b3),
            pl.BlockSpec((1, SUBLANES, LANES), per_b3),
        ],
        out_shape=(
            jax.ShapeDtypeStruct((b, s_len, D_A), BF16),
            jax.ShapeDtypeStruct((b, SUBLANES, QK_W), F32),
            jax.ShapeDtypeStruct((b, A_HEADS, A_DV, A_DQK), F32),
            jax.ShapeDtypeStruct((b, SUBLANES, A_DQK), F32),
            jax.ShapeDtypeStruct((b, SUBLANES, LANES), F32),
        ),
        scratch_shapes=[
            pltpu.VMEM((chunk + SUBLANES, QK_W), F32),
            pltpu.VMEM((A_HEADS, A_DQK, A_DV), F32),
            pltpu.VMEM((SUBLANES, A_DQK), F32),
            pltpu.VMEM((SUBLANES, LANES), F32),
        ],
        compiler_params=pltpu.CompilerParams(
            dimension_semantics=("arbitrary", "arbitrary"), vmem_limit_bytes=VMEM_LIMIT),
        name="mlstm",
    )(qk_raw, v, og, gates, tail0, c0, n0, m0, wconv, bconv, gmh)


def _s5_kernel(u_ref, s0re_ref, s0im_ref, lbre_ref, lbim_ref, wb_ref, wcre_ref, wcim_ref,
               d_ref, perm_ref, permt_ref, wglu_ref, bglu_ref,
               outb_ref, sre_out, sim_out,
               bure, buim, sre, sim, utb, xtb, ytb, *, tb, n_steps):
    step = pl.program_id(0)
    batch = u_ref.shape[0]
    rows_j = batch * PERM_T
    n_j = tb // PERM_T

    @pl.when(step == 0)
    def _():
        sre[...] = s0re_ref[...]
        sim[...] = s0im_ref[...]

    for j in range(n_j):
        xj = jnp.concatenate(
            [u_ref[b, PERM_T * j:PERM_T * (j + 1), :] for b in range(batch)], axis=0)
        hi = xj.astype(BF16)
        lo = (xj - hi.astype(F32)).astype(BF16)
        uh = _dot(perm_ref[...], hi)
        rows = slice(rows_j * j, rows_j * (j + 1))
        xtb[rows, :] = uh.astype(BF16)
        utb[rows, :] = uh + _dot(perm_ref[...], lo)

    strip = 8 * LANES
    blocks_per_strip = strip // STATE_PER_BLOCK
    for st in range(S5_N // strip):
        cols = slice(strip * st, strip * (st + 1))
        blocks = range(blocks_per_strip * st, blocks_per_strip * (st + 1))
        for k in blocks:
            r = _dot(xtb[:, LANES * k:LANES * (k + 1)], wb_ref[k])
            kc = slice(STATE_PER_BLOCK * k, STATE_PER_BLOCK * (k + 1))
            bure[:, kc] = r[:, :STATE_PER_BLOCK]
            buim[:, kc] = r[:, STATE_PER_BLOCK:]
        a_re = jnp.broadcast_to(lbre_ref[:, cols], (batch, strip))
        a_im = jnp.broadcast_to(lbim_ref[:, cols], (batch, strip))
        s_r = sre[:, cols]
        s_i = sim[:, cols]
        for t in range(tb):
            tr = slice(t * batch, (t + 1) * batch)
            n_r = a_re * s_r - a_im * s_i + bure[tr, cols]
            n_i = a_re * s_i + a_im * s_r + buim[tr, cols]
            bure[tr, cols] = n_r
            buim[tr, cols] = n_i
            s_r, s_i = n_r, n_i
        sre[:, cols] = s_r
        sim[:, cols] = s_i
        for k in blocks:
            kc = slice(STATE_PER_BLOCK * k, STATE_PER_BLOCK * (k + 1))
            ytb[:, LANES * k:LANES * (k + 1)] = (
                _dot(bure[:, kc].astype(BF16), wcre_ref[k])
                - _dot(buim[:, kc].astype(BF16), wcim_ref[k]))

    z = _gelu_tanh(ytb[...] + d_ref[...] * utb[...])
    gate = _sigmoid(_dot(z.astype(BF16), wglu_ref[...]) + bglu_ref[...])
    res = (z * gate).astype(BF16)
    for j in range(n_j):
        rb = _dot(permt_ref[...], res[rows_j * j:rows_j * (j + 1), :]).astype(BF16)
        for b in range(batch):
            outb_ref[b, PERM_T * j:PERM_T * (j + 1), :] = rb[PERM_T * b:PERM_T * (b + 1), :]

    @pl.when(step == n_steps - 1)
    def _():
        sre_out[...] = sre[...]
        sim_out[...] = sim[...]


def _s5(u, s0re, s0im, lbre, lbim, wb, wcre, wcim, d_skip, perm, permt, wglu, bglu, *, tb):
    b, s_len, _ = u.shape
    n_steps = s_len // tb
    rows = b * tb
    kernel = functools.partial(_s5_kernel, tb=tb, n_steps=n_steps)
    state = jax.ShapeDtypeStruct((b, S5_N), F32)
    return pl.pallas_call(
        kernel,
        grid=(n_steps,),
        in_specs=[
            pl.BlockSpec((b, tb, D_B), lambda i: (0, i, 0)),
            _resident((b, S5_N)),
            _resident((b, S5_N)),
            _resident((1, S5_N)),
            _resident((1, S5_N)),
            _resident((N_BLOCKS, LANES, 2 * STATE_PER_BLOCK)),
            _resident((N_BLOCKS, STATE_PER_BLOCK, LANES)),
            _resident((N_BLOCKS, STATE_PER_BLOCK, LANES)),
            _resident((1, D_B)),
            _resident((b * PERM_T, b * PERM_T)),
            _resident((b * PERM_T, b * PERM_T)),
            _resident((D_B, D_B)),
            _resident((1, D_B)),
        ],
        out_specs=[
            pl.BlockSpec((b, tb, D_B), lambda i: (0, i, 0)),
            pl.BlockSpec((b, S5_N), lambda i: (0, 0)),
            pl.BlockSpec((b, S5_N), lambda i: (0, 0)),
        ],
        out_shape=(jax.ShapeDtypeStruct((b, s_len, D_B), BF16), state, state),
        scratch_shapes=[
            pltpu.VMEM((rows, S5_N), F32),
            pltpu.VMEM((rows, S5_N), F32),
            pltpu.VMEM((b, S5_N), F32),
            pltpu.VMEM((b, S5_N), F32),
            pltpu.VMEM((rows, D_B), F32),
            pltpu.VMEM((rows, D_B), BF16),
            pltpu.VMEM((rows, D_B), F32),
        ],
        compiler_params=pltpu.CompilerParams(
            dimension_semantics=("arbitrary",), vmem_limit_bytes=VMEM_LIMIT),
        name="s5",
    )(u, s0re, s0im, lbre, lbim, wb, wcre, wcim, d_skip, perm, permt, wglu, bglu)


def _outproj_kernel(a_ref, b_ref, x_ref, wo_ref, gpost_ref, gffn_ref, x1_ref, hf_ref, *, n_sub):
    for rows in _row_blocks(a_ref.shape[0], n_sub):
        mix = (_dot(a_ref[rows, :], wo_ref[0:D_A, :])
               + _dot(b_ref[rows, :], wo_ref[D_A:D_A + D_B, :]))
        x1 = x_ref[rows, :] + _rms(mix, gpost_ref[...])
        x1_ref[rows, :] = x1
        hf_ref[rows, :] = _rms(x1, gffn_ref[...]).astype(BF16)


def _out_proj(a, b, x2d, wo, g_post, g_ffn, tm, n_sub):
    t = x2d.shape[0]
    tok = lambda i: (i, 0)
    return pl.pallas_call(
        functools.partial(_outproj_kernel, n_sub=n_sub),
        grid=(t // tm,),
        in_specs=[
            pl.BlockSpec((tm, D_A), tok),
            pl.BlockSpec((tm, D_B), tok),
            pl.BlockSpec((tm, D_MODEL), tok),
            _resident((D_A + D_B, D_MODEL)),
            _resident((1, D_MODEL)),
            _resident((1, D_MODEL)),
        ],
        out_specs=[pl.BlockSpec((tm, D_MODEL), tok), pl.BlockSpec((tm, D_MODEL), tok)],
        out_shape=(jax.ShapeDtypeStruct((t, D_MODEL), F32),
                   jax.ShapeDtypeStruct((t, D_MODEL), BF16)),
        compiler_params=pltpu.CompilerParams(
            dimension_semantics=("arbitrary",), vmem_limit_bytes=VMEM_LIMIT),
        name="out_proj",
    )(a, b, x2d, wo, g_post, g_ffn)


def _ffn_up_kernel(hfp_ref, hfs_ref, wg_ref, wu_ref, actp_ref, acts_ref, wgb, wub, *, n_p):
    m = pl.program_id(1)

    @pl.when(m == 0)
    def _():
        wgb[...] = wg_ref[...].astype(BF16)
        wub[...] = wu_ref[...].astype(BF16)

    def body(hf_ref, act_ref):
        hf = hf_ref[...]
        a = _dot(hf, wgb[...])
        act_ref[...] = (a * _sigmoid(a) * _dot(hf, wub[...])).astype(BF16)

    @pl.when(m < n_p)
    def _():
        body(hfp_ref, actp_ref)

    @pl.when(m == n_p)
    def _():
        body(hfs_ref, acts_ref)


def _ffn_up(hf_p, hf_s, wg, wu, tm, tf):
    t_p, t_s = hf_p.shape[0], hf_s.shape[0]
    n_p = t_p // tm
    n_f = D_FF // tf
    last = n_p - 1
    return pl.pallas_call(
        functools.partial(_ffn_up_kernel, n_p=n_p),
        grid=(n_f, n_p + 1),
        in_specs=[
            pl.BlockSpec((tm, D_MODEL), lambda f, m: (jnp.minimum(m, last), 0)),
            pl.BlockSpec((t_s, D_MODEL), lambda f, m: (0, 0)),
            pl.BlockSpec((D_MODEL, tf), lambda f, m: (0, f)),
            pl.BlockSpec((D_MODEL, tf), lambda f, m: (0, f)),
        ],
        out_specs=[
            pl.BlockSpec((tm, tf), lambda f, m: (jnp.minimum(m, last), f)),
            pl.BlockSpec((t_s, tf), lambda f, m: (0, f)),
        ],
        out_shape=(jax.ShapeDtypeStruct((t_p, D_FF), BF16),
                   jax.ShapeDtypeStruct((t_s, D_FF), BF16)),
        scratch_shapes=[pltpu.VMEM((D_MODEL, tf), BF16), pltpu.VMEM((D_MODEL, tf), BF16)],
        compiler_params=pltpu.CompilerParams(
            dimension_semantics=("arbitrary", "arbitrary"), vmem_limit_bytes=VMEM_LIMIT),
        name="ffn_up",
    )(hf_p, hf_s, wg, wu)


def _ffn_down_kernel(act_ref, wd_ref, x1_ref, gp_ref, out_ref, *, n_sub):
    for rows in _row_blocks(act_ref.shape[0], n_sub):
        ff = _dot(act_ref[rows, :], wd_ref[...])
        out_ref[rows, :] = x1_ref[rows, :] + _rms(ff, gp_ref[...])


def _ffn_down(act, wd, x1, g_post, tm, n_sub):
    t = act.shape[0]
    tok = lambda i: (i, 0)
    return pl.pallas_call(
        functools.partial(_ffn_down_kernel, n_sub=n_sub),
        grid=(t // tm,),
        in_specs=[
            pl.BlockSpec((tm, D_FF), tok),
            _resident((D_FF, D_MODEL)),
            pl.BlockSpec((tm, D_MODEL), tok),
            _resident((1, D_MODEL)),
        ],
        out_specs=pl.BlockSpec((tm, D_MODEL), tok),
        out_shape=jax.ShapeDtypeStruct((t, D_MODEL), F32),
        compiler_params=pltpu.CompilerParams(
            dimension_semantics=("arbitrary",), vmem_limit_bytes=VMEM_LIMIT),
        name="ffn_down",
    )(act, wd, x1, g_post)


def _block_diag_in(bbt):
    x = bbt.reshape(B_GROUP, N_BLOCKS, GROUPS_PER_BLOCK, B_STATE)
    x = jnp.transpose(x, (1, 2, 0, 3))
    eye = jnp.eye(GROUPS_PER_BLOCK, dtype=bool)[None, :, None, :, None]
    y = jnp.where(eye, x[:, :, :, None, :], 0.0)
    return y.reshape(N_BLOCKS, LANES, STATE_PER_BLOCK)


def _block_diag_out(c):
    x = c.reshape(N_BLOCKS, GROUPS_PER_BLOCK, B_GROUP, B_STATE)
    x = jnp.transpose(x, (0, 1, 3, 2))
    eye = jnp.eye(GROUPS_PER_BLOCK, dtype=bool)[None, :, None, :, None]
    y = jnp.where(eye, x[:, :, :, None, :], 0.0)
    return y.reshape(N_BLOCKS, STATE_PER_BLOCK, LANES)


def _row_permutation(batch):
    n = batch * PERM_T
    src = np.arange(n)
    b, t = src // PERM_T, src % PERM_T
    p = np.zeros((n, n), np.float32)
    p[t * batch + b, src] = 1.0
    return jnp.asarray(p, BF16), jnp.asarray(p.T, BF16)


def _pad_rows_front(x, rows):
    pad = [(0, 0)] * x.ndim
    pad[-2] = (rows - x.shape[-2], 0)
    return jnp.pad(x, pad)


def _pad_rows_back(x, rows):
    pad = [(0, 0)] * x.ndim
    pad[-2] = (0, rows - x.shape[-2])
    return jnp.pad(x, pad)


def _mix_layer(x, conv0, c0, n0, m0, sre0, sim0, p, *, tm, n_sub, chunk, tb):
    b, s_len, _ = x.shape
    t = b * s_len
    x2d = x.reshape(t, D_MODEL)
    qk_raw, v, og, u, gates = _in_proj(
        x2d, p["g_pre_mix"], p["wqk"], p["wv"], p["wog"], p["wu"], p["wgt"], p["gbias"],
        tm, n_sub)

    assert s_len % chunk == 0 or s_len < chunk
    s_pad = -(-s_len // chunk) * chunk
    def seq3(a):
        a = a.reshape(b, s_len, a.shape[-1])
        return a if s_pad == s_len else _pad_rows_back(a, s_pad)
    g3 = gates.reshape(2 * SUBLANES, b, s_len)
    if s_pad != s_len:
        g3 = jnp.pad(g3, ((0, 0), (0, 0), (0, s_pad - s_len)))
    out_a, conv_o, c_o, n_o, m_o = _mlstm(
        seq3(qk_raw), seq3(v), seq3(og), g3.reshape(2 * SUBLANES, b * s_pad),
        _pad_rows_front(conv0, SUBLANES), c0, _pad_rows_back(n0, SUBLANES),
        jnp.broadcast_to(_pad_rows_back(m0[..., None], SUBLANES), (b, SUBLANES, LANES)),
        p["w_conv"], p["b_conv"], p["g_mh"], chunk=chunk, valid=min(s_len, chunk))
    out_a = out_a[:, :s_len].reshape(t, D_A)

    out_b, sre_o, sim_o = _s5(
        u.reshape(b, s_len, D_B), sre0.reshape(b, S5_N), sim0.reshape(b, S5_N),
        p["lbre"], p["lbim"], p["wb"], p["wcre"], p["wcim"], p["s5_d"],
        p["perm"], p["permt"], p["w_glu"], p["b_glu"], tb=tb)

    x1, hf = _out_proj(out_a, out_b.reshape(t, D_B), x2d, p["w_out"],
                       p["g_post_mix"], p["g_pre_ffn"], tm, n_sub)
    states = (
        conv_o[:, SUBLANES - (CONV_W - 1):, :],
        c_o,
        n_o[:, :A_HEADS, :],
        m_o[:, :A_HEADS, 0],
        sre_o.reshape(b, B_GROUPS, B_STATE),
        sim_o.reshape(b, B_GROUPS, B_STATE),
    )
    return x1, hf, states


def kernel(x_prompt, x_sample, state_conv, state_mlstm_c, state_mlstm_n, state_mlstm_m, state_s5_re, state_s5_im, g_pre_mix, w_in, w_conv, b_conv, b_igate, b_fgate, g_mh, s5_lam_re, s5_lam_im, s5_log_dt, s5_b_re, s5_b_im, s5_c_re, s5_c_im, s5_d, w_glu, b_glu, w_out, g_post_mix, g_pre_ffn, w_gate, w_up, w_down, g_post_ffn):
    l = 0
    bp = x_prompt.shape[0]
    lbre, lbim, bbt_re, bbt_im = _discretise(
        s5_lam_re[l], s5_lam_im[l], s5_log_dt[l],
        jnp.transpose(s5_b_re[l], (2, 0, 1)), jnp.transpose(s5_b_im[l], (2, 0, 1)))

    o1 = QK_W
    o2 = o1 + D_A
    o3 = o2 + D_A
    o4 = o3 + A_HEADS
    o5 = o4 + A_HEADS
    w = w_in[l]
    zrows = jnp.zeros((SUBLANES - A_HEADS, D_MODEL), F32)
    zb = jnp.zeros((SUBLANES - A_HEADS,), F32)
    perm, permt = _row_permutation(bp)
    row = lambda a: a.reshape(1, -1)
    p = {
        "g_pre_mix": row(g_pre_mix[l]),
        "wqk": w[:, :o1].astype(BF16),
        "wv": w[:, o1:o2].astype(BF16),
        "wog": w[:, o2:o3].astype(BF16),
        "wu": w[:, o5:].astype(BF16),
        "wgt": jnp.concatenate([w[:, o3:o4].T, zrows, w[:, o4:o5].T, zrows], axis=0).astype(BF16),
        "gbias": jnp.concatenate([b_igate[l], zb, b_fgate[l], zb]).reshape(2 * SUBLANES, 1),
        "w_conv": w_conv[l],
        "b_conv": row(b_conv[l]),
        "g_mh": row(g_mh[l]),
        "lbre": lbre.reshape(1, S5_N),
        "lbim": lbim.reshape(1, S5_N),
        "wb": jnp.concatenate([_block_diag_in(bbt_re), _block_diag_in(bbt_im)], axis=-1).astype(BF16),
        "wcre": _block_diag_out(s5_c_re[l]).astype(BF16),
        "wcim": _block_diag_out(s5_c_im[l]).astype(BF16),
        "s5_d": row(s5_d[l]),
        "perm": perm,
        "permt": permt,
        "w_glu": w_glu[l].astype(BF16),
        "b_glu": row(b_glu[l]),
        "w_out": w_out[l].astype(BF16),
        "g_post_mix": row(g_post_mix[l]),
        "g_pre_ffn": row(g_pre_ffn[l]),
    }

    x1_p, hf_p, st_p = _mix_layer(
        x_prompt,
        jnp.zeros((bp, CONV_W - 1, QK_W), F32),
        jnp.zeros((bp, A_HEADS, A_DV, A_DQK), F32),
        jnp.zeros((bp, A_HEADS, A_DQK), F32),
        jnp.zeros((bp, A_HEADS), F32),
        jnp.zeros((bp, B_GROUPS, B_STATE), F32),
        jnp.zeros((bp, B_GROUPS, B_STATE), F32),
        p, tm=512, n_sub=2, chunk=128, tb=64)
    x1_s, hf_s, st_s = _mix_layer(
        x_sample, state_conv[l], state_mlstm_c[l], state_mlstm_n[l], state_mlstm_m[l],
        state_s5_re[l], state_s5_im[l],
        p, tm=128, n_sub=1, chunk=128, tb=16)

    act_p, act_s = _ffn_up(hf_p, hf_s, w_gate[l], w_up[l], tm=1024, tf=512)
    wd = w_down[l].astype(BF16)
    g_post = row(g_post_ffn[l])
    yp = _ffn_down(act_p, wd, x1_p, g_post, tm=512, n_sub=2).reshape(x_prompt.shape)
    ys = _ffn_down(act_s, wd, x1_s, g_post, tm=128, n_sub=1).reshape(x_sample.shape)
    return (yp, ys) + tuple(a[None] for a in st_p) + tuple(a[None] for a in st_s)
```

```python
import functools

import numpy as np
import jax
import jax.numpy as jnp
from jax import lax
from jax.experimental import pallas as pl
from jax.experimental.pallas import tpu as pltpu

F32 = jnp.float32
BF16 = jnp.bfloat16

D_MODEL = 2048
D_A = 1024
D_B = 1024
A_HEADS = 4
A_DV = 256
A_DQK = 128
QK_W = 1024
CONV_W = 4
B_GROUP = 16
B_GROUPS = 64
B_STATE = 64
S5_N = B_GROUPS * B_STATE
D_FF = 5632
EPS = 1e-6

SUBLANES = 8
LANES = 128
GROUPS_PER_BLOCK = LANES // B_GROUP
N_BLOCKS = B_GROUPS // GROUPS_PER_BLOCK
STATE_PER_BLOCK = GROUPS_PER_BLOCK * B_STATE
PERM_T = 16
VMEM_LIMIT = 56 * 1024 * 1024


def _sigmoid(x):
    return 1.0 / (1.0 + jnp.exp(-x))


def _gelu_tanh(x):
    c = np.sqrt(2 / np.pi).astype(np.float32)
    return x * (0.5 * (1.0 + jnp.tanh(c * (x + 0.044715 * (x ** 3)))))


def _rms(x, g):
    return x * lax.rsqrt(jnp.mean(x * x, axis=-1, keepdims=True) + EPS) * g


def _dot(a, b):
    return jnp.dot(a, b, preferred_element_type=F32)


def _log_sigmoid(z):
    return jnp.minimum(z, 0.0) - jnp.log1p(jnp.exp(-jnp.abs(z)))


def _split3(x):
    hi = x.astype(BF16)
    r = x - hi.astype(F32)
    mid = r.astype(BF16)
    lo = (r - mid.astype(F32)).astype(BF16)
    return hi, mid, lo


def _dot3_right(x, sel):
    hi, mid, lo = _split3(x)
    return _dot(hi, sel) + _dot(mid, sel) + _dot(lo, sel)


def _dot3_left(sel, x):
    hi, mid, lo = _split3(x)
    return _dot(sel, hi) + _dot(sel, mid) + _dot(sel, lo)


def _row_blocks(rows, n_sub):
    assert rows % n_sub == 0
    size = rows // n_sub
    return [slice(size * r, size * (r + 1)) for r in range(n_sub)]


def _resident(shape):
    zeros = (0,) * len(shape)
    return pl.BlockSpec(shape, lambda *_: zeros, pipeline_mode=pl.Buffered(1))


def _disc_kernel(lam_re_ref, lam_im_ref, logdt_ref, bre_ref, bim_ref,
                 lbre_ref, lbim_ref, bbre_ref, bbim_ref):
    lam_re = lam_re_ref[...]
    lam_im = lam_im_ref[...]
    dt = jnp.exp(logdt_ref[...])
    mag = jnp.exp(lam_re * dt)
    ang = lam_im * dt
    lb_re = mag * jnp.cos(ang)
    lb_im = mag * jnp.sin(ang)
    den = lam_re * lam_re + lam_im * lam_im
    nr = lb_re - 1.0
    f_re = (nr * lam_re + lb_im * lam_im) / den
    f_im = (lb_im * lam_re - nr * lam_im) / den
    lbre_ref[...] = lb_re
    lbim_ref[...] = lb_im
    b_re = bre_ref[...]
    b_im = bim_ref[...]
    bbre_ref[...] = f_re[None] * b_re - f_im[None] * b_im
    bbim_ref[...] = f_re[None] * b_im + f_im[None] * b_re


def _discretise(lam_re, lam_im, log_dt, bt_re, bt_im):
    gp = jax.ShapeDtypeStruct((B_GROUPS, B_STATE), F32)
    igp = jax.ShapeDtypeStruct((B_GROUP, B_GROUPS, B_STATE), F32)
    return pl.pallas_call(
        _disc_kernel, out_shape=(gp, gp, igp, igp), name="s5_discretise",
    )(lam_re, lam_im, log_dt.reshape(B_GROUPS, 1), bt_re, bt_im)


GATE_LANE_CMX = SUBLANES
GATE_LANE_FC = SUBLANES + A_HEADS


def _inproj_kernel(x_ref, gpre_ref, w_ref, wgt_ref, gb_ref, triu_ref, tril_ref,
                   qk_out, v_out, og_out, u_out, g_out, c_out, *, n_sub, seg):
    for rows in _row_blocks(x_ref.shape[0], n_sub):
        hm = _rms(x_ref[rows, :], gpre_ref[...]).astype(BF16)

        n = rows.stop - rows.start
        z = lax.dot_general(wgt_ref[...], hm, (((1,), (1,)), ((), ())),
                            preferred_element_type=F32) + gb_ref[...]
        qk_out[rows, :] = _dot(hm, w_ref[:, 0:QK_W])

        row = lax.broadcasted_iota(jnp.int32, z.shape, 0)
        zf = jnp.where(row >= SUBLANES, _log_sigmoid(z), z)
        zc = jnp.concatenate([zf, jnp.zeros((LANES - 2 * SUBLANES, n), F32)], axis=0).T
        lane = lax.broadcasted_iota(jnp.int32, zc.shape, 1)
        lf_c = jnp.where((lane >= SUBLANES) & (lane < SUBLANES + A_HEADS), zc, 0.0)
        fc = _dot3_right(zf, triu_ref[...])[SUBLANES:2 * SUBLANES]
        fc_c = _dot3_left(tril_ref[...], lf_c)
        v_out[rows, :] = _dot(hm, w_ref[:, QK_W:QK_W + D_A]).astype(BF16)
        og_out[rows, :] = _dot(hm, w_ref[:, QK_W + D_A:QK_W + 2 * D_A])
        u_out[rows, :] = _dot(hm, w_ref[:, QK_W + 2 * D_A:QK_W + 2 * D_A + D_B])

        g_out[0:SUBLANES, rows] = zf[0:SUBLANES] - fc
        g_out[SUBLANES:2 * SUBLANES, rows] = fc
        cm = pltpu.roll(zc, SUBLANES, axis=1) - fc_c
        t_in = lax.broadcasted_iota(jnp.int32, zc.shape, 0) & (seg - 1)
        sh = 1
        while sh < seg:
            cm = jnp.where(t_in >= sh, jnp.maximum(cm, pltpu.roll(cm, sh, axis=0)), cm)
            sh *= 2
        c_out[rows, :] = jnp.where(lane < GATE_LANE_FC, cm, pltpu.roll(fc_c, A_HEADS, axis=1))


def _segment_prefix_matrices(n, seg):
    idx = np.arange(n)
    u = ((idx[:, None] // seg == idx[None, :] // seg) & (idx[:, None] <= idx[None, :]))
    u = u.astype(np.float32)
    return jnp.asarray(u, BF16), jnp.asarray(u.T, BF16)


def _in_proj(x2d, p, *, tm, n_sub, seg):
    t = x2d.shape[0]
    assert seg & (seg - 1) == 0
    triu, tril = _segment_prefix_matrices(tm // n_sub, seg)
    w_cols = QK_W + 2 * D_A + D_B
    tok = lambda i: (i, 0)
    return pl.pallas_call(
        functools.partial(_inproj_kernel, n_sub=n_sub, seg=seg),
        grid=(t // tm,),
        in_specs=[
            pl.BlockSpec((tm, D_MODEL), tok),
            _resident((1, D_MODEL)),
            _resident((D_MODEL, w_cols)),
            _resident((2 * SUBLANES, D_MODEL)),
            _resident((2 * SUBLANES, 1)),
            _resident(triu.shape),
            _resident(tril.shape),
        ],
        out_specs=[
            pl.BlockSpec((tm, QK_W), tok),
            pl.BlockSpec((tm, D_A), tok),
            pl.BlockSpec((tm, D_A), tok),
            pl.BlockSpec((tm, D_B), tok),
            pl.BlockSpec((2 * SUBLANES, tm), lambda i: (0, i)),
            pl.BlockSpec((tm, LANES), tok),
        ],
        out_shape=(
            jax.ShapeDtypeStruct((t, QK_W), F32),
            jax.ShapeDtypeStruct((t, D_A), BF16),
            jax.ShapeDtypeStruct((t, D_A), F32),
            jax.ShapeDtypeStruct((t, D_B), F32),
            jax.ShapeDtypeStruct((2 * SUBLANES, t), F32),
            jax.ShapeDtypeStruct((t, LANES), F32),
        ),
        compiler_params=pltpu.CompilerParams(
            dimension_semantics=("arbitrary",), vmem_limit_bytes=VMEM_LIMIT),
        name="in_proj",
    )(x2d, p["g_pre_mix"], p["w_in"], p["wgt"], p["gbias"], triu, tril)


def _mlstm_kernel(qk_ref, v_ref, og_ref, g_ref, gc_ref, rep_ref, tail0_ref, c0_ref, n0_ref, m0_ref,
                  wconv_ref, bconv_ref, gmh_ref,
                  outa_ref, convo_ref, co_ref, no_ref, mo_ref,
                  xp, ct, ncr, m_s, *, chunk, valid, n_chunks):
    L = chunk
    c = pl.program_id(1)
    reps = L // LANES

    @pl.when(c == 0)
    def _():
        xp[0:SUBLANES, :] = tail0_ref[0]
        for h in range(A_HEADS):
            ct[h] = c0_ref[0, h].T
            ncr[h] = jnp.broadcast_to(n0_ref[0, h:h + 1, :], (A_DQK, A_DQK)).T
        m_s[...] = m0_ref[0]

    def lanes(x, n):
        return x if n == 1 else jnp.concatenate([x] * n, axis=1)

    xp[SUBLANES:SUBLANES + L, :] = qk_ref[0]
    xs = xp[...]
    acc = bconv_ref[...]
    for j in range(CONV_W):
        back = CONV_W - 1 - j
        tap = xs if back == 0 else pltpu.roll(xs, back, axis=0)
        acc = acc + tap[SUBLANES:, :] * wconv_ref[j:j + 1, :]
    qk = acc * _sigmoid(acc)
    q_all = (qk[:, :QK_W // 2] * (A_DQK ** -0.5)).astype(BF16)
    k_all = qk[:, QK_W // 2:].astype(BF16)

    @pl.when(c == n_chunks - 1)
    def _():
        convo_ref[0] = xp[valid:valid + SUBLANES, :]

    xp[0:SUBLANES, :] = xp[L:L + SUBLANES, :]

    rep = _dot3_right(gc_ref[0], rep_ref[...])

    g = g_ref[...]
    gr = g[0:SUBLANES]
    fc = g[SUBLANES:2 * SUBLANES]
    lane = lax.broadcasted_iota(jnp.int32, (SUBLANES, L), 1)
    f_last = jnp.sum(jnp.where(lane == L - 1, fc, 0.0), axis=1, keepdims=True)
    m0 = jnp.max(m_s[...], axis=1, keepdims=True)
    lw = f_last + gr
    m_new = jnp.maximum(f_last + m0, jnp.max(lw, axis=1, keepdims=True))
    ws = jnp.exp(lw - m_new)
    decay = jnp.broadcast_to(jnp.exp(f_last + m0 - m_new), (SUBLANES, LANES))

    row_i = lax.broadcasted_iota(jnp.int32, (L, L), 0)
    col_i = lax.broadcasted_iota(jnp.int32, (L, L), 1)
    tri = col_i <= row_i
    for h in range(A_HEADS):
        qb = q_all[:, A_DQK * h:A_DQK * (h + 1)]
        kb = k_all[:, A_DQK * h:A_DQK * (h + 1)]
        vb = v_ref[0, :, A_DV * h:A_DV * (h + 1)]
        m0_h = m_s[h:h + 1, :]
        mx = jnp.maximum(rep[:, LANES * h:LANES * (h + 1)], m0_h)
        fc_t = rep[:, LANES * (A_HEADS + h):LANES * (A_HEADS + h + 1)]
        e = jnp.exp(jnp.where(tri, gr[h:h + 1, :] - lanes(mx, reps), -jnp.inf))
        s = lax.dot_general(qb, kb, (((1,), (1,)), ((), ())),
                            preferred_element_type=F32) * e
        g_int = jnp.exp(m0_h - mx)
        ct_h = ct[h]
        ncr_h = ncr[h]
        num = (_dot(s.astype(BF16), vb)
               + lanes(g_int, A_DV // LANES) * _dot(qb, ct_h.astype(BF16)))
        den = jnp.sum(s, axis=1, keepdims=True) + g_int * _dot(qb, ncr_h.astype(BF16))
        scale = jnp.maximum(jnp.abs(den), jnp.exp(-(fc_t + mx)))
        hh = num / lanes(scale, A_DV // LANES)
        hn = _rms(hh, gmh_ref[:, A_DV * h:A_DV * (h + 1)])
        og = og_ref[0, :, A_DV * h:A_DV * (h + 1)]
        outa_ref[0, :, A_DV * h:A_DV * (h + 1)] = (hn * _sigmoid(og)).astype(BF16)
        kws = kb.astype(F32).T * ws[h:h + 1, :]
        dec_h = decay[h:h + 1, :]
        ct[h] = lanes(dec_h, A_DV // LANES) * ct_h + _dot(kws.astype(BF16), vb)
        ncr[h] = dec_h * ncr_h + jnp.sum(kws, axis=1, keepdims=True)
    m_s[...] = jnp.broadcast_to(m_new, m_s.shape)

    @pl.when(c == n_chunks - 1)
    def _():
        for h in range(A_HEADS):
            co_ref[0, h] = ct[h].T
            no_ref[0, h:h + 1, :] = ncr[h].T[0:1, :]
        mo_ref[0] = m_s[...]


def _gate_replication_matrix():
    r = np.zeros((LANES, 2 * A_HEADS * LANES), np.float32)
    for h in range(A_HEADS):
        r[GATE_LANE_CMX + h, LANES * h:LANES * (h + 1)] = 1.0
        r[GATE_LANE_FC + h, LANES * (A_HEADS + h):LANES * (A_HEADS + h + 1)] = 1.0
    return jnp.asarray(r, BF16)


def _mlstm(qk, v, og, gates, gates_c, tail0, c0, n0, m0, wconv, bconv, gmh, *, chunk, valid):
    b, s_len, _ = qk.shape
    n_chunks = s_len // chunk
    rep = _gate_replication_matrix()
    seq = lambda i, c: (i, c, 0)
    per_b3 = lambda i, c: (i, 0, 0)
    per_b4 = lambda i, c: (i, 0, 0, 0)
    const2 = lambda i, c: (0, 0)
    kernel = functools.partial(_mlstm_kernel, chunk=chunk, valid=valid, n_chunks=n_chunks)
    return pl.pallas_call(
        kernel,
        grid=(b, n_chunks),
        in_specs=[
            pl.BlockSpec((1, chunk, QK_W), seq),
            pl.BlockSpec((1, chunk, D_A), seq),
            pl.BlockSpec((1, chunk, D_A), seq),
            pl.BlockSpec((2 * SUBLANES, chunk), lambda i, c: (0, i * n_chunks + c)),
            pl.BlockSpec((1, chunk, LANES), seq),
            pl.BlockSpec(rep.shape, const2),
            pl.BlockSpec((1, SUBLANES, QK_W), per_b3),
            pl.BlockSpec((1, A_HEADS, A_DV, A_DQK), per_b4),
            pl.BlockSpec((1, A_HEADS, A_DQK), per_b3),
            pl.BlockSpec((1, SUBLANES, LANES), per_b3),
            pl.BlockSpec((CONV_W, QK_W), const2),
            pl.BlockSpec((1, QK_W), const2),
            pl.BlockSpec((1, D_A), const2),
        ],
        out_specs=[
            pl.BlockSpec((1, chunk, D_A), seq),
            pl.BlockSpec((1, SUBLANES, QK_W), per_b3),
            pl.BlockSpec((1, A_HEADS, A_DV, A_DQK), per_b4),
            pl.BlockSpec((1, A_HEADS, A_DQK), per_b3),
            pl.BlockSpec((1, SUBLANES, LANES), per_b3),
        ],
        out_shape=(
            jax.ShapeDtypeStruct((b, s_len, D_A), BF16),
            jax.ShapeDtypeStruct((b, SUBLANES, QK_W), F32),
            jax.ShapeDtypeStruct((b, A_HEADS, A_DV, A_DQK), F32),
            jax.ShapeDtypeStruct((b, A_HEADS, A_DQK), F32),
            jax.ShapeDtypeStruct((b, SUBLANES, LANES), F32),
        ),
        scratch_shapes=[
            pltpu.VMEM((chunk + SUBLANES, QK_W), F32),
            pltpu.VMEM((A_HEADS, A_DQK, A_DV), F32),
            pltpu.VMEM((A_HEADS, A_DQK, LANES), F32),
            pltpu.VMEM((SUBLANES, LANES), F32),
        ],
        compiler_params=pltpu.CompilerParams(
            dimension_semantics=("arbitrary", "arbitrary"), vmem_limit_bytes=VMEM_LIMIT),
        name="mlstm",
    )(qk, v, og, gates, gates_c, rep, tail0, c0, n0, m0, wconv, bconv, gmh)


def _s5_kernel(u_ref, s0re_ref, s0im_ref, lbre_ref, lbim_ref, wb_ref, wcre_ref, wcim_ref,
               d_ref, perm_ref, permt_ref, wglu_ref, bglu_ref,
               outb_ref, sre_out, sim_out,
               bure, buim, sre, sim, utb, xtb, ytb, *, tb, n_steps):
    step = pl.program_id(0)
    batch = u_ref.shape[0]
    rows_j = batch * PERM_T
    n_j = tb // PERM_T

    @pl.when(step == 0)
    def _():
        sre[...] = s0re_ref[...]
        sim[...] = s0im_ref[...]

    for j in range(n_j):
        xj = jnp.concatenate(
            [u_ref[b, PERM_T * j:PERM_T * (j + 1), :] for b in range(batch)], axis=0)
        hi = xj.astype(BF16)
        lo = (xj - hi.astype(F32)).astype(BF16)
        uh = _dot(perm_ref[...], hi)
        rows = slice(rows_j * j, rows_j * (j + 1))
        xtb[rows, :] = uh.astype(BF16)
        utb[rows, :] = uh + _dot(perm_ref[...], lo)

    strip = 8 * LANES
    blocks_per_strip = strip // STATE_PER_BLOCK
    for st in range(S5_N // strip):
        cols = slice(strip * st, strip * (st + 1))
        blocks = range(blocks_per_strip * st, blocks_per_strip * (st + 1))
        for k in blocks:
            r = _dot(xtb[:, LANES * k:LANES * (k + 1)], wb_ref[k])
            kc = slice(STATE_PER_BLOCK * k, STATE_PER_BLOCK * (k + 1))
            bure[:, kc] = r[:, :STATE_PER_BLOCK]
            buim[:, kc] = r[:, STATE_PER_BLOCK:]
        a_re = jnp.broadcast_to(lbre_ref[:, cols], (batch, strip))
        a_im = jnp.broadcast_to(lbim_ref[:, cols], (batch, strip))
        s_r = sre[:, cols]
        s_i = sim[:, cols]
        for t in range(tb):
            tr = slice(t * batch, (t + 1) * batch)
            n_r = a_re * s_r - a_im * s_i + bure[tr, cols]
            n_i = a_re * s_i + a_im * s_r + buim[tr, cols]
            bure[tr, cols] = n_r
            buim[tr, cols] = n_i
            s_r, s_i = n_r, n_i
        sre[:, cols] = s_r
        sim[:, cols] = s_i
        for k in blocks:
            kc = slice(STATE_PER_BLOCK * k, STATE_PER_BLOCK * (k + 1))
            ytb[:, LANES * k:LANES * (k + 1)] = (
                _dot(bure[:, kc].astype(BF16), wcre_ref[k])
                - _dot(buim[:, kc].astype(BF16), wcim_ref[k]))

    z = _gelu_tanh(ytb[...] + d_ref[...] * utb[...])
    gate = _sigmoid(_dot(z.astype(BF16), wglu_ref[...]) + bglu_ref[...])
    res = (z * gate).astype(BF16)
    for j in range(n_j):
        rb = _dot(permt_ref[...], res[rows_j * j:rows_j * (j + 1), :]).astype(BF16)
        for b in range(batch):
            outb_ref[b, PERM_T * j:PERM_T * (j + 1), :] = rb[PERM_T * b:PERM_T * (b + 1), :]

    @pl.when(step == n_steps - 1)
    def _():
        sre_out[...] = sre[...]
        sim_out[...] = sim[...]


def _s5(u, s0re, s0im, lbre, lbim, wb, wcre, wcim, d_skip, perm, permt, wglu, bglu, *, tb):
    b, s_len, _ = u.shape
    n_steps = s_len // tb
    rows = b * tb
    kernel = functools.partial(_s5_kernel, tb=tb, n_steps=n_steps)
    state = jax.ShapeDtypeStruct((b, S5_N), F32)
    return pl.pallas_call(
        kernel,
        grid=(n_steps,),
        in_specs=[
            pl.BlockSpec((b, tb, D_B), lambda i: (0, i, 0)),
            _resident((b, S5_N)),
            _resident((b, S5_N)),
            _resident((1, S5_N)),
            _resident((1, S5_N)),
            _resident((N_BLOCKS, LANES, 2 * STATE_PER_BLOCK)),
            _resident((N_BLOCKS, STATE_PER_BLOCK, LANES)),
            _resident((N_BLOCKS, STATE_PER_BLOCK, LANES)),
            _resident((1, D_B)),
            _resident((b * PERM_T, b * PERM_T)),
            _resident((b * PERM_T, b * PERM_T)),
            _resident((D_B, D_B)),
            _resident((1, D_B)),
        ],
        out_specs=[
            pl.BlockSpec((b, tb, D_B), lambda i: (0, i, 0)),
            pl.BlockSpec((b, S5_N), lambda i: (0, 0)),
            pl.BlockSpec((b, S5_N), lambda i: (0, 0)),
        ],
        out_shape=(jax.ShapeDtypeStruct((b, s_len, D_B), BF16), state, state),
        scratch_shapes=[
            pltpu.VMEM((rows, S5_N), F32),
            pltpu.VMEM((rows, S5_N), F32),
            pltpu.VMEM((b, S5_N), F32),
            pltpu.VMEM((b, S5_N), F32),
            pltpu.VMEM((rows, D_B), F32),
            pltpu.VMEM((rows, D_B), BF16),
            pltpu.VMEM((rows, D_B), F32),
        ],
        compiler_params=pltpu.CompilerParams(
            dimension_semantics=("arbitrary",), vmem_limit_bytes=VMEM_LIMIT),
        name="s5",
    )(u, s0re, s0im, lbre, lbim, wb, wcre, wcim, d_skip, perm, permt, wglu, bglu)


def _outproj_kernel(a_ref, b_ref, x_ref, wo_ref, gpost_ref, gffn_ref, x1_ref, hf_ref, *, n_sub):
    for rows in _row_blocks(a_ref.shape[0], n_sub):
        mix = (_dot(a_ref[rows, :], wo_ref[0:D_A, :])
               + _dot(b_ref[rows, :], wo_ref[D_A:D_A + D_B, :]))
        x1 = x_ref[rows, :] + _rms(mix, gpost_ref[...])
        x1_ref[rows, :] = x1
        hf_ref[rows, :] = _rms(x1, gffn_ref[...]).astype(BF16)


def _out_proj(a, b, x2d, wo, g_post, g_ffn, tm, n_sub):
    t = x2d.shape[0]
    tok = lambda i: (i, 0)
    return pl.pallas_call(
        functools.partial(_outproj_kernel, n_sub=n_sub),
        grid=(t // tm,),
        in_specs=[
            pl.BlockSpec((tm, D_A), tok),
            pl.BlockSpec((tm, D_B), tok),
            pl.BlockSpec((tm, D_MODEL), tok),
            _resident((D_A + D_B, D_MODEL)),
            _resident((1, D_MODEL)),
            _resident((1, D_MODEL)),
        ],
        out_specs=[pl.BlockSpec((tm, D_MODEL), tok), pl.BlockSpec((tm, D_MODEL), tok)],
        out_shape=(jax.ShapeDtypeStruct((t, D_MODEL), F32),
                   jax.ShapeDtypeStruct((t, D_MODEL), BF16)),
        compiler_params=pltpu.CompilerParams(
            dimension_semantics=("arbitrary",), vmem_limit_bytes=VMEM_LIMIT),
        name="out_proj",
    )(a, b, x2d, wo, g_post, g_ffn)


def _ffn_up_kernel(hfp_ref, hfs_ref, wg_ref, wu_ref, actp_ref, acts_ref, wgb, wub):
    m = pl.program_id(1)

    @pl.when(m == 0)
    def _():
        wgb[...] = wg_ref[...].astype(BF16)
        wub[...] = wu_ref[...].astype(BF16)

    def body(hf_ref, act_ref):
        hf = hf_ref[...]
        a = _dot(hf, wgb[...])
        act_ref[...] = (a * _sigmoid(a) * _dot(hf, wub[...])).astype(BF16)

    @pl.when(m == 0)
    def _():
        body(hfs_ref, acts_ref)

    @pl.when(m > 0)
    def _():
        body(hfp_ref, actp_ref)


def _ffn_up(hf_p, hf_s, wg, wu, tm, tf):
    t_p, t_s = hf_p.shape[0], hf_s.shape[0]
    n_p = t_p // tm
    n_f = D_FF // tf
    prompt_tile = lambda m: jnp.maximum(m - 1, 0)
    return pl.pallas_call(
        _ffn_up_kernel,
        grid=(n_f, n_p + 1),
        in_specs=[
            pl.BlockSpec((tm, D_MODEL), lambda f, m: (prompt_tile(m), 0)),
            pl.BlockSpec((t_s, D_MODEL), lambda f, m: (0, 0)),
            pl.BlockSpec((D_MODEL, tf), lambda f, m: (0, f)),
            pl.BlockSpec((D_MODEL, tf), lambda f, m: (0, f)),
        ],
        out_specs=[
            pl.BlockSpec((tm, tf), lambda f, m: (prompt_tile(m), f)),
            pl.BlockSpec((t_s, tf), lambda f, m: (0, f)),
        ],
        out_shape=(jax.ShapeDtypeStruct((t_p, D_FF), BF16),
                   jax.ShapeDtypeStruct((t_s, D_FF), BF16)),
        scratch_shapes=[pltpu.VMEM((D_MODEL, tf), BF16), pltpu.VMEM((D_MODEL, tf), BF16)],
        compiler_params=pltpu.CompilerParams(
            dimension_semantics=("arbitrary", "arbitrary"), vmem_limit_bytes=VMEM_LIMIT),
        name="ffn_up",
    )(hf_p, hf_s, wg, wu)


def _ffn_down_kernel(act_ref, wd_ref, x1_ref, gp_ref, out_ref, *, n_sub):
    for rows in _row_blocks(act_ref.shape[0], n_sub):
        ff = _dot(act_ref[rows, :], wd_ref[...])
        out_ref[rows, :] = x1_ref[rows, :] + _rms(ff, gp_ref[...])


def _ffn_down(act, wd, x1, g_post, tm, n_sub):
    t = act.shape[0]
    tok = lambda i: (i, 0)
    return pl.pallas_call(
        functools.partial(_ffn_down_kernel, n_sub=n_sub),
        grid=(t // tm,),
        in_specs=[
            pl.BlockSpec((tm, D_FF), tok),
            _resident((D_FF, D_MODEL)),
            pl.BlockSpec((tm, D_MODEL), tok),
            _resident((1, D_MODEL)),
        ],
        out_specs=pl.BlockSpec((tm, D_MODEL), tok),
        out_shape=jax.ShapeDtypeStruct((t, D_MODEL), F32),
        compiler_params=pltpu.CompilerParams(
            dimension_semantics=("arbitrary",), vmem_limit_bytes=VMEM_LIMIT),
        name="ffn_down",
    )(act, wd, x1, g_post)


def _block_diag_in(bbt):
    x = bbt.reshape(B_GROUP, N_BLOCKS, GROUPS_PER_BLOCK, B_STATE)
    x = jnp.transpose(x, (1, 2, 0, 3))
    eye = jnp.eye(GROUPS_PER_BLOCK, dtype=bool)[None, :, None, :, None]
    y = jnp.where(eye, x[:, :, :, None, :], 0.0)
    return y.reshape(N_BLOCKS, LANES, STATE_PER_BLOCK)


def _block_diag_out(c):
    x = c.reshape(N_BLOCKS, GROUPS_PER_BLOCK, B_GROUP, B_STATE)
    x = jnp.transpose(x, (0, 1, 3, 2))
    eye = jnp.eye(GROUPS_PER_BLOCK, dtype=bool)[None, :, None, :, None]
    y = jnp.where(eye, x[:, :, :, None, :], 0.0)
    return y.reshape(N_BLOCKS, STATE_PER_BLOCK, LANES)


def _row_permutation(batch):
    n = batch * PERM_T
    src = np.arange(n)
    b, t = src // PERM_T, src % PERM_T
    p = np.zeros((n, n), np.float32)
    p[t * batch + b, src] = 1.0
    return jnp.asarray(p, BF16), jnp.asarray(p.T, BF16)


def _pad_rows_front(x, rows):
    pad = [(0, 0)] * x.ndim
    pad[-2] = (rows - x.shape[-2], 0)
    return jnp.pad(x, pad)


def _pad_rows_back(x, rows):
    pad = [(0, 0)] * x.ndim
    pad[-2] = (0, rows - x.shape[-2])
    return jnp.pad(x, pad)


def _mix_layer(x, conv0, c0, n0, m0, sre0, sim0, p, *, tm, n_sub, chunk, tb):
    b, s_len, _ = x.shape
    t = b * s_len
    x2d = x.reshape(t, D_MODEL)
    assert s_len % chunk == 0 or s_len < chunk
    seg = min(s_len, chunk)
    qk, v, og, u, gates, gates_c = _in_proj(x2d, p, tm=tm, n_sub=n_sub, seg=seg)

    s_pad = -(-s_len // chunk) * chunk
    extra = s_pad - s_len
    def seq3(a, mode="constant"):
        a = a.reshape(b, s_len, a.shape[-1])
        return a if extra == 0 else jnp.pad(a, ((0, 0), (0, extra), (0, 0)), mode=mode)
    g3 = gates.reshape(2 * SUBLANES, b, s_len)
    if extra:
        pad_t = ((0, 0), (0, 0), (0, extra))
        g3 = jnp.concatenate([
            jnp.pad(g3[:SUBLANES], pad_t, constant_values=-jnp.inf),
            jnp.pad(g3[SUBLANES:], pad_t, mode="edge")], axis=0)
    out_a, conv_o, c_o, n_o, m_o = _mlstm(
        seq3(qk), seq3(v), seq3(og), g3.reshape(2 * SUBLANES, b * s_pad),
        seq3(gates_c, "edge"), _pad_rows_front(conv0, SUBLANES), c0, n0,
        jnp.broadcast_to(_pad_rows_back(m0[..., None], SUBLANES), (b, SUBLANES, LANES)),
        p["w_conv"], p["b_conv"], p["g_mh"], chunk=chunk, valid=seg)
    out_a = out_a[:, :s_len].reshape(t, D_A)

    out_b, sre_o, sim_o = _s5(
        u.reshape(b, s_len, D_B), sre0.reshape(b, S5_N), sim0.reshape(b, S5_N),
        p["lbre"], p["lbim"], p["wb"], p["wcre"], p["wcim"], p["s5_d"],
        p["perm"], p["permt"], p["w_glu"], p["b_glu"], tb=tb)

    x1, hf = _out_proj(out_a, out_b.reshape(t, D_B), x2d, p["w_out"],
                       p["g_post_mix"], p["g_pre_ffn"], tm, n_sub)
    states = (
        conv_o[:, SUBLANES - (CONV_W - 1):, :],
        c_o,
        n_o,
        m_o[:, :A_HEADS, 0],
        sre_o.reshape(b, B_GROUPS, B_STATE),
        sim_o.reshape(b, B_GROUPS, B_STATE),
    )
    return x1, hf, states


def kernel(x_prompt, x_sample, state_conv, state_mlstm_c, state_mlstm_n, state_mlstm_m, state_s5_re, state_s5_im, g_pre_mix, w_in, w_conv, b_conv, b_igate, b_fgate, g_mh, s5_lam_re, s5_lam_im, s5_log_dt, s5_b_re, s5_b_im, s5_c_re, s5_c_im, s5_d, w_glu, b_glu, w_out, g_post_mix, g_pre_ffn, w_gate, w_up, w_down, g_post_ffn):
    l = 0
    bp = x_prompt.shape[0]
    lbre, lbim, bbt_re, bbt_im = _discretise(
        s5_lam_re[l], s5_lam_im[l], s5_log_dt[l],
        jnp.transpose(s5_b_re[l], (2, 0, 1)), jnp.transpose(s5_b_im[l], (2, 0, 1)))

    o1 = QK_W
    o2 = o1 + D_A
    o3 = o2 + D_A
    o4 = o3 + A_HEADS
    o5 = o4 + A_HEADS
    w = w_in[l]
    zrows = jnp.zeros((SUBLANES - A_HEADS, D_MODEL), F32)
    zb = jnp.zeros((SUBLANES - A_HEADS,), F32)
    perm, permt = _row_permutation(bp)
    row = lambda a: a.reshape(1, -1)
    p = {
        "g_pre_mix": row(g_pre_mix[l]),
        "w_in": jnp.concatenate([w[:, :o3], w[:, o5:]], axis=1).astype(BF16),
        "wgt": jnp.concatenate([w[:, o3:o4].T, zrows, w[:, o4:o5].T, zrows], axis=0).astype(BF16),
        "gbias": jnp.concatenate([b_igate[l], zb, b_fgate[l], zb]).reshape(2 * SUBLANES, 1),
        "w_conv": w_conv[l],
        "b_conv": row(b_conv[l]),
        "g_mh": row(g_mh[l]),
        "lbre": lbre.reshape(1, S5_N),
        "lbim": lbim.reshape(1, S5_N),
        "wb": jnp.concatenate([_block_diag_in(bbt_re), _block_diag_in(bbt_im)], axis=-1).astype(BF16),
        "wcre": _block_diag_out(s5_c_re[l]).astype(BF16),
        "wcim": _block_diag_out(s5_c_im[l]).astype(BF16),
        "s5_d": row(s5_d[l]),
        "perm": perm,
        "permt": permt,
        "w_glu": w_glu[l].astype(BF16),
        "b_glu": row(b_glu[l]),
        "w_out": w_out[l].astype(BF16),
        "g_post_mix": row(g_post_mix[l]),
        "g_pre_ffn": row(g_pre_ffn[l]),
    }

    x1_p, hf_p, st_p = _mix_layer(
        x_prompt,
        jnp.zeros((bp, CONV_W - 1, QK_W), F32),
        jnp.zeros((bp, A_HEADS, A_DV, A_DQK), F32),
        jnp.zeros((bp, A_HEADS, A_DQK), F32),
        jnp.zeros((bp, A_HEADS), F32),
        jnp.zeros((bp, B_GROUPS, B_STATE), F32),
        jnp.zeros((bp, B_GROUPS, B_STATE), F32),
        p, tm=512, n_sub=2, chunk=128, tb=64)
    x1_s, hf_s, st_s = _mix_layer(
        x_sample, state_conv[l], state_mlstm_c[l], state_mlstm_n[l], state_mlstm_m[l],
        state_s5_re[l], state_s5_im[l],
        p, tm=128, n_sub=1, chunk=128, tb=16)

    act_p, act_s = _ffn_up(hf_p, hf_s, w_gate[l], w_up[l], tm=1024, tf=512)
    wd = w_down[l].astype(BF16)
    g_post = row(g_post_ffn[l])
    yp = _ffn_down(act_p, wd, x1_p, g_post, tm=512, n_sub=2).reshape(x_prompt.shape)
    ys = _ffn_down(act_s, wd, x1_s, g_post, tm=128, n_sub=1).reshape(x_sample.shape)
    return (yp, ys) + tuple(a[None] for a in st_p) + tuple(a[None] for a in st_s)
```

```python
import functools

import numpy as np
import jax
import jax.numpy as jnp
from jax import lax
from jax.experimental import pallas as pl
from jax.experimental.pallas import tpu as pltpu

F32 = jnp.float32
BF16 = jnp.bfloat16

D_MODEL = 2048
D_A = 1024
D_B = 1024
A_HEADS = 4
A_DV = 256
A_DQK = 128
QK_W = 1024
CONV_W = 4
B_GROUP = 16
B_GROUPS = 64
B_STATE = 64
S5_N = B_GROUPS * B_STATE
D_FF = 5632
EPS = 1e-6

SUBLANES = 8
LANES = 128
GROUPS_PER_BLOCK = LANES // B_GROUP
N_BLOCKS = B_GROUPS // GROUPS_PER_BLOCK
STATE_PER_BLOCK = GROUPS_PER_BLOCK * B_STATE
PERM_T = 16
VMEM_LIMIT = 56 * 1024 * 1024


def _sigmoid(x):
    return 1.0 / (1.0 + jnp.exp(-x))


def _gelu_tanh(x):
    c = np.sqrt(2 / np.pi).astype(np.float32)
    return x * (0.5 * (1.0 + jnp.tanh(c * (x + 0.044715 * (x ** 3)))))


def _rms(x, g):
    return x * lax.rsqrt(jnp.mean(x * x, axis=-1, keepdims=True) + EPS) * g


def _dot(a, b):
    return jnp.dot(a, b, preferred_element_type=F32)


def _log_sigmoid(z):
    return jnp.minimum(z, 0.0) - jnp.log1p(jnp.exp(-jnp.abs(z)))


def _split3(x):
    hi = x.astype(BF16)
    r = x - hi.astype(F32)
    mid = r.astype(BF16)
    lo = (r - mid.astype(F32)).astype(BF16)
    return hi, mid, lo


def _dot3_right(x, sel):
    hi, mid, lo = _split3(x)
    return _dot(hi, sel) + _dot(mid, sel) + _dot(lo, sel)


def _dot3_left(sel, x):
    hi, mid, lo = _split3(x)
    return _dot(sel, hi) + _dot(sel, mid) + _dot(sel, lo)


def _row_blocks(rows, n_sub):
    assert rows % n_sub == 0
    size = rows // n_sub
    return [slice(size * r, size * (r + 1)) for r in range(n_sub)]


def _resident(shape):
    zeros = (0,) * len(shape)
    return pl.BlockSpec(shape, lambda *_: zeros, pipeline_mode=pl.Buffered(1))


def _disc_kernel(lam_re_ref, lam_im_ref, logdt_ref, bre_ref, bim_ref,
                 lbre_ref, lbim_ref, bbre_ref, bbim_ref):
    lam_re = lam_re_ref[...]
    lam_im = lam_im_ref[...]
    dt = jnp.exp(logdt_ref[...])
    mag = jnp.exp(lam_re * dt)
    ang = lam_im * dt
    lb_re = mag * jnp.cos(ang)
    lb_im = mag * jnp.sin(ang)
    den = lam_re * lam_re + lam_im * lam_im
    nr = lb_re - 1.0
    f_re = (nr * lam_re + lb_im * lam_im) / den
    f_im = (lb_im * lam_re - nr * lam_im) / den
    lbre_ref[...] = lb_re
    lbim_ref[...] = lb_im
    b_re = bre_ref[...]
    b_im = bim_ref[...]
    bbre_ref[...] = f_re[None] * b_re - f_im[None] * b_im
    bbim_ref[...] = f_re[None] * b_im + f_im[None] * b_re


def _discretise(lam_re, lam_im, log_dt, bt_re, bt_im):
    gp = jax.ShapeDtypeStruct((B_GROUPS, B_STATE), F32)
    igp = jax.ShapeDtypeStruct((B_GROUP, B_GROUPS, B_STATE), F32)
    return pl.pallas_call(
        _disc_kernel, out_shape=(gp, gp, igp, igp), name="s5_discretise",
    )(lam_re, lam_im, log_dt.reshape(B_GROUPS, 1), bt_re, bt_im)


GATE_LANE_CMX = SUBLANES
GATE_LANE_FC = SUBLANES + A_HEADS


def _inproj_kernel(x_ref, gpre_ref, w_ref, wgc_ref, gb_ref, triu_ref, tril_ref, *rest,
                   n_sub, seg, n_cast):
    cast_in = rest[:n_cast]
    qk_out, v_out, og_out, u_out, g_out, c_out = rest[n_cast:n_cast + 6]
    cast_out = rest[n_cast + 6:2 * n_cast + 6]
    wgt = rest[2 * n_cast + 6]
    for src, dst in zip(cast_in, cast_out):
        dst[...] = src[...].astype(BF16)

    @pl.when(pl.program_id(0) == 0)
    def _():
        wgt[...] = wgc_ref[...].T[0:2 * SUBLANES, :].astype(BF16)

    for rows in _row_blocks(x_ref.shape[0], n_sub):
        hm = _rms(x_ref[rows, :], gpre_ref[...]).astype(BF16)

        n = rows.stop - rows.start
        z = lax.dot_general(wgt[...], hm, (((1,), (1,)), ((), ())),
                            preferred_element_type=F32) + gb_ref[...]
        qk_out[rows, :] = _dot(hm, w_ref[:, 0:QK_W])

        row = lax.broadcasted_iota(jnp.int32, z.shape, 0)
        zf = jnp.where(row >= SUBLANES, _log_sigmoid(z), z)
        zc = jnp.concatenate([zf, jnp.zeros((LANES - 2 * SUBLANES, n), F32)], axis=0).T
        lane = lax.broadcasted_iota(jnp.int32, zc.shape, 1)
        lf_c = jnp.where((lane >= SUBLANES) & (lane < SUBLANES + A_HEADS), zc, 0.0)
        fc = _dot3_right(zf, triu_ref[...])[SUBLANES:2 * SUBLANES]
        fc_c = _dot3_left(tril_ref[...], lf_c)
        v_out[rows, :] = _dot(hm, w_ref[:, QK_W:QK_W + D_A]).astype(BF16)
        og_out[rows, :] = _dot(hm, w_ref[:, QK_W + D_A:QK_W + 2 * D_A])
        u_out[rows, :] = _dot(hm, w_ref[:, QK_W + 2 * D_A:QK_W + 2 * D_A + D_B])

        g_out[0:SUBLANES, rows] = zf[0:SUBLANES] - fc
        g_out[SUBLANES:2 * SUBLANES, rows] = fc
        cm = pltpu.roll(zc, SUBLANES, axis=1) - fc_c
        t_in = lax.broadcasted_iota(jnp.int32, zc.shape, 0) & (seg - 1)
        sh = 1
        while sh < seg:
            cm = jnp.where(t_in >= sh, jnp.maximum(cm, pltpu.roll(cm, sh, axis=0)), cm)
            sh *= 2
        c_out[rows, :] = jnp.where(lane < GATE_LANE_FC, cm, pltpu.roll(fc_c, A_HEADS, axis=1))


def _segment_prefix_matrices(n, seg):
    idx = np.arange(n)
    u = ((idx[:, None] // seg == idx[None, :] // seg) & (idx[:, None] <= idx[None, :]))
    u = u.astype(np.float32)
    return jnp.asarray(u, BF16), jnp.asarray(u.T, BF16)


def _in_proj(x2d, p, *, tm, n_sub, seg, cast=()):
    t = x2d.shape[0]
    n_steps = t // tm
    assert seg & (seg - 1) == 0
    triu, tril = _segment_prefix_matrices(tm // n_sub, seg)
    w_cols = QK_W + 2 * D_A + D_B
    tok = lambda i: (i, 0)
    cast_specs = [pl.BlockSpec((w.shape[0] // n_steps, w.shape[1]), tok) for w in cast]
    return pl.pallas_call(
        functools.partial(_inproj_kernel, n_sub=n_sub, seg=seg, n_cast=len(cast)),
        grid=(n_steps,),
        in_specs=[
            pl.BlockSpec((tm, D_MODEL), tok),
            _resident((1, D_MODEL)),
            _resident((D_MODEL, w_cols)),
            _resident((D_MODEL, LANES)),
            _resident((2 * SUBLANES, 1)),
            _resident(triu.shape),
            _resident(tril.shape),
        ] + cast_specs,
        out_specs=[
            pl.BlockSpec((tm, QK_W), tok),
            pl.BlockSpec((tm, D_A), tok),
            pl.BlockSpec((tm, D_A), tok),
            pl.BlockSpec((tm, D_B), tok),
            pl.BlockSpec((2 * SUBLANES, tm), lambda i: (0, i)),
            pl.BlockSpec((tm, LANES), tok),
        ] + cast_specs,
        out_shape=(
            jax.ShapeDtypeStruct((t, QK_W), F32),
            jax.ShapeDtypeStruct((t, D_A), BF16),
            jax.ShapeDtypeStruct((t, D_A), F32),
            jax.ShapeDtypeStruct((t, D_B), F32),
            jax.ShapeDtypeStruct((2 * SUBLANES, t), F32),
            jax.ShapeDtypeStruct((t, LANES), F32),
        ) + tuple(jax.ShapeDtypeStruct(w.shape, BF16) for w in cast),
        scratch_shapes=[pltpu.VMEM((2 * SUBLANES, D_MODEL), BF16)],
        compiler_params=pltpu.CompilerParams(
            dimension_semantics=("arbitrary",), vmem_limit_bytes=VMEM_LIMIT),
        name="in_proj",
    )(x2d, p["g_pre_mix"], p["w_in"], p["wgc"], p["gbias"], triu, tril, *cast)


def _mlstm_kernel(qk_ref, v_ref, og_ref, g_ref, gc_ref, rep_ref, tail0_ref, c0_ref, n0_ref, m0_ref,
                  wconv_ref, bconv_ref, gmh_ref,
                  outa_ref, convo_ref, co_ref, no_ref, mo_ref,
                  xp, ct, ncr, m_s, *, chunk, valid, n_chunks):
    L = chunk
    c = pl.program_id(1)
    reps = L // LANES

    @pl.when(c == 0)
    def _():
        xp[0:SUBLANES, :] = tail0_ref[0]
        for h in range(A_HEADS):
            ct[h] = c0_ref[0, h].T
            ncr[h] = jnp.broadcast_to(n0_ref[0, h:h + 1, :], (A_DQK, A_DQK)).T
        m_s[...] = m0_ref[0]

    def lanes(x, n):
        return x if n == 1 else jnp.concatenate([x] * n, axis=1)

    xp[SUBLANES:SUBLANES + L, :] = qk_ref[0]
    xs = xp[...]
    acc = bconv_ref[...]
    for j in range(CONV_W):
        back = CONV_W - 1 - j
        tap = xs if back == 0 else pltpu.roll(xs, back, axis=0)
        acc = acc + tap[SUBLANES:, :] * wconv_ref[j:j + 1, :]
    qk = acc * _sigmoid(acc)
    q_all = (qk[:, :QK_W // 2] * (A_DQK ** -0.5)).astype(BF16)
    k_all = qk[:, QK_W // 2:].astype(BF16)

    @pl.when(c == n_chunks - 1)
    def _():
        convo_ref[0] = xp[valid:valid + SUBLANES, :]

    xp[0:SUBLANES, :] = xp[L:L + SUBLANES, :]

    rep = _dot3_right(gc_ref[0], rep_ref[...])

    g = g_ref[...]
    gr = g[0:SUBLANES]
    fc = g[SUBLANES:2 * SUBLANES]
    lane = lax.broadcasted_iota(jnp.int32, (SUBLANES, L), 1)
    f_last = jnp.sum(jnp.where(lane == L - 1, fc, 0.0), axis=1, keepdims=True)
    m0 = jnp.max(m_s[...], axis=1, keepdims=True)
    lw = f_last + gr
    m_new = jnp.maximum(f_last + m0, jnp.max(lw, axis=1, keepdims=True))
    ws = jnp.exp(lw - m_new)
    decay = jnp.broadcast_to(jnp.exp(f_last + m0 - m_new), (SUBLANES, LANES))

    row_i = lax.broadcasted_iota(jnp.int32, (L, L), 0)
    col_i = lax.broadcasted_iota(jnp.int32, (L, L), 1)
    tri = col_i <= row_i
    for h in range(A_HEADS):
        qb = q_all[:, A_DQK * h:A_DQK * (h + 1)]
        kb = k_all[:, A_DQK * h:A_DQK * (h + 1)]
        vb = v_ref[0, :, A_DV * h:A_DV * (h + 1)]
        m0_h = m_s[h:h + 1, :]
        mx = jnp.maximum(rep[:, LANES * h:LANES * (h + 1)], m0_h)
        fc_t = rep[:, LANES * (A_HEADS + h):LANES * (A_HEADS + h + 1)]
        e = jnp.exp(jnp.where(tri, gr[h:h + 1, :] - lanes(mx, reps), -jnp.inf))
        s = lax.dot_general(qb, kb, (((1,), (1,)), ((), ())),
                            preferred_element_type=F32) * e
        g_int = jnp.exp(m0_h - mx)
        ct_h = ct[h]
        ncr_h = ncr[h]
        num = (_dot(s.astype(BF16), vb)
               + lanes(g_int, A_DV // LANES) * _dot(qb, ct_h.astype(BF16)))
        den = jnp.sum(s, axis=1, keepdims=True) + g_int * _dot(qb, ncr_h.astype(BF16))
        scale = jnp.maximum(jnp.abs(den), jnp.exp(-(fc_t + mx)))
        hh = num / lanes(scale, A_DV // LANES)
        hn = _rms(hh, gmh_ref[:, A_DV * h:A_DV * (h + 1)])
        og = og_ref[0, :, A_DV * h:A_DV * (h + 1)]
        outa_ref[0, :, A_DV * h:A_DV * (h + 1)] = (hn * _sigmoid(og)).astype(BF16)
        kws = kb.astype(F32).T * ws[h:h + 1, :]
        dec_h = decay[h:h + 1, :]
        ct[h] = lanes(dec_h, A_DV // LANES) * ct_h + _dot(kws.astype(BF16), vb)
        ncr[h] = dec_h * ncr_h + jnp.sum(kws, axis=1, keepdims=True)
    m_s[...] = jnp.broadcast_to(m_new, m_s.shape)

    @pl.when(c == n_chunks - 1)
    def _():
        for h in range(A_HEADS):
            co_ref[0, h] = ct[h].T
            no_ref[0, h:h + 1, :] = ncr[h].T[0:1, :]
        mo_ref[0] = m_s[...]


def _gate_replication_matrix():
    r = np.zeros((LANES, 2 * A_HEADS * LANES), np.float32)
    for h in range(A_HEADS):
        r[GATE_LANE_CMX + h, LANES * h:LANES * (h + 1)] = 1.0
        r[GATE_LANE_FC + h, LANES * (A_HEADS + h):LANES * (A_HEADS + h + 1)] = 1.0
    return jnp.asarray(r, BF16)


def _mlstm(qk, v, og, gates, gates_c, tail0, c0, n0, m0, wconv, bconv, gmh, *, chunk, valid):
    b, s_len, _ = qk.shape
    n_chunks = s_len // chunk
    rep = _gate_replication_matrix()
    seq = lambda i, c: (i, c, 0)
    per_b3 = lambda i, c: (i, 0, 0)
    per_b4 = lambda i, c: (i, 0, 0, 0)
    const2 = lambda i, c: (0, 0)
    kernel = functools.partial(_mlstm_kernel, chunk=chunk, valid=valid, n_chunks=n_chunks)
    return pl.pallas_call(
        kernel,
        grid=(b, n_chunks),
        in_specs=[
            pl.BlockSpec((1, chunk, QK_W), seq),
            pl.BlockSpec((1, chunk, D_A), seq),
            pl.BlockSpec((1, chunk, D_A), seq),
            pl.BlockSpec((2 * SUBLANES, chunk), lambda i, c: (0, i * n_chunks + c)),
            pl.BlockSpec((1, chunk, LANES), seq),
            pl.BlockSpec(rep.shape, const2),
            pl.BlockSpec((1, SUBLANES, QK_W), per_b3),
            pl.BlockSpec((1, A_HEADS, A_DV, A_DQK), per_b4),
            pl.BlockSpec((1, A_HEADS, A_DQK), per_b3),
            pl.BlockSpec((1, SUBLANES, LANES), per_b3),
            pl.BlockSpec((CONV_W, QK_W), const2),
            pl.BlockSpec((1, QK_W), const2),
            pl.BlockSpec((1, D_A), const2),
        ],
        out_specs=[
            pl.BlockSpec((1, chunk, D_A), seq),
            pl.BlockSpec((1, SUBLANES, QK_W), per_b3),
            pl.BlockSpec((1, A_HEADS, A_DV, A_DQK), per_b4),
            pl.BlockSpec((1, A_HEADS, A_DQK), per_b3),
            pl.BlockSpec((1, SUBLANES, LANES), per_b3),
        ],
        out_shape=(
            jax.ShapeDtypeStruct((b, s_len, D_A), BF16),
            jax.ShapeDtypeStruct((b, SUBLANES, QK_W), F32),
            jax.ShapeDtypeStruct((b, A_HEADS, A_DV, A_DQK), F32),
            jax.ShapeDtypeStruct((b, A_HEADS, A_DQK), F32),
            jax.ShapeDtypeStruct((b, SUBLANES, LANES), F32),
        ),
        scratch_shapes=[
            pltpu.VMEM((chunk + SUBLANES, QK_W), F32),
            pltpu.VMEM((A_HEADS, A_DQK, A_DV), F32),
            pltpu.VMEM((A_HEADS, A_DQK, LANES), F32),
            pltpu.VMEM((SUBLANES, LANES), F32),
        ],
        compiler_params=pltpu.CompilerParams(
            dimension_semantics=("arbitrary", "arbitrary"), vmem_limit_bytes=VMEM_LIMIT),
        name="mlstm",
    )(qk, v, og, gates, gates_c, rep, tail0, c0, n0, m0, wconv, bconv, gmh)


def _s5_kernel(u_ref, s0re_ref, s0im_ref, lbre_ref, lbim_ref, wb_ref, wcre_ref, wcim_ref,
               d_ref, perm_ref, permt_ref, wglu_ref, bglu_ref,
               outb_ref, sre_out, sim_out,
               bure, buim, sre, sim, utb, xtb, ytb, *, tb, n_steps):
    step = pl.program_id(0)
    batch = u_ref.shape[0]
    rows_j = batch * PERM_T
    n_j = tb // PERM_T

    @pl.when(step == 0)
    def _():
        sre[...] = s0re_ref[...]
        sim[...] = s0im_ref[...]

    strip = 8 * LANES
    blocks_per_strip = strip // STATE_PER_BLOCK
    n_strips = S5_N // strip

    def permute_in():
        for j in range(n_j):
            xj = jnp.concatenate(
                [u_ref[b, PERM_T * j:PERM_T * (j + 1), :] for b in range(batch)], axis=0)
            hi = xj.astype(BF16)
            lo = (xj - hi.astype(F32)).astype(BF16)
            uh = _dot(perm_ref[...], hi)
            rows = slice(rows_j * j, rows_j * (j + 1))
            xtb[rows, :] = uh.astype(BF16)
            utb[rows, :] = uh + _dot(perm_ref[...], lo)

    def input_map(st):
        for k in range(blocks_per_strip * st, blocks_per_strip * (st + 1)):
            r = _dot(xtb[:, LANES * k:LANES * (k + 1)], wb_ref[k])
            kc = slice(STATE_PER_BLOCK * k, STATE_PER_BLOCK * (k + 1))
            bure[:, kc] = r[:, :STATE_PER_BLOCK]
            buim[:, kc] = r[:, STATE_PER_BLOCK:]

    def recurrence(st):
        cols = slice(strip * st, strip * (st + 1))
        a_re = jnp.broadcast_to(lbre_ref[:, cols], (batch, strip))
        a_im = jnp.broadcast_to(lbim_ref[:, cols], (batch, strip))
        s_r = sre[:, cols]
        s_i = sim[:, cols]
        for t in range(tb):
            tr = slice(t * batch, (t + 1) * batch)
            n_r = a_re * s_r - a_im * s_i + bure[tr, cols]
            n_i = a_re * s_i + a_im * s_r + buim[tr, cols]
            bure[tr, cols] = n_r
            buim[tr, cols] = n_i
            s_r, s_i = n_r, n_i
        sre[:, cols] = s_r
        sim[:, cols] = s_i

    def output_map(st):
        for k in range(blocks_per_strip * st, blocks_per_strip * (st + 1)):
            kc = slice(STATE_PER_BLOCK * k, STATE_PER_BLOCK * (k + 1))
            ytb[:, LANES * k:LANES * (k + 1)] = (
                _dot(bure[:, kc].astype(BF16), wcre_ref[k])
                - _dot(buim[:, kc].astype(BF16), wcim_ref[k]))

    permute_in()
    for st in range(n_strips):
        input_map(st)
        recurrence(st)
        output_map(st)

    z = _gelu_tanh(ytb[...] + d_ref[...] * utb[...])
    gate = _sigmoid(_dot(z.astype(BF16), wglu_ref[...]) + bglu_ref[...])
    res = (z * gate).astype(BF16)
    for j in range(n_j):
        rb = _dot(permt_ref[...], res[rows_j * j:rows_j * (j + 1), :]).astype(BF16)
        for b in range(batch):
            outb_ref[b, PERM_T * j:PERM_T * (j + 1), :] = rb[PERM_T * b:PERM_T * (b + 1), :]

    @pl.when(step == n_steps - 1)
    def _():
        sre_out[...] = sre[...]
        sim_out[...] = sim[...]


def _s5(u, s0re, s0im, lbre, lbim, wb, wcre, wcim, d_skip, perm, permt, wglu, bglu, *, tb):
    b, s_len, _ = u.shape
    n_steps = s_len // tb
    rows = b * tb
    kernel = functools.partial(_s5_kernel, tb=tb, n_steps=n_steps)
    state = jax.ShapeDtypeStruct((b, S5_N), F32)
    return pl.pallas_call(
        kernel,
        grid=(n_steps,),
        in_specs=[
            pl.BlockSpec((b, tb, D_B), lambda i: (0, i, 0)),
            _resident((b, S5_N)),
            _resident((b, S5_N)),
            _resident((1, S5_N)),
            _resident((1, S5_N)),
            _resident((N_BLOCKS, LANES, 2 * STATE_PER_BLOCK)),
            _resident((N_BLOCKS, STATE_PER_BLOCK, LANES)),
            _resident((N_BLOCKS, STATE_PER_BLOCK, LANES)),
            _resident((1, D_B)),
            _resident((b * PERM_T, b * PERM_T)),
            _resident((b * PERM_T, b * PERM_T)),
            _resident((D_B, D_B)),
            _resident((1, D_B)),
        ],
        out_specs=[
            pl.BlockSpec((b, tb, D_B), lambda i: (0, i, 0)),
            pl.BlockSpec((b, S5_N), lambda i: (0, 0)),
            pl.BlockSpec((b, S5_N), lambda i: (0, 0)),
        ],
        out_shape=(jax.ShapeDtypeStruct((b, s_len, D_B), BF16), state, state),
        scratch_shapes=[
            pltpu.VMEM((rows, S5_N), F32),
            pltpu.VMEM((rows, S5_N), F32),
            pltpu.VMEM((b, S5_N), F32),
            pltpu.VMEM((b, S5_N), F32),
            pltpu.VMEM((rows, D_B), F32),
            pltpu.VMEM((rows, D_B), BF16),
            pltpu.VMEM((rows, D_B), F32),
        ],
        compiler_params=pltpu.CompilerParams(
            dimension_semantics=("arbitrary",), vmem_limit_bytes=VMEM_LIMIT),
        name="s5",
    )(u, s0re, s0im, lbre, lbim, wb, wcre, wcim, d_skip, perm, permt, wglu, bglu)


def _outproj_kernel(a_ref, b_ref, x_ref, wo_ref, gpost_ref, gffn_ref, x1_ref, hf_ref, *, n_sub):
    for rows in _row_blocks(a_ref.shape[0], n_sub):
        mix = (_dot(a_ref[rows, :], wo_ref[0:D_A, :])
               + _dot(b_ref[rows, :], wo_ref[D_A:D_A + D_B, :]))
        x1 = x_ref[rows, :] + _rms(mix, gpost_ref[...])
        x1_ref[rows, :] = x1
        hf_ref[rows, :] = _rms(x1, gffn_ref[...]).astype(BF16)


def _out_proj(a, b, x2d, wo, g_post, g_ffn, tm, n_sub):
    t = x2d.shape[0]
    tok = lambda i: (i, 0)
    return pl.pallas_call(
        functools.partial(_outproj_kernel, n_sub=n_sub),
        grid=(t // tm,),
        in_specs=[
            pl.BlockSpec((tm, D_A), tok),
            pl.BlockSpec((tm, D_B), tok),
            pl.BlockSpec((tm, D_MODEL), tok),
            _resident((D_A + D_B, D_MODEL)),
            _resident((1, D_MODEL)),
            _resident((1, D_MODEL)),
        ],
        out_specs=[pl.BlockSpec((tm, D_MODEL), tok), pl.BlockSpec((tm, D_MODEL), tok)],
        out_shape=(jax.ShapeDtypeStruct((t, D_MODEL), F32),
                   jax.ShapeDtypeStruct((t, D_MODEL), BF16)),
        compiler_params=pltpu.CompilerParams(
            dimension_semantics=("arbitrary",), vmem_limit_bytes=VMEM_LIMIT),
        name="out_proj",
    )(a, b, x2d, wo, g_post, g_ffn)


def _ffn_up_kernel(hfp_ref, hfs_ref, wg_ref, wu_ref, wd_ref, actp_ref, acts_ref, wdb_ref,
                   wgb, wub):
    m = pl.program_id(1)

    @pl.when(m == 0)
    def _():
        wgb[...] = wg_ref[...].astype(BF16)
        wub[...] = wu_ref[...].astype(BF16)
        wdb_ref[...] = wd_ref[...].astype(BF16)

    def body(hf_ref, act_ref):
        hf = hf_ref[...]
        a = _dot(hf, wgb[...])
        act_ref[...] = (a * _sigmoid(a) * _dot(hf, wub[...])).astype(BF16)

    @pl.when(m == 0)
    def _():
        body(hfs_ref, acts_ref)

    @pl.when(m > 0)
    def _():
        body(hfp_ref, actp_ref)


def _ffn_up(hf_p, hf_s, wg, wu, wd, tm, tf):
    t_p, t_s = hf_p.shape[0], hf_s.shape[0]
    n_p = t_p // tm
    n_f = D_FF // tf
    prompt_tile = lambda m: jnp.maximum(m - 1, 0)
    return pl.pallas_call(
        _ffn_up_kernel,
        grid=(n_f, n_p + 1),
        in_specs=[
            pl.BlockSpec((tm, D_MODEL), lambda f, m: (prompt_tile(m), 0)),
            pl.BlockSpec((t_s, D_MODEL), lambda f, m: (0, 0)),
            pl.BlockSpec((D_MODEL, tf), lambda f, m: (0, f)),
            pl.BlockSpec((D_MODEL, tf), lambda f, m: (0, f)),
            pl.BlockSpec((tf, D_MODEL), lambda f, m: (f, 0)),
        ],
        out_specs=[
            pl.BlockSpec((tm, tf), lambda f, m: (prompt_tile(m), f)),
            pl.BlockSpec((t_s, tf), lambda f, m: (0, f)),
            pl.BlockSpec((tf, D_MODEL), lambda f, m: (f, 0)),
        ],
        out_shape=(jax.ShapeDtypeStruct((t_p, D_FF), BF16),
                   jax.ShapeDtypeStruct((t_s, D_FF), BF16),
                   jax.ShapeDtypeStruct((D_FF, D_MODEL), BF16)),
        scratch_shapes=[pltpu.VMEM((D_MODEL, tf), BF16), pltpu.VMEM((D_MODEL, tf), BF16)],
        compiler_params=pltpu.CompilerParams(
            dimension_semantics=("arbitrary", "arbitrary"), vmem_limit_bytes=VMEM_LIMIT),
        name="ffn_up",
    )(hf_p, hf_s, wg, wu, wd)


def _ffn_down_kernel(act_ref, wd_ref, x1_ref, gp_ref, out_ref, *, n_sub):
    for rows in _row_blocks(act_ref.shape[0], n_sub):
        ff = _dot(act_ref[rows, :], wd_ref[...])
        out_ref[rows, :] = x1_ref[rows, :] + _rms(ff, gp_ref[...])


def _ffn_down(act, wd, x1, g_post, tm, n_sub):
    t = act.shape[0]
    tok = lambda i: (i, 0)
    return pl.pallas_call(
        functools.partial(_ffn_down_kernel, n_sub=n_sub),
        grid=(t // tm,),
        in_specs=[
            pl.BlockSpec((tm, D_FF), tok),
            _resident((D_FF, D_MODEL)),
            pl.BlockSpec((tm, D_MODEL), tok),
            _resident((1, D_MODEL)),
        ],
        out_specs=pl.BlockSpec((tm, D_MODEL), tok),
        out_shape=jax.ShapeDtypeStruct((t, D_MODEL), F32),
        compiler_params=pltpu.CompilerParams(
            dimension_semantics=("arbitrary",), vmem_limit_bytes=VMEM_LIMIT),
        name="ffn_down",
    )(act, wd, x1, g_post)


def _block_diag_in(bbt):
    x = bbt.reshape(B_GROUP, N_BLOCKS, GROUPS_PER_BLOCK, B_STATE)
    x = jnp.transpose(x, (1, 2, 0, 3))
    eye = jnp.eye(GROUPS_PER_BLOCK, dtype=bool)[None, :, None, :, None]
    y = jnp.where(eye, x[:, :, :, None, :], 0.0)
    return y.reshape(N_BLOCKS, LANES, STATE_PER_BLOCK)


def _block_diag_out(c):
    x = c.reshape(N_BLOCKS, GROUPS_PER_BLOCK, B_GROUP, B_STATE)
    x = jnp.transpose(x, (0, 1, 3, 2))
    eye = jnp.eye(GROUPS_PER_BLOCK, dtype=bool)[None, :, None, :, None]
    y = jnp.where(eye, x[:, :, :, None, :], 0.0)
    return y.reshape(N_BLOCKS, STATE_PER_BLOCK, LANES)


def _row_permutation(batch):
    n = batch * PERM_T
    src = np.arange(n)
    b, t = src // PERM_T, src % PERM_T
    p = np.zeros((n, n), np.float32)
    p[t * batch + b, src] = 1.0
    return jnp.asarray(p, BF16), jnp.asarray(p.T, BF16)


def _pad_rows_front(x, rows):
    pad = [(0, 0)] * x.ndim
    pad[-2] = (rows - x.shape[-2], 0)
    return jnp.pad(x, pad)


def _pad_rows_back(x, rows):
    pad = [(0, 0)] * x.ndim
    pad[-2] = (0, rows - x.shape[-2])
    return jnp.pad(x, pad)


def _mix_layer(x, conv0, c0, n0, m0, sre0, sim0, p, *, tm, n_sub, chunk, tb):
    b, s_len, _ = x.shape
    t = b * s_len
    x2d = x.reshape(t, D_MODEL)
    assert s_len % chunk == 0 or s_len < chunk
    seg = min(s_len, chunk)
    if "w_glu" in p:
        qk, v, og, u, gates, gates_c = _in_proj(x2d, p, tm=tm, n_sub=n_sub, seg=seg)
    else:
        qk, v, og, u, gates, gates_c, p["w_glu"], p["w_out"] = _in_proj(
            x2d, p, tm=tm, n_sub=n_sub, seg=seg, cast=(p["w_glu_f32"], p["w_out_f32"]))

    s_pad = -(-s_len // chunk) * chunk
    extra = s_pad - s_len
    def seq3(a, mode="constant"):
        a = a.reshape(b, s_len, a.shape[-1])
        return a if extra == 0 else jnp.pad(a, ((0, 0), (0, extra), (0, 0)), mode=mode)
    g3 = gates.reshape(2 * SUBLANES, b, s_len)
    if extra:
        pad_t = ((0, 0), (0, 0), (0, extra))
        g3 = jnp.concatenate([
            jnp.pad(g3[:SUBLANES], pad_t, constant_values=-jnp.inf),
            jnp.pad(g3[SUBLANES:], pad_t, mode="edge")], axis=0)
    out_a, conv_o, c_o, n_o, m_o = _mlstm(
        seq3(qk), seq3(v), seq3(og), g3.reshape(2 * SUBLANES, b * s_pad),
        seq3(gates_c, "edge"), _pad_rows_front(conv0, SUBLANES), c0, n0,
        jnp.broadcast_to(_pad_rows_back(m0[..., None], SUBLANES), (b, SUBLANES, LANES)),
        p["w_conv"], p["b_conv"], p["g_mh"], chunk=chunk, valid=seg)
    out_a = out_a[:, :s_len].reshape(t, D_A)

    out_b, sre_o, sim_o = _s5(
        u.reshape(b, s_len, D_B), sre0.reshape(b, S5_N), sim0.reshape(b, S5_N),
        p["lbre"], p["lbim"], p["wb"], p["wcre"], p["wcim"], p["s5_d"],
        p["perm"], p["permt"], p["w_glu"], p["b_glu"], tb=tb)

    x1, hf = _out_proj(out_a, out_b.reshape(t, D_B), x2d, p["w_out"],
                       p["g_post_mix"], p["g_pre_ffn"], tm, 2 * n_sub)
    states = (
        conv_o[:, SUBLANES - (CONV_W - 1):, :],
        c_o,
        n_o,
        m_o[:, :A_HEADS, 0],
        sre_o.reshape(b, B_GROUPS, B_STATE),
        sim_o.reshape(b, B_GROUPS, B_STATE),
    )
    return x1, hf, states


def kernel(x_prompt, x_sample, state_conv, state_mlstm_c, state_mlstm_n, state_mlstm_m, state_s5_re, state_s5_im, g_pre_mix, w_in, w_conv, b_conv, b_igate, b_fgate, g_mh, s5_lam_re, s5_lam_im, s5_log_dt, s5_b_re, s5_b_im, s5_c_re, s5_c_im, s5_d, w_glu, b_glu, w_out, g_post_mix, g_pre_ffn, w_gate, w_up, w_down, g_post_ffn):
    l = 0
    bp = x_prompt.shape[0]
    lbre, lbim, bbt_re, bbt_im = _discretise(
        s5_lam_re[l], s5_lam_im[l], s5_log_dt[l],
        jnp.transpose(s5_b_re[l], (2, 0, 1)), jnp.transpose(s5_b_im[l], (2, 0, 1)))

    o1 = QK_W
    o2 = o1 + D_A
    o3 = o2 + D_A
    o4 = o3 + A_HEADS
    o5 = o4 + A_HEADS
    w = w_in[l]
    zb = jnp.zeros((SUBLANES - A_HEADS,), F32)
    zcols = lambda n: jnp.zeros((D_MODEL, n), F32)
    perm, permt = _row_permutation(bp)
    row = lambda a: a.reshape(1, -1)
    p = {
        "g_pre_mix": row(g_pre_mix[l]),
        "w_in": jnp.concatenate([w[:, :o3], w[:, o5:]], axis=1).astype(BF16),
        "wgc": jnp.concatenate([w[:, o3:o4], zcols(SUBLANES - A_HEADS), w[:, o4:o5],
                                zcols(LANES - SUBLANES - A_HEADS)], axis=1),
        "gbias": jnp.concatenate([b_igate[l], zb, b_fgate[l], zb]).reshape(2 * SUBLANES, 1),
        "w_conv": w_conv[l],
        "b_conv": row(b_conv[l]),
        "g_mh": row(g_mh[l]),
        "lbre": lbre.reshape(1, S5_N),
        "lbim": lbim.reshape(1, S5_N),
        "wb": jnp.concatenate([_block_diag_in(bbt_re), _block_diag_in(bbt_im)], axis=-1).astype(BF16),
        "wcre": _block_diag_out(s5_c_re[l]).astype(BF16),
        "wcim": _block_diag_out(s5_c_im[l]).astype(BF16),
        "s5_d": row(s5_d[l]),
        "perm": perm,
        "permt": permt,
        "w_glu_f32": w_glu[l],
        "b_glu": row(b_glu[l]),
        "w_out_f32": w_out[l],
        "g_post_mix": row(g_post_mix[l]),
        "g_pre_ffn": row(g_pre_ffn[l]),
    }

    x1_p, hf_p, st_p = _mix_layer(
        x_prompt,
        jnp.zeros((bp, CONV_W - 1, QK_W), F32),
        jnp.zeros((bp, A_HEADS, A_DV, A_DQK), F32),
        jnp.zeros((bp, A_HEADS, A_DQK), F32),
        jnp.zeros((bp, A_HEADS), F32),
        jnp.zeros((bp, B_GROUPS, B_STATE), F32),
        jnp.zeros((bp, B_GROUPS, B_STATE), F32),
        p, tm=512, n_sub=2, chunk=128, tb=64)
    x1_s, hf_s, st_s = _mix_layer(
        x_sample, state_conv[l], state_mlstm_c[l], state_mlstm_n[l], state_mlstm_m[l],
        state_s5_re[l], state_s5_im[l],
        p, tm=128, n_sub=1, chunk=128, tb=16)

    act_p, act_s, wd = _ffn_up(hf_p, hf_s, w_gate[l], w_up[l], w_down[l], tm=1024, tf=512)
    g_post = row(g_post_ffn[l])
    yp = _ffn_down(act_p, wd, x1_p, g_post, tm=512, n_sub=2).reshape(x_prompt.shape)
    ys = _ffn_down(act_s, wd, x1_s, g_post, tm=128, n_sub=1).reshape(x_sample.shape)
    return (yp, ys) + tuple(a[None] for a in st_p) + tuple(a[None] for a in st_s)
```

```python
import functools

import numpy as np
import jax
import jax.numpy as jnp
from jax import lax
from jax.experimental import pallas as pl
from jax.experimental.pallas import tpu as pltpu

F32 = jnp.float32
BF16 = jnp.bfloat16

D_MODEL = 2048
D_A = 1024
D_B = 1024
A_HEADS = 4
A_DV = 256
A_DQK = 128
QK_W = 1024
CONV_W = 4
B_GROUP = 16
B_GROUPS = 64
B_STATE = 64
S5_N = B_GROUPS * B_STATE
D_FF = 5632
EPS = 1e-6

SUBLANES = 8
LANES = 128
GROUPS_PER_BLOCK = LANES // B_GROUP
N_BLOCKS = B_GROUPS // GROUPS_PER_BLOCK
STATE_PER_BLOCK = GROUPS_PER_BLOCK * B_STATE
PERM_T = 16
VMEM_LIMIT = 56 * 1024 * 1024


def _sigmoid(x):
    return 1.0 / (1.0 + jnp.exp(-x))


def _gelu_tanh(x):
    c = np.sqrt(2 / np.pi).astype(np.float32)
    return x * (0.5 * (1.0 + jnp.tanh(c * (x + 0.044715 * (x ** 3)))))


def _rms(x, g):
    return x * lax.rsqrt(jnp.mean(x * x, axis=-1, keepdims=True) + EPS) * g


def _dot(a, b):
    return jnp.dot(a, b, preferred_element_type=F32)


def _log_sigmoid(z):
    return jnp.minimum(z, 0.0) - jnp.log1p(jnp.exp(-jnp.abs(z)))


def _split3(x):
    hi = x.astype(BF16)
    r = x - hi.astype(F32)
    mid = r.astype(BF16)
    lo = (r - mid.astype(F32)).astype(BF16)
    return hi, mid, lo


def _dot3_right(x, sel):
    hi, mid, lo = _split3(x)
    return _dot(hi, sel) + _dot(mid, sel) + _dot(lo, sel)


def _dot3_left(sel, x):
    hi, mid, lo = _split3(x)
    return _dot(sel, hi) + _dot(sel, mid) + _dot(sel, lo)


def _row_blocks(rows, n_sub):
    assert rows % n_sub == 0
    size = rows // n_sub
    return [slice(size * r, size * (r + 1)) for r in range(n_sub)]


def _resident(shape):
    zeros = (0,) * len(shape)
    return pl.BlockSpec(shape, lambda *_: zeros, pipeline_mode=pl.Buffered(1))


def _disc_kernel(lam_re_ref, lam_im_ref, logdt_ref, bre_ref, bim_ref,
                 lbre_ref, lbim_ref, bbre_ref, bbim_ref):
    lam_re = lam_re_ref[...]
    lam_im = lam_im_ref[...]
    dt = jnp.exp(logdt_ref[...])
    mag = jnp.exp(lam_re * dt)
    ang = lam_im * dt
    lb_re = mag * jnp.cos(ang)
    lb_im = mag * jnp.sin(ang)
    den = lam_re * lam_re + lam_im * lam_im
    nr = lb_re - 1.0
    f_re = (nr * lam_re + lb_im * lam_im) / den
    f_im = (lb_im * lam_re - nr * lam_im) / den
    lbre_ref[...] = lb_re
    lbim_ref[...] = lb_im
    b_re = bre_ref[...]
    b_im = bim_ref[...]
    bbre_ref[...] = f_re[None] * b_re - f_im[None] * b_im
    bbim_ref[...] = f_re[None] * b_im + f_im[None] * b_re


def _discretise(lam_re, lam_im, log_dt, bt_re, bt_im):
    gp = jax.ShapeDtypeStruct((B_GROUPS, B_STATE), F32)
    igp = jax.ShapeDtypeStruct((B_GROUP, B_GROUPS, B_STATE), F32)
    return pl.pallas_call(
        _disc_kernel, out_shape=(gp, gp, igp, igp), name="s5_discretise",
    )(lam_re, lam_im, log_dt.reshape(B_GROUPS, 1), bt_re, bt_im)


GATE_LANE_CMX = SUBLANES
GATE_LANE_FC = SUBLANES + A_HEADS


def _inproj_kernel(x_ref, gpre_ref, w_ref, wu_ref, wgc_ref, gb_ref, triu_ref, tril_ref, *rest,
                   n_sub, seg, n_cast):
    cast_in = rest[:n_cast]
    qk_out, v_out, og_out, u_out, g_out, c_out = rest[n_cast:n_cast + 6]
    cast_out = rest[n_cast + 6:2 * n_cast + 6]
    wgt = rest[2 * n_cast + 6]
    for src, dst in zip(cast_in, cast_out):
        dst[...] = src[...].astype(BF16)

    @pl.when(pl.program_id(0) == 0)
    def _():
        wgt[...] = wgc_ref[...].T[0:2 * SUBLANES, :].astype(BF16)

    for rows in _row_blocks(x_ref.shape[0], n_sub):
        hm = _rms(x_ref[rows, :], gpre_ref[...]).astype(BF16)

        n = rows.stop - rows.start
        z = lax.dot_general(wgt[...], hm, (((1,), (1,)), ((), ())),
                            preferred_element_type=F32) + gb_ref[...]
        qk_out[rows, :] = _dot(hm, w_ref[:, 0:QK_W])

        row = lax.broadcasted_iota(jnp.int32, z.shape, 0)
        zf = jnp.where(row >= SUBLANES, _log_sigmoid(z), z)
        zc = jnp.concatenate([zf, jnp.zeros((LANES - 2 * SUBLANES, n), F32)], axis=0).T
        lane = lax.broadcasted_iota(jnp.int32, zc.shape, 1)
        lf_c = jnp.where((lane >= SUBLANES) & (lane < SUBLANES + A_HEADS), zc, 0.0)
        fc = _dot3_right(zf, triu_ref[...])[SUBLANES:2 * SUBLANES]
        fc_c = _dot3_left(tril_ref[...], lf_c)
        v_out[rows, :] = _dot(hm, w_ref[:, QK_W:QK_W + D_A]).astype(BF16)
        og_out[rows, :] = _dot(hm, w_ref[:, QK_W + D_A:QK_W + 2 * D_A])
        u_out[rows, :] = _dot(hm, wu_ref[...])

        g_out[0:SUBLANES, rows] = zf[0:SUBLANES] - fc
        g_out[SUBLANES:2 * SUBLANES, rows] = fc
        cm = pltpu.roll(zc, SUBLANES, axis=1) - fc_c
        t_in = lax.broadcasted_iota(jnp.int32, zc.shape, 0) & (seg - 1)
        sh = 1
        while sh < seg:
            cm = jnp.where(t_in >= sh, jnp.maximum(cm, pltpu.roll(cm, sh, axis=0)), cm)
            sh *= 2
        c_out[rows, :] = jnp.where(lane < GATE_LANE_FC, cm, pltpu.roll(fc_c, A_HEADS, axis=1))


def _segment_prefix_matrices(n, seg):
    idx = np.arange(n)
    u = ((idx[:, None] // seg == idx[None, :] // seg) & (idx[:, None] <= idx[None, :]))
    u = u.astype(np.float32)
    return jnp.asarray(u, BF16), jnp.asarray(u.T, BF16)


def _in_proj(x2d, p, *, tm, n_sub, seg, cast=()):
    t = x2d.shape[0]
    n_steps = t // tm
    assert seg & (seg - 1) == 0
    triu, tril = _segment_prefix_matrices(tm // n_sub, seg)
    w_cols = QK_W + 2 * D_A
    tok = lambda i: (i, 0)
    cast_specs = [pl.BlockSpec((w.shape[0] // n_steps, w.shape[1]), tok) for w in cast]
    return pl.pallas_call(
        functools.partial(_inproj_kernel, n_sub=n_sub, seg=seg, n_cast=len(cast)),
        grid=(n_steps,),
        in_specs=[
            pl.BlockSpec((tm, D_MODEL), tok),
            _resident((1, D_MODEL)),
            _resident((D_MODEL, w_cols)),
            _resident((D_MODEL, D_B)),
            _resident((D_MODEL, LANES)),
            _resident((2 * SUBLANES, 1)),
            _resident(triu.shape),
            _resident(tril.shape),
        ] + cast_specs,
        out_specs=[
            pl.BlockSpec((tm, QK_W), tok),
            pl.BlockSpec((tm, D_A), tok),
            pl.BlockSpec((tm, D_A), tok),
            pl.BlockSpec((tm, D_B), tok),
            pl.BlockSpec((2 * SUBLANES, tm), lambda i: (0, i)),
            pl.BlockSpec((tm, LANES), tok),
        ] + cast_specs,
        out_shape=(
            jax.ShapeDtypeStruct((t, QK_W), F32),
            jax.ShapeDtypeStruct((t, D_A), BF16),
            jax.ShapeDtypeStruct((t, D_A), F32),
            jax.ShapeDtypeStruct((t, D_B), F32),
            jax.ShapeDtypeStruct((2 * SUBLANES, t), F32),
            jax.ShapeDtypeStruct((t, LANES), F32),
        ) + tuple(jax.ShapeDtypeStruct(w.shape, BF16) for w in cast),
        scratch_shapes=[pltpu.VMEM((2 * SUBLANES, D_MODEL), BF16)],
        compiler_params=pltpu.CompilerParams(
            dimension_semantics=("arbitrary",), vmem_limit_bytes=VMEM_LIMIT),
        name="in_proj",
    )(x2d, p["g_pre_mix"], p["w_in"], p["w_in_u"], p["wgc"], p["gbias"], triu, tril, *cast)


def _mlstm_kernel(qk_ref, v_ref, og_ref, g_ref, gc_ref, rep_ref, tail0_ref, c0_ref, n0_ref, m0_ref,
                  wconv_ref, bconv_ref, gmh_ref,
                  outa_ref, convo_ref, co_ref, no_ref, mo_ref,
                  xp, ct, ncr, m_s, *, chunk, valid, n_chunks):
    L = chunk
    c = pl.program_id(1)
    reps = L // LANES

    @pl.when(c == 0)
    def _():
        xp[0:SUBLANES, :] = tail0_ref[0]
        for h in range(A_HEADS):
            ct[h] = c0_ref[0, h].T
            ncr[h] = jnp.broadcast_to(n0_ref[0, h:h + 1, :], (A_DQK, A_DQK)).T
        m_s[...] = m0_ref[0]

    def lanes(x, n):
        return x if n == 1 else jnp.concatenate([x] * n, axis=1)

    xp[SUBLANES:SUBLANES + L, :] = qk_ref[0]
    xs = xp[...]
    acc = bconv_ref[...]
    for j in range(CONV_W):
        back = CONV_W - 1 - j
        tap = xs if back == 0 else pltpu.roll(xs, back, axis=0)
        acc = acc + tap[SUBLANES:, :] * wconv_ref[j:j + 1, :]
    qk = acc * _sigmoid(acc)
    q_all = (qk[:, :QK_W // 2] * (A_DQK ** -0.5)).astype(BF16)
    k_all = qk[:, QK_W // 2:].astype(BF16)

    @pl.when(c == n_chunks - 1)
    def _():
        convo_ref[0] = xp[valid:valid + SUBLANES, :]

    xp[0:SUBLANES, :] = xp[L:L + SUBLANES, :]

    rep = _dot3_right(gc_ref[0], rep_ref[...])

    g = g_ref[...]
    gr = g[0:SUBLANES]
    fc = g[SUBLANES:2 * SUBLANES]
    lane = lax.broadcasted_iota(jnp.int32, (SUBLANES, L), 1)
    f_last = jnp.sum(jnp.where(lane == L - 1, fc, 0.0), axis=1, keepdims=True)
    m0 = jnp.max(m_s[...], axis=1, keepdims=True)
    lw = f_last + gr
    m_new = jnp.maximum(f_last + m0, jnp.max(lw, axis=1, keepdims=True))
    ws = jnp.exp(lw - m_new)
    decay = jnp.broadcast_to(jnp.exp(f_last + m0 - m_new), (SUBLANES, LANES))

    row_i = lax.broadcasted_iota(jnp.int32, (L, L), 0)
    col_i = lax.broadcasted_iota(jnp.int32, (L, L), 1)
    tri = col_i <= row_i
    for h in range(A_HEADS):
        qb = q_all[:, A_DQK * h:A_DQK * (h + 1)]
        kb = k_all[:, A_DQK * h:A_DQK * (h + 1)]
        vb = v_ref[0, :, A_DV * h:A_DV * (h + 1)]
        m0_h = m_s[h:h + 1, :]
        mx = jnp.maximum(rep[:, LANES * h:LANES * (h + 1)], m0_h)
        fc_t = rep[:, LANES * (A_HEADS + h):LANES * (A_HEADS + h + 1)]
        e = jnp.exp(jnp.where(tri, gr[h:h + 1, :] - lanes(mx, reps), -jnp.inf))
        s = lax.dot_general(qb, kb, (((1,), (1,)), ((), ())),
                            preferred_element_type=F32) * e
        g_int = jnp.exp(m0_h - mx)
        ct_h = ct[h]
        ncr_h = ncr[h]
        num = (_dot(s.astype(BF16), vb)
               + lanes(g_int, A_DV // LANES) * _dot(qb, ct_h.astype(BF16)))
        den = jnp.sum(s, axis=1, keepdims=True) + g_int * _dot(qb, ncr_h.astype(BF16))
        scale = jnp.maximum(jnp.abs(den), jnp.exp(-(fc_t + mx)))
        hh = num / lanes(scale, A_DV // LANES)
        hn = _rms(hh, gmh_ref[:, A_DV * h:A_DV * (h + 1)])
        og = og_ref[0, :, A_DV * h:A_DV * (h + 1)]
        outa_ref[0, :, A_DV * h:A_DV * (h + 1)] = (hn * _sigmoid(og)).astype(BF16)
        kws = kb.astype(F32).T * ws[h:h + 1, :]
        dec_h = decay[h:h + 1, :]
        ct[h] = lanes(dec_h, A_DV // LANES) * ct_h + _dot(kws.astype(BF16), vb)
        ncr[h] = dec_h * ncr_h + jnp.sum(kws, axis=1, keepdims=True)
    m_s[...] = jnp.broadcast_to(m_new, m_s.shape)

    @pl.when(c == n_chunks - 1)
    def _():
        for h in range(A_HEADS):
            co_ref[0, h] = ct[h].T
            no_ref[0, h:h + 1, :] = ncr[h].T[0:1, :]
        mo_ref[0] = m_s[...]


def _gate_replication_matrix():
    r = np.zeros((LANES, 2 * A_HEADS * LANES), np.float32)
    for h in range(A_HEADS):
        r[GATE_LANE_CMX + h, LANES * h:LANES * (h + 1)] = 1.0
        r[GATE_LANE_FC + h, LANES * (A_HEADS + h):LANES * (A_HEADS + h + 1)] = 1.0
    return jnp.asarray(r, BF16)


def _mlstm(qk, v, og, gates, gates_c, tail0, c0, n0, m0, wconv, bconv, gmh, *, chunk, valid):
    b, s_len, _ = qk.shape
    n_chunks = s_len // chunk
    rep = _gate_replication_matrix()
    seq = lambda i, c: (i, c, 0)
    per_b3 = lambda i, c: (i, 0, 0)
    per_b4 = lambda i, c: (i, 0, 0, 0)
    const2 = lambda i, c: (0, 0)
    kernel = functools.partial(_mlstm_kernel, chunk=chunk, valid=valid, n_chunks=n_chunks)
    return pl.pallas_call(
        kernel,
        grid=(b, n_chunks),
        in_specs=[
            pl.BlockSpec((1, chunk, QK_W), seq),
            pl.BlockSpec((1, chunk, D_A), seq),
            pl.BlockSpec((1, chunk, D_A), seq),
            pl.BlockSpec((2 * SUBLANES, chunk), lambda i, c: (0, i * n_chunks + c)),
            pl.BlockSpec((1, chunk, LANES), seq),
            pl.BlockSpec(rep.shape, const2),
            pl.BlockSpec((1, SUBLANES, QK_W), per_b3),
            pl.BlockSpec((1, A_HEADS, A_DV, A_DQK), per_b4),
            pl.BlockSpec((1, A_HEADS, A_DQK), per_b3),
            pl.BlockSpec((1, SUBLANES, LANES), per_b3),
            pl.BlockSpec((CONV_W, QK_W), const2),
            pl.BlockSpec((1, QK_W), const2),
            pl.BlockSpec((1, D_A), const2),
        ],
        out_specs=[
            pl.BlockSpec((1, chunk, D_A), seq),
            pl.BlockSpec((1, SUBLANES, QK_W), per_b3),
            pl.BlockSpec((1, A_HEADS, A_DV, A_DQK), per_b4),
            pl.BlockSpec((1, A_HEADS, A_DQK), per_b3),
            pl.BlockSpec((1, SUBLANES, LANES), per_b3),
        ],
        out_shape=(
            jax.ShapeDtypeStruct((b, s_len, D_A), BF16),
            jax.ShapeDtypeStruct((b, SUBLANES, QK_W), F32),
            jax.ShapeDtypeStruct((b, A_HEADS, A_DV, A_DQK), F32),
            jax.ShapeDtypeStruct((b, A_HEADS, A_DQK), F32),
            jax.ShapeDtypeStruct((b, SUBLANES, LANES), F32),
        ),
        scratch_shapes=[
            pltpu.VMEM((chunk + SUBLANES, QK_W), F32),
            pltpu.VMEM((A_HEADS, A_DQK, A_DV), F32),
            pltpu.VMEM((A_HEADS, A_DQK, LANES), F32),
            pltpu.VMEM((SUBLANES, LANES), F32),
        ],
        compiler_params=pltpu.CompilerParams(
            dimension_semantics=("arbitrary", "arbitrary"), vmem_limit_bytes=VMEM_LIMIT),
        name="mlstm",
    )(qk, v, og, gates, gates_c, rep, tail0, c0, n0, m0, wconv, bconv, gmh)


def _s5_kernel(u_ref, s0re_ref, s0im_ref, lbre_ref, lbim_ref, wb_ref, wcre_ref, wcim_ref,
               d_ref, perm_ref, permt_ref, wglu_ref, bglu_ref,
               outb_ref, sre_out, sim_out,
               bure, buim, sre, sim, utb, xtb, ytb, *, tb, n_steps):
    step = pl.program_id(0)
    batch = u_ref.shape[0]
    rows_j = batch * PERM_T
    n_j = tb // PERM_T

    @pl.when(step == 0)
    def _():
        sre[...] = s0re_ref[...]
        sim[...] = s0im_ref[...]

    strip = 8 * LANES
    blocks_per_strip = strip // STATE_PER_BLOCK
    n_strips = S5_N // strip

    def permute_in():
        for j in range(n_j):
            xj = jnp.concatenate(
                [u_ref[b, PERM_T * j:PERM_T * (j + 1), :] for b in range(batch)], axis=0)
            hi = xj.astype(BF16)
            lo = (xj - hi.astype(F32)).astype(BF16)
            uh = _dot(perm_ref[...], hi)
            rows = slice(rows_j * j, rows_j * (j + 1))
            xtb[rows, :] = uh.astype(BF16)
            utb[rows, :] = uh + _dot(perm_ref[...], lo)

    def input_map(st):
        for k in range(blocks_per_strip * st, blocks_per_strip * (st + 1)):
            r = _dot(xtb[:, LANES * k:LANES * (k + 1)], wb_ref[k])
            kc = slice(STATE_PER_BLOCK * k, STATE_PER_BLOCK * (k + 1))
            bure[:, kc] = r[:, :STATE_PER_BLOCK]
            buim[:, kc] = r[:, STATE_PER_BLOCK:]

    def recurrence(st):
        cols = slice(strip * st, strip * (st + 1))
        a_re = jnp.broadcast_to(lbre_ref[:, cols], (batch, strip))
        a_im = jnp.broadcast_to(lbim_ref[:, cols], (batch, strip))
        s_r = sre[:, cols]
        s_i = sim[:, cols]
        for t in range(tb):
            tr = slice(t * batch, (t + 1) * batch)
            n_r = a_re * s_r - a_im * s_i + bure[tr, cols]
            n_i = a_re * s_i + a_im * s_r + buim[tr, cols]
            bure[tr, cols] = n_r
            buim[tr, cols] = n_i
            s_r, s_i = n_r, n_i
        sre[:, cols] = s_r
        sim[:, cols] = s_i

    def output_map(st):
        for k in range(blocks_per_strip * st, blocks_per_strip * (st + 1)):
            kc = slice(STATE_PER_BLOCK * k, STATE_PER_BLOCK * (k + 1))
            ytb[:, LANES * k:LANES * (k + 1)] = (
                _dot(bure[:, kc].astype(BF16), wcre_ref[k])
                - _dot(buim[:, kc].astype(BF16), wcim_ref[k]))

    permute_in()
    for st in range(n_strips):
        input_map(st)
        recurrence(st)
        output_map(st)

    z = _gelu_tanh(ytb[...] + d_ref[...] * utb[...])
    gate = _sigmoid(_dot(z.astype(BF16), wglu_ref[...]) + bglu_ref[...])
    res = (z * gate).astype(BF16)
    for j in range(n_j):
        rb = _dot(permt_ref[...], res[rows_j * j:rows_j * (j + 1), :]).astype(BF16)
        for b in range(batch):
            outb_ref[b, PERM_T * j:PERM_T * (j + 1), :] = rb[PERM_T * b:PERM_T * (b + 1), :]

    @pl.when(step == n_steps - 1)
    def _():
        sre_out[...] = sre[...]
        sim_out[...] = sim[...]


def _s5(u, s0re, s0im, lbre, lbim, wb, wcre, wcim, d_skip, perm, permt, wglu, bglu, *, tb):
    b, s_len, _ = u.shape
    n_steps = s_len // tb
    rows = b * tb
    kernel = functools.partial(_s5_kernel, tb=tb, n_steps=n_steps)
    state = jax.ShapeDtypeStruct((b, S5_N), F32)
    return pl.pallas_call(
        kernel,
        grid=(n_steps,),
        in_specs=[
            pl.BlockSpec((b, tb, D_B), lambda i: (0, i, 0)),
            _resident((b, S5_N)),
            _resident((b, S5_N)),
            _resident((1, S5_N)),
            _resident((1, S5_N)),
            _resident((N_BLOCKS, LANES, 2 * STATE_PER_BLOCK)),
            _resident((N_BLOCKS, STATE_PER_BLOCK, LANES)),
            _resident((N_BLOCKS, STATE_PER_BLOCK, LANES)),
            _resident((1, D_B)),
            _resident((b * PERM_T, b * PERM_T)),
            _resident((b * PERM_T, b * PERM_T)),
            _resident((D_B, D_B)),
            _resident((1, D_B)),
        ],
        out_specs=[
            pl.BlockSpec((b, tb, D_B), lambda i: (0, i, 0)),
            pl.BlockSpec((b, S5_N), lambda i: (0, 0)),
            pl.BlockSpec((b, S5_N), lambda i: (0, 0)),
        ],
        out_shape=(jax.ShapeDtypeStruct((b, s_len, D_B), BF16), state, state),
        scratch_shapes=[
            pltpu.VMEM((rows, S5_N), F32),
            pltpu.VMEM((rows, S5_N), F32),
            pltpu.VMEM((b, S5_N), F32),
            pltpu.VMEM((b, S5_N), F32),
            pltpu.VMEM((rows, D_B), F32),
            pltpu.VMEM((rows, D_B), BF16),
            pltpu.VMEM((rows, D_B), F32),
        ],
        compiler_params=pltpu.CompilerParams(
            dimension_semantics=("arbitrary",), vmem_limit_bytes=VMEM_LIMIT),
        name="s5",
    )(u, s0re, s0im, lbre, lbim, wb, wcre, wcim, d_skip, perm, permt, wglu, bglu)


def _outproj_kernel(a_ref, b_ref, x_ref, wo_ref, gpost_ref, gffn_ref, x1_ref, hf_ref, *, n_sub):
    for rows in _row_blocks(a_ref.shape[0], n_sub):
        mix = (_dot(a_ref[rows, :], wo_ref[0:D_A, :])
               + _dot(b_ref[rows, :], wo_ref[D_A:D_A + D_B, :]))
        x1 = x_ref[rows, :] + _rms(mix, gpost_ref[...])
        x1_ref[rows, :] = x1
        hf_ref[rows, :] = _rms(x1, gffn_ref[...]).astype(BF16)


def _out_proj(a, b, x2d, wo, g_post, g_ffn, tm, n_sub):
    t = x2d.shape[0]
    tok = lambda i: (i, 0)
    return pl.pallas_call(
        functools.partial(_outproj_kernel, n_sub=n_sub),
        grid=(t // tm,),
        in_specs=[
            pl.BlockSpec((tm, D_A), tok),
            pl.BlockSpec((tm, D_B), tok),
            pl.BlockSpec((tm, D_MODEL), tok),
            _resident((D_A + D_B, D_MODEL)),
            _resident((1, D_MODEL)),
            _resident((1, D_MODEL)),
        ],
        out_specs=[pl.BlockSpec((tm, D_MODEL), tok), pl.BlockSpec((tm, D_MODEL), tok)],
        out_shape=(jax.ShapeDtypeStruct((t, D_MODEL), F32),
                   jax.ShapeDtypeStruct((t, D_MODEL), BF16)),
        compiler_params=pltpu.CompilerParams(
            dimension_semantics=("arbitrary",), vmem_limit_bytes=VMEM_LIMIT),
        name="out_proj",
    )(a, b, x2d, wo, g_post, g_ffn)


def _ffn_up_kernel(hfp_ref, hfs_ref, wg_ref, wu_ref, wd_ref, actp_ref, acts_ref, wdb_ref,
                   wgb, wub, *, n_p):
    m = pl.program_id(1)

    @pl.when(m == 0)
    def _():
        wgb[...] = wg_ref[...].astype(BF16)
        wub[...] = wu_ref[...].astype(BF16)

    def body(hf_ref, act_ref):
        hf = hf_ref[...]
        a = _dot(hf, wgb[...])
        act_ref[...] = (a * _sigmoid(a) * _dot(hf, wub[...])).astype(BF16)

    @pl.when(m == 0)
    def _():
        body(hfs_ref, acts_ref)

    @pl.when(m > 0)
    def _():
        body(hfp_ref, actp_ref)
        n_rows = wd_ref.shape[0] // n_p
        r0 = pl.multiple_of((m - 1) * n_rows, n_rows)
        wdb_ref[pl.ds(r0, n_rows), :] = wd_ref[pl.ds(r0, n_rows), :].astype(BF16)


def _ffn_up(hf_p, hf_s, wg, wu, wd, tm, tf):
    t_p, t_s = hf_p.shape[0], hf_s.shape[0]
    n_p = t_p // tm
    n_f = D_FF // tf
    prompt_tile = lambda m: jnp.maximum(m - 1, 0)
    assert tf % n_p == 0
    return pl.pallas_call(
        functools.partial(_ffn_up_kernel, n_p=n_p),
        grid=(n_f, n_p + 1),
        in_specs=[
            pl.BlockSpec((tm, D_MODEL), lambda f, m: (prompt_tile(m), 0)),
            pl.BlockSpec((t_s, D_MODEL), lambda f, m: (0, 0)),
            pl.BlockSpec((D_MODEL, tf), lambda f, m: (0, f)),
            pl.BlockSpec((D_MODEL, tf), lambda f, m: (0, f)),
            pl.BlockSpec((tf, D_MODEL), lambda f, m: (f, 0)),
        ],
        out_specs=[
            pl.BlockSpec((tm, tf), lambda f, m: (prompt_tile(m), f)),
            pl.BlockSpec((t_s, tf), lambda f, m: (0, f)),
            pl.BlockSpec((tf, D_MODEL), lambda f, m: (f, 0)),
        ],
        out_shape=(jax.ShapeDtypeStruct((t_p, D_FF), BF16),
                   jax.ShapeDtypeStruct((t_s, D_FF), BF16),
                   jax.ShapeDtypeStruct((D_FF, D_MODEL), BF16)),
        scratch_shapes=[pltpu.VMEM((D_MODEL, tf), BF16), pltpu.VMEM((D_MODEL, tf), BF16)],
        compiler_params=pltpu.CompilerParams(
            dimension_semantics=("arbitrary", "arbitrary"), vmem_limit_bytes=VMEM_LIMIT),
        name="ffn_up",
    )(hf_p, hf_s, wg, wu, wd)


def _ffn_down_kernel(act_ref, wd_ref, x1_ref, gp_ref, out_ref, *, n_sub):
    for rows in _row_blocks(act_ref.shape[0], n_sub):
        ff = _dot(act_ref[rows, :], wd_ref[...])
        out_ref[rows, :] = x1_ref[rows, :] + _rms(ff, gp_ref[...])


def _ffn_down(act, wd, x1, g_post, tm, n_sub):
    t = act.shape[0]
    tok = lambda i: (i, 0)
    return pl.pallas_call(
        functools.partial(_ffn_down_kernel, n_sub=n_sub),
        grid=(t // tm,),
        in_specs=[
            pl.BlockSpec((tm, D_FF), tok),
            _resident((D_FF, D_MODEL)),
            pl.BlockSpec((tm, D_MODEL), tok),
            _resident((1, D_MODEL)),
        ],
        out_specs=pl.BlockSpec((tm, D_MODEL), tok),
        out_shape=jax.ShapeDtypeStruct((t, D_MODEL), F32),
        compiler_params=pltpu.CompilerParams(
            dimension_semantics=("arbitrary",), vmem_limit_bytes=VMEM_LIMIT),
        name="ffn_down",
    )(act, wd, x1, g_post)


def _block_diag_in(bbt):
    x = bbt.reshape(B_GROUP, N_BLOCKS, GROUPS_PER_BLOCK, B_STATE)
    x = jnp.transpose(x, (1, 2, 0, 3))
    eye = jnp.eye(GROUPS_PER_BLOCK, dtype=bool)[None, :, None, :, None]
    y = jnp.where(eye, x[:, :, :, None, :], 0.0)
    return y.reshape(N_BLOCKS, LANES, STATE_PER_BLOCK)


def _block_diag_out(c):
    x = c.reshape(N_BLOCKS, GROUPS_PER_BLOCK, B_GROUP, B_STATE)
    x = jnp.transpose(x, (0, 1, 3, 2))
    eye = jnp.eye(GROUPS_PER_BLOCK, dtype=bool)[None, :, None, :, None]
    y = jnp.where(eye, x[:, :, :, None, :], 0.0)
    return y.reshape(N_BLOCKS, STATE_PER_BLOCK, LANES)


def _row_permutation(batch):
    n = batch * PERM_T
    src = np.arange(n)
    b, t = src // PERM_T, src % PERM_T
    p = np.zeros((n, n), np.float32)
    p[t * batch + b, src] = 1.0
    return jnp.asarray(p, BF16), jnp.asarray(p.T, BF16)


def _pad_rows_front(x, rows):
    pad = [(0, 0)] * x.ndim
    pad[-2] = (rows - x.shape[-2], 0)
    return jnp.pad(x, pad)


def _pad_rows_back(x, rows):
    pad = [(0, 0)] * x.ndim
    pad[-2] = (0, rows - x.shape[-2])
    return jnp.pad(x, pad)


def _mix_layer(x, conv0, c0, n0, m0, sre0, sim0, p, *, tm, n_sub, chunk, tb):
    b, s_len, _ = x.shape
    t = b * s_len
    x2d = x.reshape(t, D_MODEL)
    assert s_len % chunk == 0 or s_len < chunk
    seg = min(s_len, chunk)
    if "w_glu" in p:
        qk, v, og, u, gates, gates_c = _in_proj(x2d, p, tm=tm, n_sub=n_sub, seg=seg)
    else:
        qk, v, og, u, gates, gates_c, p["w_glu"], p["w_out"] = _in_proj(
            x2d, p, tm=tm, n_sub=n_sub, seg=seg, cast=(p["w_glu_f32"], p["w_out_f32"]))

    s_pad = -(-s_len // chunk) * chunk
    extra = s_pad - s_len
    def seq3(a, mode="constant"):
        a = a.reshape(b, s_len, a.shape[-1])
        return a if extra == 0 else jnp.pad(a, ((0, 0), (0, extra), (0, 0)), mode=mode)
    g3 = gates.reshape(2 * SUBLANES, b, s_len)
    if extra:
        pad_t = ((0, 0), (0, 0), (0, extra))
        g3 = jnp.concatenate([
            jnp.pad(g3[:SUBLANES], pad_t, constant_values=-jnp.inf),
            jnp.pad(g3[SUBLANES:], pad_t, mode="edge")], axis=0)
    out_a, conv_o, c_o, n_o, m_o = _mlstm(
        seq3(qk), seq3(v), seq3(og), g3.reshape(2 * SUBLANES, b * s_pad),
        seq3(gates_c, "edge"), _pad_rows_front(conv0, SUBLANES), c0, n0,
        jnp.broadcast_to(_pad_rows_back(m0[..., None], SUBLANES), (b, SUBLANES, LANES)),
        p["w_conv"], p["b_conv"], p["g_mh"], chunk=chunk, valid=seg)
    out_a = out_a[:, :s_len].reshape(t, D_A)

    out_b, sre_o, sim_o = _s5(
        u.reshape(b, s_len, D_B), sre0.reshape(b, S5_N), sim0.reshape(b, S5_N),
        p["lbre"], p["lbim"], p["wb"], p["wcre"], p["wcim"], p["s5_d"],
        p["perm"], p["permt"], p["w_glu"], p["b_glu"], tb=tb)

    x1, hf = _out_proj(out_a, out_b.reshape(t, D_B), x2d, p["w_out"],
                       p["g_post_mix"], p["g_pre_ffn"], tm, 2 * n_sub)
    states = (
        conv_o[:, SUBLANES - (CONV_W - 1):, :],
        c_o,
        n_o,
        m_o[:, :A_HEADS, 0],
        sre_o.reshape(b, B_GROUPS, B_STATE),
        sim_o.reshape(b, B_GROUPS, B_STATE),
    )
    return x1, hf, states


def kernel(x_prompt, x_sample, state_conv, state_mlstm_c, state_mlstm_n, state_mlstm_m, state_s5_re, state_s5_im, g_pre_mix, w_in, w_conv, b_conv, b_igate, b_fgate, g_mh, s5_lam_re, s5_lam_im, s5_log_dt, s5_b_re, s5_b_im, s5_c_re, s5_c_im, s5_d, w_glu, b_glu, w_out, g_post_mix, g_pre_ffn, w_gate, w_up, w_down, g_post_ffn):
    l = 0
    bp = x_prompt.shape[0]
    lbre, lbim, bbt_re, bbt_im = _discretise(
        s5_lam_re[l], s5_lam_im[l], s5_log_dt[l],
        jnp.transpose(s5_b_re[l], (2, 0, 1)), jnp.transpose(s5_b_im[l], (2, 0, 1)))

    o1 = QK_W
    o2 = o1 + D_A
    o3 = o2 + D_A
    o4 = o3 + A_HEADS
    o5 = o4 + A_HEADS
    w = w_in[l]
    zb = jnp.zeros((SUBLANES - A_HEADS,), F32)
    zcols = lambda n: jnp.zeros((D_MODEL, n), F32)
    perm, permt = _row_permutation(bp)
    row = lambda a: a.reshape(1, -1)
    p = {
        "g_pre_mix": row(g_pre_mix[l]),
        "w_in": w[:, :o3].astype(BF16),
        "w_in_u": w[:, o5:].astype(BF16),
        "wgc": jnp.concatenate([w[:, o3:o4], zcols(SUBLANES - A_HEADS), w[:, o4:o5],
                                zcols(LANES - SUBLANES - A_HEADS)], axis=1),
        "gbias": jnp.concatenate([b_igate[l], zb, b_fgate[l], zb]).reshape(2 * SUBLANES, 1),
        "w_conv": w_conv[l],
        "b_conv": row(b_conv[l]),
        "g_mh": row(g_mh[l]),
        "lbre": lbre.reshape(1, S5_N),
        "lbim": lbim.reshape(1, S5_N),
        "wb": jnp.concatenate([_block_diag_in(bbt_re), _block_diag_in(bbt_im)], axis=-1).astype(BF16),
        "wcre": _block_diag_out(s5_c_re[l]).astype(BF16),
        "wcim": _block_diag_out(s5_c_im[l]).astype(BF16),
        "s5_d": row(s5_d[l]),
        "perm": perm,
        "permt": permt,
        "w_glu_f32": w_glu[l],
        "b_glu": row(b_glu[l]),
        "w_out_f32": w_out[l],
        "g_post_mix": row(g_post_mix[l]),
        "g_pre_ffn": row(g_pre_ffn[l]),
    }

    x1_p, hf_p, st_p = _mix_layer(
        x_prompt,
        jnp.zeros((bp, CONV_W - 1, QK_W), F32),
        jnp.zeros((bp, A_HEADS, A_DV, A_DQK), F32),
        jnp.zeros((bp, A_HEADS, A_DQK), F32),
        jnp.zeros((bp, A_HEADS), F32),
        jnp.zeros((bp, B_GROUPS, B_STATE), F32),
        jnp.zeros((bp, B_GROUPS, B_STATE), F32),
        p, tm=512, n_sub=2, chunk=128, tb=64)
    x1_s, hf_s, st_s = _mix_layer(
        x_sample, state_conv[l], state_mlstm_c[l], state_mlstm_n[l], state_mlstm_m[l],
        state_s5_re[l], state_s5_im[l],
        p, tm=128, n_sub=1, chunk=128, tb=16)

    act_p, act_s, wd = _ffn_up(hf_p, hf_s, w_gate[l], w_up[l], w_down[l], tm=1024, tf=512)
    g_post = row(g_post_ffn[l])
    yp = _ffn_down(act_p, wd, x1_p, g_post, tm=512, n_sub=2).reshape(x_prompt.shape)
    ys = _ffn_down(act_s, wd, x1_s, g_post, tm=128, n_sub=1).reshape(x_sample.shape)
    return (yp, ys) + tuple(a[None] for a in st_p) + tuple(a[None] for a in st_s)
```

```python
import functools

import numpy as np
import jax
import jax.numpy as jnp
from jax import lax
from jax.experimental import pallas as pl
from jax.experimental.pallas import tpu as pltpu

F32 = jnp.float32
BF16 = jnp.bfloat16

D_MODEL = 2048
D_A = 1024
D_B = 1024
A_HEADS = 4
A_DV = 256
A_DQK = 128
QK_W = 1024
CONV_W = 4
B_GROUP = 16
B_GROUPS = 64
B_STATE = 64
S5_N = B_GROUPS * B_STATE
D_FF = 5632
EPS = 1e-6

SUBLANES = 8
LANES = 128
GROUPS_PER_BLOCK = LANES // B_GROUP
N_BLOCKS = B_GROUPS // GROUPS_PER_BLOCK
STATE_PER_BLOCK = GROUPS_PER_BLOCK * B_STATE
PERM_T = 16
VMEM_LIMIT = 56 * 1024 * 1024


def _sigmoid(x):
    return 1.0 / (1.0 + jnp.exp(-x))


def _gelu_tanh(x):
    c = np.sqrt(2 / np.pi).astype(np.float32)
    return x * (0.5 * (1.0 + jnp.tanh(c * (x + 0.044715 * (x ** 3)))))


def _rms(x, g):
    return x * lax.rsqrt(jnp.mean(x * x, axis=-1, keepdims=True) + EPS) * g


def _dot(a, b):
    return jnp.dot(a, b, preferred_element_type=F32)


def _log_sigmoid(z):
    return jnp.minimum(z, 0.0) - jnp.log1p(jnp.exp(-jnp.abs(z)))


def _split3(x):
    hi = x.astype(BF16)
    r = x - hi.astype(F32)
    mid = r.astype(BF16)
    lo = (r - mid.astype(F32)).astype(BF16)
    return hi, mid, lo


def _dot3_right(x, sel):
    hi, mid, lo = _split3(x)
    return _dot(hi, sel) + _dot(mid, sel) + _dot(lo, sel)


def _dot3_left(sel, x):
    hi, mid, lo = _split3(x)
    return _dot(sel, hi) + _dot(sel, mid) + _dot(sel, lo)


def _row_blocks(rows, n_sub):
    assert rows % n_sub == 0
    size = rows // n_sub
    return [slice(size * r, size * (r + 1)) for r in range(n_sub)]


def _resident(shape):
    zeros = (0,) * len(shape)
    return pl.BlockSpec(shape, lambda *_: zeros, pipeline_mode=pl.Buffered(1))


def _disc_kernel(lam_re_ref, lam_im_ref, logdt_ref, bre_ref, bim_ref,
                 lbre_ref, lbim_ref, bbre_ref, bbim_ref):
    lam_re = lam_re_ref[...]
    lam_im = lam_im_ref[...]
    dt = jnp.exp(logdt_ref[...])
    mag = jnp.exp(lam_re * dt)
    ang = lam_im * dt
    lb_re = mag * jnp.cos(ang)
    lb_im = mag * jnp.sin(ang)
    den = lam_re * lam_re + lam_im * lam_im
    nr = lb_re - 1.0
    f_re = (nr * lam_re + lb_im * lam_im) / den
    f_im = (lb_im * lam_re - nr * lam_im) / den
    lbre_ref[...] = lb_re
    lbim_ref[...] = lb_im
    b_re = bre_ref[...]
    b_im = bim_ref[...]
    bbre_ref[...] = f_re[None] * b_re - f_im[None] * b_im
    bbim_ref[...] = f_re[None] * b_im + f_im[None] * b_re


def _discretise(lam_re, lam_im, log_dt, bt_re, bt_im):
    gp = jax.ShapeDtypeStruct((B_GROUPS, B_STATE), F32)
    igp = jax.ShapeDtypeStruct((B_GROUP, B_GROUPS, B_STATE), F32)
    return pl.pallas_call(
        _disc_kernel, out_shape=(gp, gp, igp, igp), name="s5_discretise",
    )(lam_re, lam_im, log_dt.reshape(B_GROUPS, 1), bt_re, bt_im)


GATE_COL = QK_W + 2 * D_A
GATE_LANE_CMX = A_HEADS
GATE_LANE_FC = 2 * A_HEADS


def _inproj_kernel(x_ref, gpre_ref, w_ref, wu_ref, gb_ref, triu_ref, tril_ref, *rest,
                   n_sub, seg, n_cast):
    cast_in = rest[:n_cast]
    qk_out, v_out, og_out, u_out, g_out, c_out = rest[n_cast:n_cast + 6]
    cast_out = rest[n_cast + 6:2 * n_cast + 6]
    wgt = rest[2 * n_cast + 6]
    for src, dst in zip(cast_in, cast_out):
        dst[...] = src[...].astype(BF16)

    @pl.when(pl.program_id(0) == 0)
    def _():
        wg = w_ref[:, GATE_COL:GATE_COL + LANES].astype(F32)
        wgt[...] = wg.T[0:2 * SUBLANES, :].astype(BF16)

    for rows in _row_blocks(x_ref.shape[0], n_sub):
        hm = _rms(x_ref[rows, :], gpre_ref[...]).astype(BF16)

        n = rows.stop - rows.start
        z = lax.dot_general(wgt[...], hm, (((1,), (1,)), ((), ())),
                            preferred_element_type=F32) + gb_ref[...]
        qk_out[rows, :] = _dot(hm, w_ref[:, 0:QK_W])

        row = lax.broadcasted_iota(jnp.int32, z.shape, 0)
        zf = jnp.where((row >= A_HEADS) & (row < 2 * A_HEADS), _log_sigmoid(z), z)
        zf8 = zf[0:SUBLANES]
        zc = jnp.concatenate([zf8, jnp.zeros((LANES - SUBLANES, n), F32)], axis=0).T
        lane = lax.broadcasted_iota(jnp.int32, zc.shape, 1)
        lf_c = jnp.where((lane >= A_HEADS) & (lane < 2 * A_HEADS), zc, 0.0)
        fc = pltpu.roll(_dot3_right(zf, triu_ref[...])[0:SUBLANES], A_HEADS, axis=0)
        fc_c = _dot3_left(tril_ref[...], lf_c)
        v_out[rows, :] = _dot(hm, w_ref[:, QK_W:QK_W + D_A]).astype(BF16)
        og_out[rows, :] = _dot(hm, w_ref[:, QK_W + D_A:QK_W + 2 * D_A])
        u_out[rows, :] = _dot(hm, wu_ref[...])

        g_out[0:SUBLANES, rows] = zf8 - fc
        g_out[SUBLANES:2 * SUBLANES, rows] = fc
        cm = pltpu.roll(zc, A_HEADS, axis=1) - fc_c
        t_in = lax.broadcasted_iota(jnp.int32, zc.shape, 0) & (seg - 1)
        sh = 1
        while sh < seg:
            cm = jnp.where(t_in >= sh, jnp.maximum(cm, pltpu.roll(cm, sh, axis=0)), cm)
            sh *= 2
        c_out[rows, :] = jnp.where(lane < GATE_LANE_FC, cm, pltpu.roll(fc_c, A_HEADS, axis=1))


def _segment_prefix_matrices(n, seg):
    idx = np.arange(n)
    u = ((idx[:, None] // seg == idx[None, :] // seg) & (idx[:, None] <= idx[None, :]))
    u = u.astype(np.float32)
    return jnp.asarray(u, BF16), jnp.asarray(u.T, BF16)


def _in_proj(x2d, p, *, tm, n_sub, seg, cast=()):
    t = x2d.shape[0]
    n_steps = t // tm
    assert seg & (seg - 1) == 0
    triu, tril = _segment_prefix_matrices(tm // n_sub, seg)
    w_cols = GATE_COL + LANES
    tok = lambda i: (i, 0)
    cast_specs = [pl.BlockSpec((w.shape[0] // n_steps, w.shape[1]), tok) for w in cast]
    return pl.pallas_call(
        functools.partial(_inproj_kernel, n_sub=n_sub, seg=seg, n_cast=len(cast)),
        grid=(n_steps,),
        in_specs=[
            pl.BlockSpec((tm, D_MODEL), tok),
            _resident((1, D_MODEL)),
            _resident((D_MODEL, w_cols)),
            _resident((D_MODEL, D_B)),
            _resident((2 * SUBLANES, 1)),
            _resident(triu.shape),
            _resident(tril.shape),
        ] + cast_specs,
        out_specs=[
            pl.BlockSpec((tm, QK_W), tok),
            pl.BlockSpec((tm, D_A), tok),
            pl.BlockSpec((tm, D_A), tok),
            pl.BlockSpec((tm, D_B), tok),
            pl.BlockSpec((2 * SUBLANES, tm), lambda i: (0, i)),
            pl.BlockSpec((tm, LANES), tok),
        ] + cast_specs,
        out_shape=(
            jax.ShapeDtypeStruct((t, QK_W), F32),
            jax.ShapeDtypeStruct((t, D_A), BF16),
            jax.ShapeDtypeStruct((t, D_A), F32),
            jax.ShapeDtypeStruct((t, D_B), F32),
            jax.ShapeDtypeStruct((2 * SUBLANES, t), F32),
            jax.ShapeDtypeStruct((t, LANES), F32),
        ) + tuple(jax.ShapeDtypeStruct(w.shape, BF16) for w in cast),
        scratch_shapes=[pltpu.VMEM((2 * SUBLANES, D_MODEL), BF16)],
        compiler_params=pltpu.CompilerParams(
            dimension_semantics=("arbitrary",), vmem_limit_bytes=VMEM_LIMIT),
        name="in_proj",
    )(x2d, p["g_pre_mix"], p["w_in"], p["w_in_u"], p["gbias"], triu, tril, *cast)


def _mlstm_kernel(qk_ref, v_ref, og_ref, g_ref, gc_ref, rep_ref, tail0_ref, c0_ref, n0_ref, m0_ref,
                  wconv_ref, bconv_ref, gmh_ref,
                  outa_ref, convo_ref, co_ref, no_ref, mo_ref,
                  xp, ct, ncr, m_s, *, chunk, valid, n_chunks):
    L = chunk
    c = pl.program_id(1)
    reps = L // LANES

    @pl.when(c == 0)
    def _():
        xp[0:SUBLANES, :] = tail0_ref[0]
        for h in range(A_HEADS):
            ct[h] = c0_ref[0, h].T
            ncr[h] = jnp.broadcast_to(n0_ref[0, h:h + 1, :], (A_DQK, A_DQK)).T
        m_s[...] = m0_ref[0]

    def lanes(x, n):
        return x if n == 1 else jnp.concatenate([x] * n, axis=1)

    xp[SUBLANES:SUBLANES + L, :] = qk_ref[0]
    xs = xp[...]
    acc = bconv_ref[...]
    for j in range(CONV_W):
        back = CONV_W - 1 - j
        tap = xs if back == 0 else pltpu.roll(xs, back, axis=0)
        acc = acc + tap[SUBLANES:, :] * wconv_ref[j:j + 1, :]
    qk = acc * _sigmoid(acc)
    q_all = (qk[:, :QK_W // 2] * (A_DQK ** -0.5)).astype(BF16)
    k_all = qk[:, QK_W // 2:].astype(BF16)

    @pl.when(c == n_chunks - 1)
    def _():
        convo_ref[0] = xp[valid:valid + SUBLANES, :]

    xp[0:SUBLANES, :] = xp[L:L + SUBLANES, :]

    rep = _dot3_right(gc_ref[0], rep_ref[...])

    g = g_ref[...]
    gr = g[0:SUBLANES]
    fc = g[SUBLANES:2 * SUBLANES]
    lane = lax.broadcasted_iota(jnp.int32, (SUBLANES, L), 1)
    f_last = jnp.sum(jnp.where(lane == L - 1, fc, 0.0), axis=1, keepdims=True)
    m0 = jnp.max(m_s[...], axis=1, keepdims=True)
    lw = f_last + gr
    m_new = jnp.maximum(f_last + m0, jnp.max(lw, axis=1, keepdims=True))
    ws = jnp.exp(lw - m_new)
    decay = jnp.broadcast_to(jnp.exp(f_last + m0 - m_new), (SUBLANES, LANES))

    row_i = lax.broadcasted_iota(jnp.int32, (L, L), 0)
    col_i = lax.broadcasted_iota(jnp.int32, (L, L), 1)
    tri = col_i <= row_i
    for h in range(A_HEADS):
        qb = q_all[:, A_DQK * h:A_DQK * (h + 1)]
        kb = k_all[:, A_DQK * h:A_DQK * (h + 1)]
        vb = v_ref[0, :, A_DV * h:A_DV * (h + 1)]
        m0_h = m_s[h:h + 1, :]
        mx = jnp.maximum(rep[:, LANES * h:LANES * (h + 1)], m0_h)
        fc_t = rep[:, LANES * (A_HEADS + h):LANES * (A_HEADS + h + 1)]
        e = jnp.exp(jnp.where(tri, gr[h:h + 1, :] - lanes(mx, reps), -jnp.inf))
        s = lax.dot_general(qb, kb, (((1,), (1,)), ((), ())),
                            preferred_element_type=F32) * e
        g_int = jnp.exp(m0_h - mx)
        ct_h = ct[h]
        ncr_h = ncr[h]
        num = (_dot(s.astype(BF16), vb)
               + lanes(g_int, A_DV // LANES) * _dot(qb, ct_h.astype(BF16)))
        den = jnp.sum(s, axis=1, keepdims=True) + g_int * _dot(qb, ncr_h.astype(BF16))
        scale = jnp.maximum(jnp.abs(den), jnp.exp(-(fc_t + mx)))
        hh = num / lanes(scale, A_DV // LANES)
        hn = _rms(hh, gmh_ref[:, A_DV * h:A_DV * (h + 1)])
        og = og_ref[0, :, A_DV * h:A_DV * (h + 1)]
        outa_ref[0, :, A_DV * h:A_DV * (h + 1)] = (hn * _sigmoid(og)).astype(BF16)
        kws = kb.astype(F32).T * ws[h:h + 1, :]
        dec_h = decay[h:h + 1, :]
        ct[h] = lanes(dec_h, A_DV // LANES) * ct_h + _dot(kws.astype(BF16), vb)
        ncr[h] = dec_h * ncr_h + jnp.sum(kws, axis=1, keepdims=True)
    m_s[...] = jnp.broadcast_to(m_new, m_s.shape)

    @pl.when(c == n_chunks - 1)
    def _():
        for h in range(A_HEADS):
            co_ref[0, h] = ct[h].T
            no_ref[0, h:h + 1, :] = ncr[h].T[0:1, :]
        mo_ref[0] = m_s[...]


def _gate_replication_matrix():
    r = np.zeros((LANES, 2 * A_HEADS * LANES), np.float32)
    for h in range(A_HEADS):
        r[GATE_LANE_CMX + h, LANES * h:LANES * (h + 1)] = 1.0
        r[GATE_LANE_FC + h, LANES * (A_HEADS + h):LANES * (A_HEADS + h + 1)] = 1.0
    return jnp.asarray(r, BF16)


def _mlstm(qk, v, og, gates, gates_c, tail0, c0, n0, m0, wconv, bconv, gmh, *, chunk, valid):
    b, s_len, _ = qk.shape
    n_chunks = s_len // chunk
    rep = _gate_replication_matrix()
    seq = lambda i, c: (i, c, 0)
    per_b3 = lambda i, c: (i, 0, 0)
    per_b4 = lambda i, c: (i, 0, 0, 0)
    const2 = lambda i, c: (0, 0)
    kernel = functools.partial(_mlstm_kernel, chunk=chunk, valid=valid, n_chunks=n_chunks)
    return pl.pallas_call(
        kernel,
        grid=(b, n_chunks),
        in_specs=[
            pl.BlockSpec((1, chunk, QK_W), seq),
            pl.BlockSpec((1, chunk, D_A), seq),
            pl.BlockSpec((1, chunk, D_A), seq),
            pl.BlockSpec((2 * SUBLANES, chunk), lambda i, c: (0, i * n_chunks + c)),
            pl.BlockSpec((1, chunk, LANES), seq),
            pl.BlockSpec(rep.shape, const2),
            pl.BlockSpec((1, SUBLANES, QK_W), per_b3),
            pl.BlockSpec((1, A_HEADS, A_DV, A_DQK), per_b4),
            pl.BlockSpec((1, A_HEADS, A_DQK), per_b3),
            pl.BlockSpec((1, SUBLANES, LANES), per_b3),
            pl.BlockSpec((CONV_W, QK_W), const2),
            pl.BlockSpec((1, QK_W), const2),
            pl.BlockSpec((1, D_A), const2),
        ],
        out_specs=[
            pl.BlockSpec((1, chunk, D_A), seq),
            pl.BlockSpec((1, SUBLANES, QK_W), per_b3),
            pl.BlockSpec((1, A_HEADS, A_DV, A_DQK), per_b4),
            pl.BlockSpec((1, A_HEADS, A_DQK), per_b3),
            pl.BlockSpec((1, SUBLANES, LANES), per_b3),
        ],
        out_shape=(
            jax.ShapeDtypeStruct((b, s_len, D_A), BF16),
            jax.ShapeDtypeStruct((b, SUBLANES, QK_W), F32),
            jax.ShapeDtypeStruct((b, A_HEADS, A_DV, A_DQK), F32),
            jax.ShapeDtypeStruct((b, A_HEADS, A_DQK), F32),
            jax.ShapeDtypeStruct((b, SUBLANES, LANES), F32),
        ),
        scratch_shapes=[
            pltpu.VMEM((chunk + SUBLANES, QK_W), F32),
            pltpu.VMEM((A_HEADS, A_DQK, A_DV), F32),
            pltpu.VMEM((A_HEADS, A_DQK, LANES), F32),
            pltpu.VMEM((SUBLANES, LANES), F32),
        ],
        compiler_params=pltpu.CompilerParams(
            dimension_semantics=("arbitrary", "arbitrary"), vmem_limit_bytes=VMEM_LIMIT),
        name="mlstm",
    )(qk, v, og, gates, gates_c, rep, tail0, c0, n0, m0, wconv, bconv, gmh)


def _s5_kernel(u_ref, s0re_ref, s0im_ref, lbre_ref, lbim_ref, wb_ref, wcre_ref, wcim_ref,
               d_ref, perm_ref, permt_ref, wglu_ref, bglu_ref,
               outb_ref, sre_out, sim_out,
               bure, buim, sre, sim, utb, xtb, ytb, *, tb, n_steps):
    step = pl.program_id(0)
    batch = u_ref.shape[0]
    rows_j = batch * PERM_T
    n_j = tb // PERM_T

    @pl.when(step == 0)
    def _():
        sre[...] = s0re_ref[...]
        sim[...] = s0im_ref[...]

    strip = 8 * LANES
    blocks_per_strip = strip // STATE_PER_BLOCK
    n_strips = S5_N // strip

    def permute_in():
        for j in range(n_j):
            xj = jnp.concatenate(
                [u_ref[b, PERM_T * j:PERM_T * (j + 1), :] for b in range(batch)], axis=0)
            hi = xj.astype(BF16)
            lo = (xj - hi.astype(F32)).astype(BF16)
            uh = _dot(perm_ref[...], hi)
            rows = slice(rows_j * j, rows_j * (j + 1))
            xtb[rows, :] = uh.astype(BF16)
            utb[rows, :] = uh + _dot(perm_ref[...], lo)

    def input_map(st):
        for k in range(blocks_per_strip * st, blocks_per_strip * (st + 1)):
            r = _dot(xtb[:, LANES * k:LANES * (k + 1)], wb_ref[k])
            kc = slice(STATE_PER_BLOCK * k, STATE_PER_BLOCK * (k + 1))
            bure[:, kc] = r[:, :STATE_PER_BLOCK]
            buim[:, kc] = r[:, STATE_PER_BLOCK:]

    def recurrence(st):
        cols = slice(strip * st, strip * (st + 1))
        a_re = jnp.broadcast_to(lbre_ref[:, cols], (batch, strip))
        a_im = jnp.broadcast_to(lbim_ref[:, cols], (batch, strip))
        s_r = sre[:, cols]
        s_i = sim[:, cols]
        for t in range(tb):
            tr = slice(t * batch, (t + 1) * batch)
            n_r = a_re * s_r - a_im * s_i + bure[tr, cols]
            n_i = a_re * s_i + a_im * s_r + buim[tr, cols]
            bure[tr, cols] = n_r
            buim[tr, cols] = n_i
            s_r, s_i = n_r, n_i
        sre[:, cols] = s_r
        sim[:, cols] = s_i

    def output_map(st):
        for k in range(blocks_per_strip * st, blocks_per_strip * (st + 1)):
            kc = slice(STATE_PER_BLOCK * k, STATE_PER_BLOCK * (k + 1))
            ytb[:, LANES * k:LANES * (k + 1)] = (
                _dot(bure[:, kc].astype(BF16), wcre_ref[k])
                - _dot(buim[:, kc].astype(BF16), wcim_ref[k]))

    permute_in()
    for st in range(n_strips):
        input_map(st)
        recurrence(st)
        output_map(st)

    z = _gelu_tanh(ytb[...] + d_ref[...] * utb[...])
    gate = _sigmoid(_dot(z.astype(BF16), wglu_ref[...]) + bglu_ref[...])
    res = (z * gate).astype(BF16)
    for j in range(n_j):
        rb = _dot(permt_ref[...], res[rows_j * j:rows_j * (j + 1), :]).astype(BF16)
        for b in range(batch):
            outb_ref[b, PERM_T * j:PERM_T * (j + 1), :] = rb[PERM_T * b:PERM_T * (b + 1), :]

    @pl.when(step == n_steps - 1)
    def _():
        sre_out[...] = sre[...]
        sim_out[...] = sim[...]


def _s5(u, s0re, s0im, lbre, lbim, wb, wcre, wcim, d_skip, perm, permt, wglu, bglu, *, tb):
    b, s_len, _ = u.shape
    n_steps = s_len // tb
    rows = b * tb
    kernel = functools.partial(_s5_kernel, tb=tb, n_steps=n_steps)
    state = jax.ShapeDtypeStruct((b, S5_N), F32)
    return pl.pallas_call(
        kernel,
        grid=(n_steps,),
        in_specs=[
            pl.BlockSpec((b, tb, D_B), lambda i: (0, i, 0)),
            _resident((b, S5_N)),
            _resident((b, S5_N)),
            _resident((1, S5_N)),
            _resident((1, S5_N)),
            _resident((N_BLOCKS, LANES, 2 * STATE_PER_BLOCK)),
            _resident((N_BLOCKS, STATE_PER_BLOCK, LANES)),
            _resident((N_BLOCKS, STATE_PER_BLOCK, LANES)),
            _resident((1, D_B)),
            _resident((b * PERM_T, b * PERM_T)),
            _resident((b * PERM_T, b * PERM_T)),
            _resident((D_B, D_B)),
            _resident((1, D_B)),
        ],
        out_specs=[
            pl.BlockSpec((b, tb, D_B), lambda i: (0, i, 0)),
            pl.BlockSpec((b, S5_N), lambda i: (0, 0)),
            pl.BlockSpec((b, S5_N), lambda i: (0, 0)),
        ],
        out_shape=(jax.ShapeDtypeStruct((b, s_len, D_B), BF16), state, state),
        scratch_shapes=[
            pltpu.VMEM((rows, S5_N), F32),
            pltpu.VMEM((rows, S5_N), F32),
            pltpu.VMEM((b, S5_N), F32),
            pltpu.VMEM((b, S5_N), F32),
            pltpu.VMEM((rows, D_B), F32),
            pltpu.VMEM((rows, D_B), BF16),
            pltpu.VMEM((rows, D_B), F32),
        ],
        compiler_params=pltpu.CompilerParams(
            dimension_semantics=("arbitrary",), vmem_limit_bytes=VMEM_LIMIT),
        name="s5",
    )(u, s0re, s0im, lbre, lbim, wb, wcre, wcim, d_skip, perm, permt, wglu, bglu)


def _outproj_kernel(a_ref, b_ref, x_ref, wo_ref, gpost_ref, gffn_ref, *rest, n_sub, n_cast):
    cast_in = rest[:n_cast]
    x1_ref, hf_ref = rest[n_cast:n_cast + 2]
    cast_out = rest[n_cast + 2:]
    for src, dst in zip(cast_in, cast_out):
        dst[...] = src[...].astype(BF16)
    for rows in _row_blocks(a_ref.shape[0], n_sub):
        mix = (_dot(a_ref[rows, :], wo_ref[0:D_A, :])
               + _dot(b_ref[rows, :], wo_ref[D_A:D_A + D_B, :]))
        x1 = x_ref[rows, :] + _rms(mix, gpost_ref[...])
        x1_ref[rows, :] = x1
        hf_ref[rows, :] = _rms(x1, gffn_ref[...]).astype(BF16)


def _out_proj(a, b, x2d, wo, g_post, g_ffn, tm, n_sub, cast=()):
    t = x2d.shape[0]
    n_steps = t // tm
    tok = lambda i: (i, 0)
    cast_specs = [pl.BlockSpec((w.shape[0] // n_steps, w.shape[1]), tok) for w in cast]
    return pl.pallas_call(
        functools.partial(_outproj_kernel, n_sub=n_sub, n_cast=len(cast)),
        grid=(n_steps,),
        in_specs=[
            pl.BlockSpec((tm, D_A), tok),
            pl.BlockSpec((tm, D_B), tok),
            pl.BlockSpec((tm, D_MODEL), tok),
            _resident((D_A + D_B, D_MODEL)),
            _resident((1, D_MODEL)),
            _resident((1, D_MODEL)),
        ] + cast_specs,
        out_specs=[pl.BlockSpec((tm, D_MODEL), tok), pl.BlockSpec((tm, D_MODEL), tok)] + cast_specs,
        out_shape=(jax.ShapeDtypeStruct((t, D_MODEL), F32),
                   jax.ShapeDtypeStruct((t, D_MODEL), BF16))
        + tuple(jax.ShapeDtypeStruct(w.shape, BF16) for w in cast),
        compiler_params=pltpu.CompilerParams(
            dimension_semantics=("arbitrary",), vmem_limit_bytes=VMEM_LIMIT),
        name="out_proj",
    )(a, b, x2d, wo, g_post, g_ffn, *cast)


def _ffn_up_kernel(hfp_ref, hfs_ref, wg_ref, wu_ref, actp_ref, acts_ref, wgb, wub):
    m = pl.program_id(1)

    @pl.when(m == 0)
    def _():
        wgb[...] = wg_ref[...].astype(BF16)
        wub[...] = wu_ref[...].astype(BF16)

    def body(hf_ref, act_ref):
        hf = hf_ref[...]
        a = _dot(hf, wgb[...])
        act_ref[...] = (a * _sigmoid(a) * _dot(hf, wub[...])).astype(BF16)

    @pl.when(m == 0)
    def _():
        body(hfs_ref, acts_ref)

    @pl.when(m > 0)
    def _():
        body(hfp_ref, actp_ref)


def _ffn_up(hf_p, hf_s, wg, wu, tm, tf):
    t_p, t_s = hf_p.shape[0], hf_s.shape[0]
    n_p = t_p // tm
    n_f = D_FF // tf
    prompt_tile = lambda m: jnp.maximum(m - 1, 0)
    return pl.pallas_call(
        _ffn_up_kernel,
        grid=(n_f, n_p + 1),
        in_specs=[
            pl.BlockSpec((tm, D_MODEL), lambda f, m: (prompt_tile(m), 0)),
            pl.BlockSpec((t_s, D_MODEL), lambda f, m: (0, 0)),
            pl.BlockSpec((D_MODEL, tf), lambda f, m: (0, f)),
            pl.BlockSpec((D_MODEL, tf), lambda f, m: (0, f)),
        ],
        out_specs=[
            pl.BlockSpec((tm, tf), lambda f, m: (prompt_tile(m), f)),
            pl.BlockSpec((t_s, tf), lambda f, m: (0, f)),
        ],
        out_shape=(jax.ShapeDtypeStruct((t_p, D_FF), BF16),
                   jax.ShapeDtypeStruct((t_s, D_FF), BF16)),
        scratch_shapes=[pltpu.VMEM((D_MODEL, tf), BF16), pltpu.VMEM((D_MODEL, tf), BF16)],
        compiler_params=pltpu.CompilerParams(
            dimension_semantics=("arbitrary", "arbitrary"), vmem_limit_bytes=VMEM_LIMIT),
        name="ffn_up",
    )(hf_p, hf_s, wg, wu)


def _ffn_down_kernel(act_ref, wd_ref, x1_ref, gp_ref, out_ref, *, n_sub):
    for rows in _row_blocks(act_ref.shape[0], n_sub):
        ff = _dot(act_ref[rows, :], wd_ref[...])
        out_ref[rows, :] = x1_ref[rows, :] + _rms(ff, gp_ref[...])


def _ffn_down(act, wd, x1, g_post, tm, n_sub):
    t = act.shape[0]
    tok = lambda i: (i, 0)
    return pl.pallas_call(
        functools.partial(_ffn_down_kernel, n_sub=n_sub),
        grid=(t // tm,),
        in_specs=[
            pl.BlockSpec((tm, D_FF), tok),
            _resident((D_FF, D_MODEL)),
            pl.BlockSpec((tm, D_MODEL), tok),
            _resident((1, D_MODEL)),
        ],
        out_specs=pl.BlockSpec((tm, D_MODEL), tok),
        out_shape=jax.ShapeDtypeStruct((t, D_MODEL), F32),
        compiler_params=pltpu.CompilerParams(
            dimension_semantics=("arbitrary",), vmem_limit_bytes=VMEM_LIMIT),
        name="ffn_down",
    )(act, wd, x1, g_post)


def _block_diag_in(bbt):
    x = bbt.reshape(B_GROUP, N_BLOCKS, GROUPS_PER_BLOCK, B_STATE)
    x = jnp.transpose(x, (1, 2, 0, 3))
    eye = jnp.eye(GROUPS_PER_BLOCK, dtype=bool)[None, :, None, :, None]
    y = jnp.where(eye, x[:, :, :, None, :], 0.0)
    return y.reshape(N_BLOCKS, LANES, STATE_PER_BLOCK)


def _block_diag_out(c):
    x = c.reshape(N_BLOCKS, GROUPS_PER_BLOCK, B_GROUP, B_STATE)
    x = jnp.transpose(x, (0, 1, 3, 2))
    eye = jnp.eye(GROUPS_PER_BLOCK, dtype=bool)[None, :, None, :, None]
    y = jnp.where(eye, x[:, :, :, None, :], 0.0)
    return y.reshape(N_BLOCKS, STATE_PER_BLOCK, LANES)


def _row_permutation(batch):
    n = batch * PERM_T
    src = np.arange(n)
    b, t = src // PERM_T, src % PERM_T
    p = np.zeros((n, n), np.float32)
    p[t * batch + b, src] = 1.0
    return jnp.asarray(p, BF16), jnp.asarray(p.T, BF16)


def _pad_rows_front(x, rows):
    pad = [(0, 0)] * x.ndim
    pad[-2] = (rows - x.shape[-2], 0)
    return jnp.pad(x, pad)


def _pad_rows_back(x, rows):
    pad = [(0, 0)] * x.ndim
    pad[-2] = (0, rows - x.shape[-2])
    return jnp.pad(x, pad)


def _mix_layer(x, conv0, c0, n0, m0, sre0, sim0, p, *, tm, n_sub, chunk, tb):
    b, s_len, _ = x.shape
    t = b * s_len
    x2d = x.reshape(t, D_MODEL)
    assert s_len % chunk == 0 or s_len < chunk
    seg = min(s_len, chunk)
    if "w_glu" in p:
        qk, v, og, u, gates, gates_c = _in_proj(x2d, p, tm=tm, n_sub=n_sub, seg=seg)
    else:
        qk, v, og, u, gates, gates_c, p["w_glu"], p["w_out"] = _in_proj(
            x2d, p, tm=tm, n_sub=n_sub, seg=seg, cast=(p["w_glu_f32"], p["w_out_f32"]))

    s_pad = -(-s_len // chunk) * chunk
    extra = s_pad - s_len
    def seq3(a, mode="constant"):
        a = a.reshape(b, s_len, a.shape[-1])
        return a if extra == 0 else jnp.pad(a, ((0, 0), (0, extra), (0, 0)), mode=mode)
    g3 = gates.reshape(2 * SUBLANES, b, s_len)
    if extra:
        pad_t = ((0, 0), (0, 0), (0, extra))
        g3 = jnp.concatenate([
            jnp.pad(g3[:SUBLANES], pad_t, constant_values=-jnp.inf),
            jnp.pad(g3[SUBLANES:], pad_t, mode="edge")], axis=0)
    out_a, conv_o, c_o, n_o, m_o = _mlstm(
        seq3(qk), seq3(v), seq3(og), g3.reshape(2 * SUBLANES, b * s_pad),
        seq3(gates_c, "edge"), _pad_rows_front(conv0, SUBLANES), c0, n0,
        jnp.broadcast_to(_pad_rows_back(m0[..., None], SUBLANES), (b, SUBLANES, LANES)),
        p["w_conv"], p["b_conv"], p["g_mh"], chunk=chunk, valid=seg)
    out_a = out_a[:, :s_len].reshape(t, D_A)

    out_b, sre_o, sim_o = _s5(
        u.reshape(b, s_len, D_B), sre0.reshape(b, S5_N), sim0.reshape(b, S5_N),
        p["lbre"], p["lbim"], p["wb"], p["wcre"], p["wcim"], p["s5_d"],
        p["perm"], p["permt"], p["w_glu"], p["b_glu"], tb=tb)

    cast = () if "w_down" in p else (p["w_down_f32"],)
    x1, hf, *cast_out = _out_proj(out_a, out_b.reshape(t, D_B), x2d, p["w_out"],
                                  p["g_post_mix"], p["g_pre_ffn"], tm, 2 * n_sub, cast=cast)
    if cast_out:
        p["w_down"] = cast_out[0]
    states = (
        conv_o[:, SUBLANES - (CONV_W - 1):, :],
        c_o,
        n_o,
        m_o[:, :A_HEADS, 0],
        sre_o.reshape(b, B_GROUPS, B_STATE),
        sim_o.reshape(b, B_GROUPS, B_STATE),
    )
    return x1, hf, states


def kernel(x_prompt, x_sample, state_conv, state_mlstm_c, state_mlstm_n, state_mlstm_m, state_s5_re, state_s5_im, g_pre_mix, w_in, w_conv, b_conv, b_igate, b_fgate, g_mh, s5_lam_re, s5_lam_im, s5_log_dt, s5_b_re, s5_b_im, s5_c_re, s5_c_im, s5_d, w_glu, b_glu, w_out, g_post_mix, g_pre_ffn, w_gate, w_up, w_down, g_post_ffn):
    l = 0
    bp = x_prompt.shape[0]
    lbre, lbim, bbt_re, bbt_im = _discretise(
        s5_lam_re[l], s5_lam_im[l], s5_log_dt[l],
        jnp.transpose(s5_b_re[l], (2, 0, 1)), jnp.transpose(s5_b_im[l], (2, 0, 1)))

    o1 = QK_W
    o2 = o1 + D_A
    o3 = o2 + D_A
    o4 = o3 + A_HEADS
    o5 = o4 + A_HEADS
    w = w_in[l]
    perm, permt = _row_permutation(bp)
    row = lambda a: a.reshape(1, -1)
    p = {
        "g_pre_mix": row(g_pre_mix[l]),
        "w_in": w[:, :o3 + LANES].astype(BF16),
        "w_in_u": w[:, o5:].astype(BF16),
        "gbias": jnp.concatenate(
            [b_igate[l], b_fgate[l], jnp.zeros((SUBLANES,), F32)]).reshape(2 * SUBLANES, 1),
        "w_conv": w_conv[l],
        "b_conv": row(b_conv[l]),
        "g_mh": row(g_mh[l]),
        "lbre": lbre.reshape(1, S5_N),
        "lbim": lbim.reshape(1, S5_N),
        "wb": jnp.concatenate([_block_diag_in(bbt_re), _block_diag_in(bbt_im)], axis=-1).astype(BF16),
        "wcre": _block_diag_out(s5_c_re[l]).astype(BF16),
        "wcim": _block_diag_out(s5_c_im[l]).astype(BF16),
        "s5_d": row(s5_d[l]),
        "perm": perm,
        "permt": permt,
        "w_glu_f32": w_glu[l],
        "b_glu": row(b_glu[l]),
        "w_out_f32": w_out[l],
        "w_down_f32": w_down[l],
        "g_post_mix": row(g_post_mix[l]),
        "g_pre_ffn": row(g_pre_ffn[l]),
    }

    x1_p, hf_p, st_p = _mix_layer(
        x_prompt,
        jnp.zeros((bp, CONV_W - 1, QK_W), F32),
        jnp.zeros((bp, A_HEADS, A_DV, A_DQK), F32),
        jnp.zeros((bp, A_HEADS, A_DQK), F32),
        jnp.zeros((bp, A_HEADS), F32),
        jnp.zeros((bp, B_GROUPS, B_STATE), F32),
        jnp.zeros((bp, B_GROUPS, B_STATE), F32),
        p, tm=512, n_sub=2, chunk=128, tb=64)
    x1_s, hf_s, st_s = _mix_layer(
        x_sample, state_conv[l], state_mlstm_c[l], state_mlstm_n[l], state_mlstm_m[l],
        state_s5_re[l], state_s5_im[l],
        p, tm=128, n_sub=1, chunk=128, tb=16)

    act_p, act_s = _ffn_up(hf_p, hf_s, w_gate[l], w_up[l], tm=1024, tf=512)
    wd = p["w_down"]
    g_post = row(g_post_ffn[l])
    yp = _ffn_down(act_p, wd, x1_p, g_post, tm=512, n_sub=2).reshape(x_prompt.shape)
    ys = _ffn_down(act_s, wd, x1_s, g_post, tm=128, n_sub=1).reshape(x_sample.shape)
    return (yp, ys) + tuple(a[None] for a in st_p) + tuple(a[None] for a in st_s)
```

```python
import functools

import numpy as np
import jax
import jax.numpy as jnp
from jax import lax
from jax.experimental import pallas as pl
from jax.experimental.pallas import tpu as pltpu

F32 = jnp.float32
BF16 = jnp.bfloat16

D_MODEL = 2048
D_A = 1024
D_B = 1024
A_HEADS = 4
A_DV = 256
A_DQK = 128
QK_W = 1024
CONV_W = 4
B_GROUP = 16
B_GROUPS = 64
B_STATE = 64
S5_N = B_GROUPS * B_STATE
D_FF = 5632
EPS = 1e-6

SUBLANES = 8
LANES = 128
GROUPS_PER_BLOCK = LANES // B_GROUP
N_BLOCKS = B_GROUPS // GROUPS_PER_BLOCK
STATE_PER_BLOCK = GROUPS_PER_BLOCK * B_STATE
PERM_T = 16
VMEM_LIMIT = 56 * 1024 * 1024


def _sigmoid(x):
    return 1.0 / (1.0 + jnp.exp(-x))


def _gelu_tanh(x):
    c = np.sqrt(2 / np.pi).astype(np.float32)
    return x * (0.5 * (1.0 + jnp.tanh(c * (x + 0.044715 * (x ** 3)))))


def _rms(x, g):
    return x * lax.rsqrt(jnp.mean(x * x, axis=-1, keepdims=True) + EPS) * g


def _dot(a, b):
    return jnp.dot(a, b, preferred_element_type=F32)


def _log_sigmoid(z):
    return jnp.minimum(z, 0.0) - jnp.log1p(jnp.exp(-jnp.abs(z)))


def _split3(x):
    hi = x.astype(BF16)
    r = x - hi.astype(F32)
    mid = r.astype(BF16)
    lo = (r - mid.astype(F32)).astype(BF16)
    return hi, mid, lo


def _dot3_right(x, sel):
    hi, mid, lo = _split3(x)
    return _dot(hi, sel) + _dot(mid, sel) + _dot(lo, sel)


def _dot3_left(sel, x):
    hi, mid, lo = _split3(x)
    return _dot(sel, hi) + _dot(sel, mid) + _dot(sel, lo)


def _row_blocks(rows, n_sub):
    assert rows % n_sub == 0
    size = rows // n_sub
    return [slice(size * r, size * (r + 1)) for r in range(n_sub)]


def _resident(shape):
    zeros = (0,) * len(shape)
    return pl.BlockSpec(shape, lambda *_: zeros, pipeline_mode=pl.Buffered(1))


def _disc_kernel(lam_re_ref, lam_im_ref, logdt_ref, bre_ref, bim_ref,
                 lbre_ref, lbim_ref, bbre_ref, bbim_ref):
    lam_re = lam_re_ref[...]
    lam_im = lam_im_ref[...]
    dt = jnp.exp(logdt_ref[...])
    mag = jnp.exp(lam_re * dt)
    ang = lam_im * dt
    lb_re = mag * jnp.cos(ang)
    lb_im = mag * jnp.sin(ang)
    den = lam_re * lam_re + lam_im * lam_im
    nr = lb_re - 1.0
    f_re = (nr * lam_re + lb_im * lam_im) / den
    f_im = (lb_im * lam_re - nr * lam_im) / den
    lbre_ref[...] = lb_re
    lbim_ref[...] = lb_im
    b_re = bre_ref[...]
    b_im = bim_ref[...]
    bbre_ref[...] = f_re[None] * b_re - f_im[None] * b_im
    bbim_ref[...] = f_re[None] * b_im + f_im[None] * b_re


def _discretise(lam_re, lam_im, log_dt, bt_re, bt_im):
    gp = jax.ShapeDtypeStruct((B_GROUPS, B_STATE), F32)
    igp = jax.ShapeDtypeStruct((B_GROUP, B_GROUPS, B_STATE), F32)
    return pl.pallas_call(
        _disc_kernel, out_shape=(gp, gp, igp, igp), name="s5_discretise",
    )(lam_re, lam_im, log_dt.reshape(B_GROUPS, 1), bt_re, bt_im)


GATE_COL = QK_W + 2 * D_A
GATE_LANE_CMX = A_HEADS
GATE_LANE_FC = 2 * A_HEADS


def _inproj_kernel(x_ref, gpre_ref, w_ref, wtail_ref, gb_ref, triu_ref, tril_ref, *rest,
                   n_sub, seg, n_cast):
    cast_in = rest[:n_cast]
    qk_out, v_out, og_out, u_out, g_out, c_out = rest[n_cast:n_cast + 6]
    cast_out = rest[n_cast + 6:2 * n_cast + 6]
    wgt, wu = rest[2 * n_cast + 6:2 * n_cast + 8]
    for src, dst in zip(cast_in, cast_out):
        dst[...] = src[...].astype(BF16)

    @pl.when(pl.program_id(0) == 0)
    def _():
        wgt[...] = wtail_ref[:, 0:LANES].T[0:2 * SUBLANES, :].astype(BF16)
        wu[...] = wtail_ref[:, 2 * A_HEADS:2 * A_HEADS + D_B].astype(BF16)

    for rows in _row_blocks(x_ref.shape[0], n_sub):
        hm = _rms(x_ref[rows, :], gpre_ref[...]).astype(BF16)

        n = rows.stop - rows.start
        z = lax.dot_general(wgt[...], hm, (((1,), (1,)), ((), ())),
                            preferred_element_type=F32) + gb_ref[...]
        qk_out[rows, :] = _dot(hm, w_ref[:, 0:QK_W])

        row = lax.broadcasted_iota(jnp.int32, z.shape, 0)
        zf = jnp.where((row >= A_HEADS) & (row < 2 * A_HEADS), _log_sigmoid(z), z)
        zf8 = zf[0:SUBLANES]
        zc = jnp.concatenate([zf8, jnp.zeros((LANES - SUBLANES, n), F32)], axis=0).T
        lane = lax.broadcasted_iota(jnp.int32, zc.shape, 1)
        lf_c = jnp.where((lane >= A_HEADS) & (lane < 2 * A_HEADS), zc, 0.0)
        fc = pltpu.roll(_dot3_right(zf, triu_ref[...])[0:SUBLANES], A_HEADS, axis=0)
        fc_c = _dot3_left(tril_ref[...], lf_c)
        v_out[rows, :] = _dot(hm, w_ref[:, QK_W:QK_W + D_A]).astype(BF16)
        og_out[rows, :] = _dot(hm, w_ref[:, QK_W + D_A:QK_W + 2 * D_A])
        u_out[rows, :] = _dot(hm, wu[...])

        g_out[0:SUBLANES, rows] = zf8 - fc
        g_out[SUBLANES:2 * SUBLANES, rows] = fc
        cm = pltpu.roll(zc, A_HEADS, axis=1) - fc_c
        t_in = lax.broadcasted_iota(jnp.int32, zc.shape, 0) & (seg - 1)
        sh = 1
        while sh < seg:
            cm = jnp.where(t_in >= sh, jnp.maximum(cm, pltpu.roll(cm, sh, axis=0)), cm)
            sh *= 2
        c_out[rows, :] = jnp.where(lane < GATE_LANE_FC, cm, pltpu.roll(fc_c, A_HEADS, axis=1))


def _segment_prefix_matrices(n, seg):
    idx = np.arange(n)
    u = ((idx[:, None] // seg == idx[None, :] // seg) & (idx[:, None] <= idx[None, :]))
    u = u.astype(np.float32)
    return jnp.asarray(u, BF16), jnp.asarray(u.T, BF16)


def _in_proj(x2d, p, *, tm, n_sub, seg, cast=()):
    t = x2d.shape[0]
    n_steps = t // tm
    assert seg & (seg - 1) == 0
    triu, tril = _segment_prefix_matrices(tm // n_sub, seg)
    tok = lambda i: (i, 0)
    cast_specs = [pl.BlockSpec((w.shape[0] // n_steps, w.shape[1]), tok) for w in cast]
    return pl.pallas_call(
        functools.partial(_inproj_kernel, n_sub=n_sub, seg=seg, n_cast=len(cast)),
        grid=(n_steps,),
        in_specs=[
            pl.BlockSpec((tm, D_MODEL), tok),
            _resident((1, D_MODEL)),
            _resident((D_MODEL, GATE_COL)),
            _resident((D_MODEL, 2 * A_HEADS + D_B)),
            _resident((2 * SUBLANES, 1)),
            _resident(triu.shape),
            _resident(tril.shape),
        ] + cast_specs,
        out_specs=[
            pl.BlockSpec((tm, QK_W), tok),
            pl.BlockSpec((tm, D_A), tok),
            pl.BlockSpec((tm, D_A), tok),
            pl.BlockSpec((tm, D_B), tok),
            pl.BlockSpec((2 * SUBLANES, tm), lambda i: (0, i)),
            pl.BlockSpec((tm, LANES), tok),
        ] + cast_specs,
        out_shape=(
            jax.ShapeDtypeStruct((t, QK_W), F32),
            jax.ShapeDtypeStruct((t, D_A), BF16),
            jax.ShapeDtypeStruct((t, D_A), F32),
            jax.ShapeDtypeStruct((t, D_B), F32),
            jax.ShapeDtypeStruct((2 * SUBLANES, t), F32),
            jax.ShapeDtypeStruct((t, LANES), F32),
        ) + tuple(jax.ShapeDtypeStruct(w.shape, BF16) for w in cast),
        scratch_shapes=[pltpu.VMEM((2 * SUBLANES, D_MODEL), BF16),
                        pltpu.VMEM((D_MODEL, D_B), BF16)],
        compiler_params=pltpu.CompilerParams(
            dimension_semantics=("arbitrary",), vmem_limit_bytes=VMEM_LIMIT),
        name="in_proj",
    )(x2d, p["g_pre_mix"], p["w_in"], p["w_in_tail"], p["gbias"], triu, tril, *cast)


def _mlstm_kernel(qk_ref, v_ref, og_ref, g_ref, gc_ref, rep_ref, tail0_ref, c0_ref, n0_ref, m0_ref,
                  wconv_ref, bconv_ref, gmh_ref,
                  outa_ref, convo_ref, co_ref, no_ref, mo_ref,
                  xp, ct, ncr, m_s, *, chunk, valid, n_chunks):
    L = chunk
    c = pl.program_id(1)
    reps = L // LANES

    @pl.when(c == 0)
    def _():
        xp[0:SUBLANES, :] = tail0_ref[0]
        for h in range(A_HEADS):
            ct[h] = c0_ref[0, h].T
            ncr[h] = jnp.broadcast_to(n0_ref[0, h:h + 1, :], (A_DQK, A_DQK)).T
        m_s[...] = m0_ref[0]

    def lanes(x, n):
        return x if n == 1 else jnp.concatenate([x] * n, axis=1)

    xp[SUBLANES:SUBLANES + L, :] = qk_ref[0]
    xs = xp[...]
    acc = bconv_ref[...]
    for j in range(CONV_W):
        back = CONV_W - 1 - j
        tap = xs if back == 0 else pltpu.roll(xs, back, axis=0)
        acc = acc + tap[SUBLANES:, :] * wconv_ref[j:j + 1, :]
    qk = acc * _sigmoid(acc)
    q_all = (qk[:, :QK_W // 2] * (A_DQK ** -0.5)).astype(BF16)
    k_all = qk[:, QK_W // 2:].astype(BF16)

    @pl.when(c == n_chunks - 1)
    def _():
        convo_ref[0] = xp[valid:valid + SUBLANES, :]

    xp[0:SUBLANES, :] = xp[L:L + SUBLANES, :]

    rep = _dot3_right(gc_ref[0], rep_ref[...])

    g = g_ref[...]
    gr = g[0:SUBLANES]
    fc = g[SUBLANES:2 * SUBLANES]
    lane = lax.broadcasted_iota(jnp.int32, (SUBLANES, L), 1)
    f_last = jnp.sum(jnp.where(lane == L - 1, fc, 0.0), axis=1, keepdims=True)
    m0 = jnp.max(m_s[...], axis=1, keepdims=True)
    lw = f_last + gr
    m_new = jnp.maximum(f_last + m0, jnp.max(lw, axis=1, keepdims=True))
    ws = jnp.exp(lw - m_new)
    decay = jnp.broadcast_to(jnp.exp(f_last + m0 - m_new), (SUBLANES, LANES))

    row_i = lax.broadcasted_iota(jnp.int32, (L, L), 0)
    col_i = lax.broadcasted_iota(jnp.int32, (L, L), 1)
    tri = col_i <= row_i
    for h in range(A_HEADS):
        qb = q_all[:, A_DQK * h:A_DQK * (h + 1)]
        kb = k_all[:, A_DQK * h:A_DQK * (h + 1)]
        vb = v_ref[0, :, A_DV * h:A_DV * (h + 1)]
        m0_h = m_s[h:h + 1, :]
        mx = jnp.maximum(rep[:, LANES * h:LANES * (h + 1)], m0_h)
        fc_t = rep[:, LANES * (A_HEADS + h):LANES * (A_HEADS + h + 1)]
        e = jnp.exp(jnp.where(tri, gr[h:h + 1, :] - lanes(mx, reps), -jnp.inf))
        s = lax.dot_general(qb, kb, (((1,), (1,)), ((), ())),
                            preferred_element_type=F32) * e
        g_int = jnp.exp(m0_h - mx)
        ct_h = ct[h]
        ncr_h = ncr[h]
        num = (_dot(s.astype(BF16), vb)
               + lanes(g_int, A_DV // LANES) * _dot(qb, ct_h.astype(BF16)))
        den = jnp.sum(s, axis=1, keepdims=True) + g_int * _dot(qb, ncr_h.astype(BF16))
        scale = jnp.maximum(jnp.abs(den), jnp.exp(-(fc_t + mx)))
        hh = num / lanes(scale, A_DV // LANES)
        hn = _rms(hh, gmh_ref[:, A_DV * h:A_DV * (h + 1)])
        og = og_ref[0, :, A_DV * h:A_DV * (h + 1)]
        outa_ref[0, :, A_DV * h:A_DV * (h + 1)] = (hn * _sigmoid(og)).astype(BF16)
        kws = kb.astype(F32).T * ws[h:h + 1, :]
        dec_h = decay[h:h + 1, :]
        ct[h] = lanes(dec_h, A_DV // LANES) * ct_h + _dot(kws.astype(BF16), vb)
        ncr[h] = dec_h * ncr_h + jnp.sum(kws, axis=1, keepdims=True)
    m_s[...] = jnp.broadcast_to(m_new, m_s.shape)

    @pl.when(c == n_chunks - 1)
    def _():
        for h in range(A_HEADS):
            co_ref[0, h] = ct[h].T
            no_ref[0, h:h + 1, :] = ncr[h].T[0:1, :]
        mo_ref[0] = m_s[...]


def _gate_replication_matrix():
    r = np.zeros((LANES, 2 * A_HEADS * LANES), np.float32)
    for h in range(A_HEADS):
        r[GATE_LANE_CMX + h, LANES * h:LANES * (h + 1)] = 1.0
        r[GATE_LANE_FC + h, LANES * (A_HEADS + h):LANES * (A_HEADS + h + 1)] = 1.0
    return jnp.asarray(r, BF16)


def _mlstm(qk, v, og, gates, gates_c, tail0, c0, n0, m0, wconv, bconv, gmh, *, chunk, valid):
    b, s_len, _ = qk.shape
    n_chunks = s_len // chunk
    rep = _gate_replication_matrix()
    seq = lambda i, c: (i, c, 0)
    per_b3 = lambda i, c: (i, 0, 0)
    per_b4 = lambda i, c: (i, 0, 0, 0)
    const2 = lambda i, c: (0, 0)
    kernel = functools.partial(_mlstm_kernel, chunk=chunk, valid=valid, n_chunks=n_chunks)
    return pl.pallas_call(
        kernel,
        grid=(b, n_chunks),
        in_specs=[
            pl.BlockSpec((1, chunk, QK_W), seq),
            pl.BlockSpec((1, chunk, D_A), seq),
            pl.BlockSpec((1, chunk, D_A), seq),
            pl.BlockSpec((2 * SUBLANES, chunk), lambda i, c: (0, i * n_chunks + c)),
            pl.BlockSpec((1, chunk, LANES), seq),
            pl.BlockSpec(rep.shape, const2),
            pl.BlockSpec((1, SUBLANES, QK_W), per_b3),
            pl.BlockSpec((1, A_HEADS, A_DV, A_DQK), per_b4),
            pl.BlockSpec((1, A_HEADS, A_DQK), per_b3),
            pl.BlockSpec((1, SUBLANES, LANES), per_b3),
            pl.BlockSpec((CONV_W, QK_W), const2),
            pl.BlockSpec((1, QK_W), const2),
            pl.BlockSpec((1, D_A), const2),
        ],
        out_specs=[
            pl.BlockSpec((1, chunk, D_A), seq),
            pl.BlockSpec((1, SUBLANES, QK_W), per_b3),
            pl.BlockSpec((1, A_HEADS, A_DV, A_DQK), per_b4),
            pl.BlockSpec((1, A_HEADS, A_DQK), per_b3),
            pl.BlockSpec((1, SUBLANES, LANES), per_b3),
        ],
        out_shape=(
            jax.ShapeDtypeStruct((b, s_len, D_A), BF16),
            jax.ShapeDtypeStruct((b, SUBLANES, QK_W), F32),
            jax.ShapeDtypeStruct((b, A_HEADS, A_DV, A_DQK), F32),
            jax.ShapeDtypeStruct((b, A_HEADS, A_DQK), F32),
            jax.ShapeDtypeStruct((b, SUBLANES, LANES), F32),
        ),
        scratch_shapes=[
            pltpu.VMEM((chunk + SUBLANES, QK_W), F32),
            pltpu.VMEM((A_HEADS, A_DQK, A_DV), F32),
            pltpu.VMEM((A_HEADS, A_DQK, LANES), F32),
            pltpu.VMEM((SUBLANES, LANES), F32),
        ],
        compiler_params=pltpu.CompilerParams(
            dimension_semantics=("arbitrary", "arbitrary"), vmem_limit_bytes=VMEM_LIMIT),
        name="mlstm",
    )(qk, v, og, gates, gates_c, rep, tail0, c0, n0, m0, wconv, bconv, gmh)


def _s5_kernel(u_ref, s0re_ref, s0im_ref, lbre_ref, lbim_ref, wb_ref, wcre_ref, wcim_ref,
               d_ref, perm_ref, permt_ref, wglu_ref, bglu_ref,
               outb_ref, sre_out, sim_out,
               bure, buim, sre, sim, utb, xtb, ytb, *, tb, n_steps):
    step = pl.program_id(0)
    batch = u_ref.shape[0]
    rows_j = batch * PERM_T
    n_j = tb // PERM_T

    @pl.when(step == 0)
    def _():
        sre[...] = s0re_ref[...]
        sim[...] = s0im_ref[...]

    strip = 8 * LANES
    blocks_per_strip = strip // STATE_PER_BLOCK
    n_strips = S5_N // strip

    def permute_in():
        for j in range(n_j):
            xj = jnp.concatenate(
                [u_ref[b, PERM_T * j:PERM_T * (j + 1), :] for b in range(batch)], axis=0)
            hi = xj.astype(BF16)
            lo = (xj - hi.astype(F32)).astype(BF16)
            uh = _dot(perm_ref[...], hi)
            rows = slice(rows_j * j, rows_j * (j + 1))
            xtb[rows, :] = uh.astype(BF16)
            utb[rows, :] = uh + _dot(perm_ref[...], lo)

    def input_map(st):
        for k in range(blocks_per_strip * st, blocks_per_strip * (st + 1)):
            r = _dot(xtb[:, LANES * k:LANES * (k + 1)], wb_ref[k])
            kc = slice(STATE_PER_BLOCK * k, STATE_PER_BLOCK * (k + 1))
            bure[:, kc] = r[:, :STATE_PER_BLOCK]
            buim[:, kc] = r[:, STATE_PER_BLOCK:]

    def recurrence(st):
        cols = slice(strip * st, strip * (st + 1))
        a_re = jnp.broadcast_to(lbre_ref[:, cols], (batch, strip))
        a_im = jnp.broadcast_to(lbim_ref[:, cols], (batch, strip))
        s_r = sre[:, cols]
        s_i = sim[:, cols]
        for t in range(tb):
            tr = slice(t * batch, (t + 1) * batch)
            n_r = a_re * s_r - a_im * s_i + bure[tr, cols]
            n_i = a_re * s_i + a_im * s_r + buim[tr, cols]
            bure[tr, cols] = n_r
            buim[tr, cols] = n_i
            s_r, s_i = n_r, n_i
        sre[:, cols] = s_r
        sim[:, cols] = s_i

    def output_map(st):
        for k in range(blocks_per_strip * st, blocks_per_strip * (st + 1)):
            kc = slice(STATE_PER_BLOCK * k, STATE_PER_BLOCK * (k + 1))
            ytb[:, LANES * k:LANES * (k + 1)] = (
                _dot(bure[:, kc].astype(BF16), wcre_ref[k])
                - _dot(buim[:, kc].astype(BF16), wcim_ref[k]))

    permute_in()
    for st in range(n_strips):
        input_map(st)
        recurrence(st)
        output_map(st)

    z = _gelu_tanh(ytb[...] + d_ref[...] * utb[...])
    gate = _sigmoid(_dot(z.astype(BF16), wglu_ref[...]) + bglu_ref[...])
    res = (z * gate).astype(BF16)
    for j in range(n_j):
        rb = _dot(permt_ref[...], res[rows_j * j:rows_j * (j + 1), :]).astype(BF16)
        for b in range(batch):
            outb_ref[b, PERM_T * j:PERM_T * (j + 1), :] = rb[PERM_T * b:PERM_T * (b + 1), :]

    @pl.when(step == n_steps - 1)
    def _():
        sre_out[...] = sre[...]
        sim_out[...] = sim[...]


def _s5(u, s0re, s0im, lbre, lbim, wb, wcre, wcim, d_skip, perm, permt, wglu, bglu, *, tb):
    b, s_len, _ = u.shape
    n_steps = s_len // tb
    rows = b * tb
    kernel = functools.partial(_s5_kernel, tb=tb, n_steps=n_steps)
    state = jax.ShapeDtypeStruct((b, S5_N), F32)
    return pl.pallas_call(
        kernel,
        grid=(n_steps,),
        in_specs=[
            pl.BlockSpec((b, tb, D_B), lambda i: (0, i, 0)),
            _resident((b, S5_N)),
            _resident((b, S5_N)),
            _resident((1, S5_N)),
            _resident((1, S5_N)),
            _resident((N_BLOCKS, LANES, 2 * STATE_PER_BLOCK)),
            _resident((N_BLOCKS, STATE_PER_BLOCK, LANES)),
            _resident((N_BLOCKS, STATE_PER_BLOCK, LANES)),
            _resident((1, D_B)),
            _resident((b * PERM_T, b * PERM_T)),
            _resident((b * PERM_T, b * PERM_T)),
            _resident((D_B, D_B)),
            _resident((1, D_B)),
        ],
        out_specs=[
            pl.BlockSpec((b, tb, D_B), lambda i: (0, i, 0)),
            pl.BlockSpec((b, S5_N), lambda i: (0, 0)),
            pl.BlockSpec((b, S5_N), lambda i: (0, 0)),
        ],
        out_shape=(jax.ShapeDtypeStruct((b, s_len, D_B), BF16), state, state),
        scratch_shapes=[
            pltpu.VMEM((rows, S5_N), F32),
            pltpu.VMEM((rows, S5_N), F32),
            pltpu.VMEM((b, S5_N), F32),
            pltpu.VMEM((b, S5_N), F32),
            pltpu.VMEM((rows, D_B), F32),
            pltpu.VMEM((rows, D_B), BF16),
            pltpu.VMEM((rows, D_B), F32),
        ],
        compiler_params=pltpu.CompilerParams(
            dimension_semantics=("arbitrary",), vmem_limit_bytes=VMEM_LIMIT),
        name="s5",
    )(u, s0re, s0im, lbre, lbim, wb, wcre, wcim, d_skip, perm, permt, wglu, bglu)


def _outproj_kernel(a_ref, b_ref, x_ref, wo_ref, gpost_ref, gffn_ref, *rest, n_sub, n_cast):
    cast_in = rest[:n_cast]
    x1_ref, hf_ref = rest[n_cast:n_cast + 2]
    cast_out = rest[n_cast + 2:]
    for src, dst in zip(cast_in, cast_out):
        dst[...] = src[...].astype(BF16)
    for rows in _row_blocks(a_ref.shape[0], n_sub):
        mix = (_dot(a_ref[rows, :], wo_ref[0:D_A, :])
               + _dot(b_ref[rows, :], wo_ref[D_A:D_A + D_B, :]))
        x1 = x_ref[rows, :] + _rms(mix, gpost_ref[...])
        x1_ref[rows, :] = x1
        hf_ref[rows, :] = _rms(x1, gffn_ref[...]).astype(BF16)


def _out_proj(a, b, x2d, wo, g_post, g_ffn, tm, n_sub, cast=()):
    t = x2d.shape[0]
    n_steps = t // tm
    tok = lambda i: (i, 0)
    cast_specs = [pl.BlockSpec((w.shape[0] // n_steps, w.shape[1]), tok) for w in cast]
    return pl.pallas_call(
        functools.partial(_outproj_kernel, n_sub=n_sub, n_cast=len(cast)),
        grid=(n_steps,),
        in_specs=[
            pl.BlockSpec((tm, D_A), tok),
            pl.BlockSpec((tm, D_B), tok),
            pl.BlockSpec((tm, D_MODEL), tok),
            _resident((D_A + D_B, D_MODEL)),
            _resident((1, D_MODEL)),
            _resident((1, D_MODEL)),
        ] + cast_specs,
        out_specs=[pl.BlockSpec((tm, D_MODEL), tok), pl.BlockSpec((tm, D_MODEL), tok)] + cast_specs,
        out_shape=(jax.ShapeDtypeStruct((t, D_MODEL), F32),
                   jax.ShapeDtypeStruct((t, D_MODEL), BF16))
        + tuple(jax.ShapeDtypeStruct(w.shape, BF16) for w in cast),
        compiler_params=pltpu.CompilerParams(
            dimension_semantics=("arbitrary",), vmem_limit_bytes=VMEM_LIMIT),
        name="out_proj",
    )(a, b, x2d, wo, g_post, g_ffn, *cast)


def _ffn_up_kernel(hfp_ref, hfs_ref, wg_ref, wu_ref, actp_ref, acts_ref, wgb, wub):
    m = pl.program_id(1)

    @pl.when(m == 0)
    def _():
        wgb[...] = wg_ref[...].astype(BF16)
        wub[...] = wu_ref[...].astype(BF16)

    def body(hf_ref, act_ref):
        hf = hf_ref[...]
        a = _dot(hf, wgb[...])
        act_ref[...] = (a * _sigmoid(a) * _dot(hf, wub[...])).astype(BF16)

    @pl.when(m == 0)
    def _():
        body(hfs_ref, acts_ref)

    @pl.when(m > 0)
    def _():
        body(hfp_ref, actp_ref)


def _ffn_up(hf_p, hf_s, wg, wu, tm, tf):
    t_p, t_s = hf_p.shape[0], hf_s.shape[0]
    n_p = t_p // tm
    n_f = D_FF // tf
    prompt_tile = lambda m: jnp.maximum(m - 1, 0)
    return pl.pallas_call(
        _ffn_up_kernel,
        grid=(n_f, n_p + 1),
        in_specs=[
            pl.BlockSpec((tm, D_MODEL), lambda f, m: (prompt_tile(m), 0)),
            pl.BlockSpec((t_s, D_MODEL), lambda f, m: (0, 0)),
            pl.BlockSpec((D_MODEL, tf), lambda f, m: (0, f)),
            pl.BlockSpec((D_MODEL, tf), lambda f, m: (0, f)),
        ],
        out_specs=[
            pl.BlockSpec((tm, tf), lambda f, m: (prompt_tile(m), f)),
            pl.BlockSpec((t_s, tf), lambda f, m: (0, f)),
        ],
        out_shape=(jax.ShapeDtypeStruct((t_p, D_FF), BF16),
                   jax.ShapeDtypeStruct((t_s, D_FF), BF16)),
        scratch_shapes=[pltpu.VMEM((D_MODEL, tf), BF16), pltpu.VMEM((D_MODEL, tf), BF16)],
        compiler_params=pltpu.CompilerParams(
            dimension_semantics=("arbitrary", "arbitrary"), vmem_limit_bytes=VMEM_LIMIT),
        name="ffn_up",
    )(hf_p, hf_s, wg, wu)


def _ffn_down_kernel(act_ref, wd_ref, x1_ref, gp_ref, out_ref, *, n_sub):
    for rows in _row_blocks(act_ref.shape[0], n_sub):
        ff = _dot(act_ref[rows, :], wd_ref[...])
        out_ref[rows, :] = x1_ref[rows, :] + _rms(ff, gp_ref[...])


def _ffn_down(act, wd, x1, g_post, tm, n_sub):
    t = act.shape[0]
    tok = lambda i: (i, 0)
    return pl.pallas_call(
        functools.partial(_ffn_down_kernel, n_sub=n_sub),
        grid=(t // tm,),
        in_specs=[
            pl.BlockSpec((tm, D_FF), tok),
            _resident((D_FF, D_MODEL)),
            pl.BlockSpec((tm, D_MODEL), tok),
            _resident((1, D_MODEL)),
        ],
        out_specs=pl.BlockSpec((tm, D_MODEL), tok),
        out_shape=jax.ShapeDtypeStruct((t, D_MODEL), F32),
        compiler_params=pltpu.CompilerParams(
            dimension_semantics=("arbitrary",), vmem_limit_bytes=VMEM_LIMIT),
        name="ffn_down",
    )(act, wd, x1, g_post)


def _block_diag_in(bbt):
    x = bbt.reshape(B_GROUP, N_BLOCKS, GROUPS_PER_BLOCK, B_STATE)
    x = jnp.transpose(x, (1, 2, 0, 3))
    eye = jnp.eye(GROUPS_PER_BLOCK, dtype=bool)[None, :, None, :, None]
    y = jnp.where(eye, x[:, :, :, None, :], 0.0)
    return y.reshape(N_BLOCKS, LANES, STATE_PER_BLOCK)


def _block_diag_out(c):
    x = c.reshape(N_BLOCKS, GROUPS_PER_BLOCK, B_GROUP, B_STATE)
    x = jnp.transpose(x, (0, 1, 3, 2))
    eye = jnp.eye(GROUPS_PER_BLOCK, dtype=bool)[None, :, None, :, None]
    y = jnp.where(eye, x[:, :, :, None, :], 0.0)
    return y.reshape(N_BLOCKS, STATE_PER_BLOCK, LANES)


def _row_permutation(batch):
    n = batch * PERM_T
    src = np.arange(n)
    b, t = src // PERM_T, src % PERM_T
    p = np.zeros((n, n), np.float32)
    p[t * batch + b, src] = 1.0
    return jnp.asarray(p, BF16), jnp.asarray(p.T, BF16)


def _pad_rows_front(x, rows):
    pad = [(0, 0)] * x.ndim
    pad[-2] = (rows - x.shape[-2], 0)
    return jnp.pad(x, pad)


def _pad_rows_back(x, rows):
    pad = [(0, 0)] * x.ndim
    pad[-2] = (0, rows - x.shape[-2])
    return jnp.pad(x, pad)


def _mix_layer(x, conv0, c0, n0, m0, sre0, sim0, p, *, tm, n_sub, chunk, tb):
    b, s_len, _ = x.shape
    t = b * s_len
    x2d = x.reshape(t, D_MODEL)
    assert s_len % chunk == 0 or s_len < chunk
    seg = min(s_len, chunk)
    if "w_glu" in p:
        qk, v, og, u, gates, gates_c = _in_proj(x2d, p, tm=tm, n_sub=n_sub, seg=seg)
    else:
        qk, v, og, u, gates, gates_c, p["w_glu"], p["w_out"] = _in_proj(
            x2d, p, tm=tm, n_sub=n_sub, seg=seg, cast=(p["w_glu_f32"], p["w_out_f32"]))

    s_pad = -(-s_len // chunk) * chunk
    extra = s_pad - s_len
    def seq3(a, mode="constant"):
        a = a.reshape(b, s_len, a.shape[-1])
        return a if extra == 0 else jnp.pad(a, ((0, 0), (0, extra), (0, 0)), mode=mode)
    g3 = gates.reshape(2 * SUBLANES, b, s_len)
    if extra:
        pad_t = ((0, 0), (0, 0), (0, extra))
        g3 = jnp.concatenate([
            jnp.pad(g3[:SUBLANES], pad_t, constant_values=-jnp.inf),
            jnp.pad(g3[SUBLANES:], pad_t, mode="edge")], axis=0)
    out_a, conv_o, c_o, n_o, m_o = _mlstm(
        seq3(qk), seq3(v), seq3(og), g3.reshape(2 * SUBLANES, b * s_pad),
        seq3(gates_c, "edge"), _pad_rows_front(conv0, SUBLANES), c0, n0,
        jnp.broadcast_to(_pad_rows_back(m0[..., None], SUBLANES), (b, SUBLANES, LANES)),
        p["w_conv"], p["b_conv"], p["g_mh"], chunk=chunk, valid=seg)
    out_a = out_a[:, :s_len].reshape(t, D_A)

    out_b, sre_o, sim_o = _s5(
        u.reshape(b, s_len, D_B), sre0.reshape(b, S5_N), sim0.reshape(b, S5_N),
        p["lbre"], p["lbim"], p["wb"], p["wcre"], p["wcim"], p["s5_d"],
        p["perm"], p["permt"], p["w_glu"], p["b_glu"], tb=tb)

    cast = () if "w_down" in p else (p["w_down_f32"],)
    x1, hf, *cast_out = _out_proj(out_a, out_b.reshape(t, D_B), x2d, p["w_out"],
                                  p["g_post_mix"], p["g_pre_ffn"], tm, 2 * n_sub, cast=cast)
    if cast_out:
        p["w_down"] = cast_out[0]
    states = (
        conv_o[:, SUBLANES - (CONV_W - 1):, :],
        c_o,
        n_o,
        m_o[:, :A_HEADS, 0],
        sre_o.reshape(b, B_GROUPS, B_STATE),
        sim_o.reshape(b, B_GROUPS, B_STATE),
    )
    return x1, hf, states


def kernel(x_prompt, x_sample, state_conv, state_mlstm_c, state_mlstm_n, state_mlstm_m, state_s5_re, state_s5_im, g_pre_mix, w_in, w_conv, b_conv, b_igate, b_fgate, g_mh, s5_lam_re, s5_lam_im, s5_log_dt, s5_b_re, s5_b_im, s5_c_re, s5_c_im, s5_d, w_glu, b_glu, w_out, g_post_mix, g_pre_ffn, w_gate, w_up, w_down, g_post_ffn):
    l = 0
    bp = x_prompt.shape[0]
    lbre, lbim, bbt_re, bbt_im = _discretise(
        s5_lam_re[l], s5_lam_im[l], s5_log_dt[l],
        jnp.transpose(s5_b_re[l], (2, 0, 1)), jnp.transpose(s5_b_im[l], (2, 0, 1)))

    o1 = QK_W
    o2 = o1 + D_A
    o3 = o2 + D_A
    o4 = o3 + A_HEADS
    o5 = o4 + A_HEADS
    w = w_in[l]
    perm, permt = _row_permutation(bp)
    row = lambda a: a.reshape(1, -1)
    p = {
        "g_pre_mix": row(g_pre_mix[l]),
        "w_in": w[:, :o3].astype(BF16),
        "w_in_tail": w[:, o3:],
        "gbias": jnp.concatenate(
            [b_igate[l], b_fgate[l], jnp.zeros((SUBLANES,), F32)]).reshape(2 * SUBLANES, 1),
        "w_conv": w_conv[l],
        "b_conv": row(b_conv[l]),
        "g_mh": row(g_mh[l]),
        "lbre": lbre.reshape(1, S5_N),
        "lbim": lbim.reshape(1, S5_N),
        "wb": jnp.concatenate([_block_diag_in(bbt_re), _block_diag_in(bbt_im)], axis=-1).astype(BF16),
        "wcre": _block_diag_out(s5_c_re[l]).astype(BF16),
        "wcim": _block_diag_out(s5_c_im[l]).astype(BF16),
        "s5_d": row(s5_d[l]),
        "perm": perm,
        "permt": permt,
        "w_glu_f32": w_glu[l],
        "b_glu": row(b_glu[l]),
        "w_out_f32": w_out[l],
        "w_down_f32": w_down[l],
        "g_post_mix": row(g_post_mix[l]),
        "g_pre_ffn": row(g_pre_ffn[l]),
    }

    x1_p, hf_p, st_p = _mix_layer(
        x_prompt,
        jnp.zeros((bp, CONV_W - 1, QK_W), F32),
        jnp.zeros((bp, A_HEADS, A_DV, A_DQK), F32),
        jnp.zeros((bp, A_HEADS, A_DQK), F32),
        jnp.zeros((bp, A_HEADS), F32),
        jnp.zeros((bp, B_GROUPS, B_STATE), F32),
        jnp.zeros((bp, B_GROUPS, B_STATE), F32),
        p, tm=512, n_sub=2, chunk=128, tb=64)
    x1_s, hf_s, st_s = _mix_layer(
        x_sample, state_conv[l], state_mlstm_c[l], state_mlstm_n[l], state_mlstm_m[l],
        state_s5_re[l], state_s5_im[l],
        p, tm=128, n_sub=1, chunk=128, tb=16)

    act_p, act_s = _ffn_up(hf_p, hf_s, w_gate[l], w_up[l], tm=1024, tf=512)
    wd = p["w_down"]
    g_post = row(g_post_ffn[l])
    yp = _ffn_down(act_p, wd, x1_p, g_post, tm=512, n_sub=2).reshape(x_prompt.shape)
    ys = _ffn_down(act_s, wd, x1_s, g_post, tm=128, n_sub=1).reshape(x_sample.shape)
    return (yp, ys) + tuple(a[None] for a in st_p) + tuple(a[None] for a in st_s)
```

```python
import functools

import numpy as np
import jax
import jax.numpy as jnp
from jax import lax
from jax.experimental import pallas as pl
from jax.experimental.pallas import tpu as pltpu

F32 = jnp.float32
BF16 = jnp.bfloat16

D_MODEL = 2048
D_A = 1024
D_B = 1024
A_HEADS = 4
A_DV = 256
A_DQK = 128
QK_W = 1024
CONV_W = 4
B_GROUP = 16
B_GROUPS = 64
B_STATE = 64
S5_N = B_GROUPS * B_STATE
D_FF = 5632
EPS = 1e-6

SUBLANES = 8
LANES = 128
GROUPS_PER_BLOCK = LANES // B_GROUP
N_BLOCKS = B_GROUPS // GROUPS_PER_BLOCK
STATE_PER_BLOCK = GROUPS_PER_BLOCK * B_STATE
PERM_T = 16
VMEM_LIMIT = 56 * 1024 * 1024


def _sigmoid(x):
    return 1.0 / (1.0 + jnp.exp(-x))


def _gelu_tanh(x):
    c = np.sqrt(2 / np.pi).astype(np.float32)
    return x * (0.5 * (1.0 + jnp.tanh(c * (x + 0.044715 * (x ** 3)))))


def _rms(x, g):
    return x * lax.rsqrt(jnp.mean(x * x, axis=-1, keepdims=True) + EPS) * g


def _dot(a, b):
    return jnp.dot(a, b, preferred_element_type=F32)


def _log_sigmoid(z):
    return jnp.minimum(z, 0.0) - jnp.log1p(jnp.exp(-jnp.abs(z)))


def _split3(x):
    hi = x.astype(BF16)
    r = x - hi.astype(F32)
    mid = r.astype(BF16)
    lo = (r - mid.astype(F32)).astype(BF16)
    return hi, mid, lo


def _dot3_right(x, sel):
    hi, mid, lo = _split3(x)
    return _dot(hi, sel) + _dot(mid, sel) + _dot(lo, sel)


def _dot3_left(sel, x):
    hi, mid, lo = _split3(x)
    return _dot(sel, hi) + _dot(sel, mid) + _dot(sel, lo)


def _row_blocks(rows, n_sub):
    assert rows % n_sub == 0
    size = rows // n_sub
    return [slice(size * r, size * (r + 1)) for r in range(n_sub)]


def _resident(shape):
    zeros = (0,) * len(shape)
    return pl.BlockSpec(shape, lambda *_: zeros, pipeline_mode=pl.Buffered(1))


def _disc_kernel(lam_re_ref, lam_im_ref, logdt_ref, bre_ref, bim_ref,
                 lbre_ref, lbim_ref, bbre_ref, bbim_ref):
    lam_re = lam_re_ref[...]
    lam_im = lam_im_ref[...]
    dt = jnp.exp(logdt_ref[...])
    mag = jnp.exp(lam_re * dt)
    ang = lam_im * dt
    lb_re = mag * jnp.cos(ang)
    lb_im = mag * jnp.sin(ang)
    den = lam_re * lam_re + lam_im * lam_im
    nr = lb_re - 1.0
    f_re = (nr * lam_re + lb_im * lam_im) / den
    f_im = (lb_im * lam_re - nr * lam_im) / den
    lbre_ref[...] = lb_re
    lbim_ref[...] = lb_im
    b_re = bre_ref[...]
    b_im = bim_ref[...]
    bbre_ref[...] = f_re[None] * b_re - f_im[None] * b_im
    bbim_ref[...] = f_re[None] * b_im + f_im[None] * b_re


def _discretise(lam_re, lam_im, log_dt, bt_re, bt_im):
    gp = jax.ShapeDtypeStruct((B_GROUPS, B_STATE), F32)
    igp = jax.ShapeDtypeStruct((B_GROUP, B_GROUPS, B_STATE), F32)
    return pl.pallas_call(
        _disc_kernel, out_shape=(gp, gp, igp, igp), name="s5_discretise",
    )(lam_re, lam_im, log_dt.reshape(B_GROUPS, 1), bt_re, bt_im)


GATE_COL = QK_W + 2 * D_A
GATE_LANE_CMX = A_HEADS
GATE_LANE_FC = 2 * A_HEADS


def _dot_t(a, bt):
    return lax.dot_general(a, bt, (((1,), (1,)), ((), ())), preferred_element_type=F32)


def _inproj_kernel(x_ref, gpre_ref, wt_ref, wut_ref, wgt_ref, gb_ref, triu_ref, tril_ref, *rest,
                   n_sub, seg, n_cast):
    cast_in = rest[:n_cast]
    qk_out, v_out, og_out, u_out, g_out, c_out = rest[n_cast:n_cast + 6]
    cast_out = rest[n_cast + 6:2 * n_cast + 6]
    for src, dst in zip(cast_in, cast_out):
        dst[...] = src[...].astype(BF16)

    for rows in _row_blocks(x_ref.shape[0], n_sub):
        hm = _rms(x_ref[rows, :], gpre_ref[...]).astype(BF16)

        n = rows.stop - rows.start
        z = _dot_t(wgt_ref[...], hm) + gb_ref[...]
        qk_out[rows, :] = _dot_t(hm, wt_ref[0:QK_W, :])

        row = lax.broadcasted_iota(jnp.int32, z.shape, 0)
        zf = jnp.where((row >= A_HEADS) & (row < 2 * A_HEADS), _log_sigmoid(z), z)
        zf8 = zf[0:SUBLANES]
        zc = jnp.concatenate([zf8, jnp.zeros((LANES - SUBLANES, n), F32)], axis=0).T
        lane = lax.broadcasted_iota(jnp.int32, zc.shape, 1)
        lf_c = jnp.where((lane >= A_HEADS) & (lane < 2 * A_HEADS), zc, 0.0)
        fc = pltpu.roll(_dot3_right(zf, triu_ref[...])[0:SUBLANES], A_HEADS, axis=0)
        fc_c = _dot3_left(tril_ref[...], lf_c)
        v_out[rows, :] = _dot_t(hm, wt_ref[QK_W:QK_W + D_A, :]).astype(BF16)
        og_out[rows, :] = _dot_t(hm, wt_ref[QK_W + D_A:QK_W + 2 * D_A, :])
        u_out[rows, :] = _dot_t(hm, wut_ref[...])

        g_out[0:SUBLANES, rows] = zf8 - fc
        g_out[SUBLANES:2 * SUBLANES, rows] = fc
        cm = pltpu.roll(zc, A_HEADS, axis=1) - fc_c
        t_in = lax.broadcasted_iota(jnp.int32, zc.shape, 0) & (seg - 1)
        sh = 1
        while sh < seg:
            cm = jnp.where(t_in >= sh, jnp.maximum(cm, pltpu.roll(cm, sh, axis=0)), cm)
            sh *= 2
        c_out[rows, :] = jnp.where(lane < GATE_LANE_FC, cm, pltpu.roll(fc_c, A_HEADS, axis=1))


def _segment_prefix_matrices(n, seg):
    idx = np.arange(n)
    u = ((idx[:, None] // seg == idx[None, :] // seg) & (idx[:, None] <= idx[None, :]))
    u = u.astype(np.float32)
    return jnp.asarray(u, BF16), jnp.asarray(u.T, BF16)


def _in_proj(x2d, p, *, tm, n_sub, seg, cast=()):
    t = x2d.shape[0]
    n_steps = t // tm
    assert seg & (seg - 1) == 0
    triu, tril = _segment_prefix_matrices(tm // n_sub, seg)
    tok = lambda i: (i, 0)
    cast_specs = [pl.BlockSpec((w.shape[0] // n_steps, w.shape[1]), tok) for w in cast]
    return pl.pallas_call(
        functools.partial(_inproj_kernel, n_sub=n_sub, seg=seg, n_cast=len(cast)),
        grid=(n_steps,),
        in_specs=[
            pl.BlockSpec((tm, D_MODEL), tok),
            _resident((1, D_MODEL)),
            _resident((GATE_COL, D_MODEL)),
            _resident((D_B, D_MODEL)),
            _resident((2 * SUBLANES, D_MODEL)),
            _resident((2 * SUBLANES, 1)),
            _resident(triu.shape),
            _resident(tril.shape),
        ] + cast_specs,
        out_specs=[
            pl.BlockSpec((tm, QK_W), tok),
            pl.BlockSpec((tm, D_A), tok),
            pl.BlockSpec((tm, D_A), tok),
            pl.BlockSpec((tm, D_B), tok),
            pl.BlockSpec((2 * SUBLANES, tm), lambda i: (0, i)),
            pl.BlockSpec((tm, LANES), tok),
        ] + cast_specs,
        out_shape=(
            jax.ShapeDtypeStruct((t, QK_W), F32),
            jax.ShapeDtypeStruct((t, D_A), BF16),
            jax.ShapeDtypeStruct((t, D_A), F32),
            jax.ShapeDtypeStruct((t, D_B), F32),
            jax.ShapeDtypeStruct((2 * SUBLANES, t), F32),
            jax.ShapeDtypeStruct((t, LANES), F32),
        ) + tuple(jax.ShapeDtypeStruct(w.shape, BF16) for w in cast),
        compiler_params=pltpu.CompilerParams(
            dimension_semantics=("arbitrary",), vmem_limit_bytes=VMEM_LIMIT),
        name="in_proj",
    )(x2d, p["g_pre_mix"], p["w_in_t"], p["w_in_ut"], p["w_in_gt"], p["gbias"], triu, tril, *cast)


def _mlstm_kernel(qk_ref, v_ref, og_ref, g_ref, gc_ref, rep_ref, tail0_ref, c0_ref, n0_ref, m0_ref,
                  wconv_ref, bconv_ref, gmh_ref,
                  outa_ref, convo_ref, co_ref, no_ref, mo_ref,
                  xp, ct, ncr, m_s, *, chunk, valid, n_chunks):
    L = chunk
    c = pl.program_id(1)
    reps = L // LANES

    @pl.when(c == 0)
    def _():
        xp[0:SUBLANES, :] = tail0_ref[0]
        for h in range(A_HEADS):
            ct[h] = c0_ref[0, h].T
            ncr[h] = jnp.broadcast_to(n0_ref[0, h:h + 1, :], (A_DQK, A_DQK)).T
        m_s[...] = m0_ref[0]

    def lanes(x, n):
        return x if n == 1 else jnp.concatenate([x] * n, axis=1)

    xp[SUBLANES:SUBLANES + L, :] = qk_ref[0]
    xs = xp[...]
    acc = bconv_ref[...]
    for j in range(CONV_W):
        back = CONV_W - 1 - j
        tap = xs if back == 0 else pltpu.roll(xs, back, axis=0)
        acc = acc + tap[SUBLANES:, :] * wconv_ref[j:j + 1, :]
    qk = acc * _sigmoid(acc)
    q_all = (qk[:, :QK_W // 2] * (A_DQK ** -0.5)).astype(BF16)
    k_all = qk[:, QK_W // 2:].astype(BF16)

    @pl.when(c == n_chunks - 1)
    def _():
        convo_ref[0] = xp[valid:valid + SUBLANES, :]

    xp[0:SUBLANES, :] = xp[L:L + SUBLANES, :]

    rep = _dot3_right(gc_ref[0], rep_ref[...])

    g = g_ref[...]
    gr = g[0:SUBLANES]
    fc = g[SUBLANES:2 * SUBLANES]
    lane = lax.broadcasted_iota(jnp.int32, (SUBLANES, L), 1)
    f_last = jnp.sum(jnp.where(lane == L - 1, fc, 0.0), axis=1, keepdims=True)
    m0 = jnp.max(m_s[...], axis=1, keepdims=True)
    lw = f_last + gr
    m_new = jnp.maximum(f_last + m0, jnp.max(lw, axis=1, keepdims=True))
    ws = jnp.exp(lw - m_new)
    decay = jnp.broadcast_to(jnp.exp(f_last + m0 - m_new), (SUBLANES, LANES))

    row_i = lax.broadcasted_iota(jnp.int32, (L, L), 0)
    col_i = lax.broadcasted_iota(jnp.int32, (L, L), 1)
    tri = col_i <= row_i
    for h in range(A_HEADS):
        qb = q_all[:, A_DQK * h:A_DQK * (h + 1)]
        kb = k_all[:, A_DQK * h:A_DQK * (h + 1)]
        vb = v_ref[0, :, A_DV * h:A_DV * (h + 1)]
        m0_h = m_s[h:h + 1, :]
        mx = jnp.maximum(rep[:, LANES * h:LANES * (h + 1)], m0_h)
        fc_t = rep[:, LANES * (A_HEADS + h):LANES * (A_HEADS + h + 1)]
        e = jnp.exp(jnp.where(tri, gr[h:h + 1, :] - lanes(mx, reps), -jnp.inf))
        s = lax.dot_general(qb, kb, (((1,), (1,)), ((), ())),
                            preferred_element_type=F32) * e
        g_int = jnp.exp(m0_h - mx)
        ct_h = ct[h]
        ncr_h = ncr[h]
        num = (_dot(s.astype(BF16), vb)
               + lanes(g_int, A_DV // LANES) * _dot(qb, ct_h.astype(BF16)))
        den = jnp.sum(s, axis=1, keepdims=True) + g_int * _dot(qb, ncr_h.astype(BF16))
        scale = jnp.maximum(jnp.abs(den), jnp.exp(-(fc_t + mx)))
        hh = num / lanes(scale, A_DV // LANES)
        hn = _rms(hh, gmh_ref[:, A_DV * h:A_DV * (h + 1)])
        og = og_ref[0, :, A_DV * h:A_DV * (h + 1)]
        outa_ref[0, :, A_DV * h:A_DV * (h + 1)] = (hn * _sigmoid(og)).astype(BF16)
        kws = kb.astype(F32).T * ws[h:h + 1, :]
        dec_h = decay[h:h + 1, :]
        ct[h] = lanes(dec_h, A_DV // LANES) * ct_h + _dot(kws.astype(BF16), vb)
        ncr[h] = dec_h * ncr_h + jnp.sum(kws, axis=1, keepdims=True)
    m_s[...] = jnp.broadcast_to(m_new, m_s.shape)

    @pl.when(c == n_chunks - 1)
    def _():
        for h in range(A_HEADS):
            co_ref[0, h] = ct[h].T
            no_ref[0, h:h + 1, :] = ncr[h].T[0:1, :]
        mo_ref[0] = m_s[...]


def _gate_replication_matrix():
    r = np.zeros((LANES, 2 * A_HEADS * LANES), np.float32)
    for h in range(A_HEADS):
        r[GATE_LANE_CMX + h, LANES * h:LANES * (h + 1)] = 1.0
        r[GATE_LANE_FC + h, LANES * (A_HEADS + h):LANES * (A_HEADS + h + 1)] = 1.0
    return jnp.asarray(r, BF16)


def _mlstm(qk, v, og, gates, gates_c, tail0, c0, n0, m0, wconv, bconv, gmh, *, chunk, valid):
    b, s_len, _ = qk.shape
    n_chunks = s_len // chunk
    rep = _gate_replication_matrix()
    seq = lambda i, c: (i, c, 0)
    per_b3 = lambda i, c: (i, 0, 0)
    per_b4 = lambda i, c: (i, 0, 0, 0)
    const2 = lambda i, c: (0, 0)
    kernel = functools.partial(_mlstm_kernel, chunk=chunk, valid=valid, n_chunks=n_chunks)
    return pl.pallas_call(
        kernel,
        grid=(b, n_chunks),
        in_specs=[
            pl.BlockSpec((1, chunk, QK_W), seq),
            pl.BlockSpec((1, chunk, D_A), seq),
            pl.BlockSpec((1, chunk, D_A), seq),
            pl.BlockSpec((2 * SUBLANES, chunk), lambda i, c: (0, i * n_chunks + c)),
            pl.BlockSpec((1, chunk, LANES), seq),
            pl.BlockSpec(rep.shape, const2),
            pl.BlockSpec((1, SUBLANES, QK_W), per_b3),
            pl.BlockSpec((1, A_HEADS, A_DV, A_DQK), per_b4),
            pl.BlockSpec((1, A_HEADS, A_DQK), per_b3),
            pl.BlockSpec((1, SUBLANES, LANES), per_b3),
            pl.BlockSpec((CONV_W, QK_W), const2),
            pl.BlockSpec((1, QK_W), const2),
            pl.BlockSpec((1, D_A), const2),
        ],
        out_specs=[
            pl.BlockSpec((1, chunk, D_A), seq),
            pl.BlockSpec((1, SUBLANES, QK_W), per_b3),
            pl.BlockSpec((1, A_HEADS, A_DV, A_DQK), per_b4),
            pl.BlockSpec((1, A_HEADS, A_DQK), per_b3),
            pl.BlockSpec((1, SUBLANES, LANES), per_b3),
        ],
        out_shape=(
            jax.ShapeDtypeStruct((b, s_len, D_A), BF16),
            jax.ShapeDtypeStruct((b, SUBLANES, QK_W), F32),
            jax.ShapeDtypeStruct((b, A_HEADS, A_DV, A_DQK), F32),
            jax.ShapeDtypeStruct((b, A_HEADS, A_DQK), F32),
            jax.ShapeDtypeStruct((b, SUBLANES, LANES), F32),
        ),
        scratch_shapes=[
            pltpu.VMEM((chunk + SUBLANES, QK_W), F32),
            pltpu.VMEM((A_HEADS, A_DQK, A_DV), F32),
            pltpu.VMEM((A_HEADS, A_DQK, LANES), F32),
            pltpu.VMEM((SUBLANES, LANES), F32),
        ],
        compiler_params=pltpu.CompilerParams(
            dimension_semantics=("arbitrary", "arbitrary"), vmem_limit_bytes=VMEM_LIMIT),
        name="mlstm",
    )(qk, v, og, gates, gates_c, rep, tail0, c0, n0, m0, wconv, bconv, gmh)


def _s5_kernel(u_ref, s0re_ref, s0im_ref, lbre_ref, lbim_ref, wb_ref, wcre_ref, wcim_ref,
               d_ref, perm_ref, permt_ref, wglu_ref, bglu_ref,
               outb_ref, sre_out, sim_out,
               bure, buim, sre, sim, utb, xtb, ytb, *, tb, n_steps):
    step = pl.program_id(0)
    batch = u_ref.shape[0]
    rows_j = batch * PERM_T
    n_j = tb // PERM_T

    @pl.when(step == 0)
    def _():
        sre[...] = s0re_ref[...]
        sim[...] = s0im_ref[...]

    strip = 8 * LANES
    blocks_per_strip = strip // STATE_PER_BLOCK
    n_strips = S5_N // strip

    def permute_in():
        for j in range(n_j):
            xj = jnp.concatenate(
                [u_ref[b, PERM_T * j:PERM_T * (j + 1), :] for b in range(batch)], axis=0)
            hi = xj.astype(BF16)
            lo = (xj - hi.astype(F32)).astype(BF16)
            uh = _dot(perm_ref[...], hi)
            rows = slice(rows_j * j, rows_j * (j + 1))
            xtb[rows, :] = uh.astype(BF16)
            utb[rows, :] = uh + _dot(perm_ref[...], lo)

    def input_map(st):
        for k in range(blocks_per_strip * st, blocks_per_strip * (st + 1)):
            r = _dot(xtb[:, LANES * k:LANES * (k + 1)], wb_ref[k])
            kc = slice(STATE_PER_BLOCK * k, STATE_PER_BLOCK * (k + 1))
            bure[:, kc] = r[:, :STATE_PER_BLOCK]
            buim[:, kc] = r[:, STATE_PER_BLOCK:]

    def recurrence(st):
        cols = slice(strip * st, strip * (st + 1))
        a_re = jnp.broadcast_to(lbre_ref[:, cols], (batch, strip))
        a_im = jnp.broadcast_to(lbim_ref[:, cols], (batch, strip))
        s_r = sre[:, cols]
        s_i = sim[:, cols]
        for t in range(tb):
            tr = slice(t * batch, (t + 1) * batch)
            n_r = a_re * s_r - a_im * s_i + bure[tr, cols]
            n_i = a_re * s_i + a_im * s_r + buim[tr, cols]
            bure[tr, cols] = n_r
            buim[tr, cols] = n_i
            s_r, s_i = n_r, n_i
        sre[:, cols] = s_r
        sim[:, cols] = s_i

    def output_map(st):
        for k in range(blocks_per_strip * st, blocks_per_strip * (st + 1)):
            kc = slice(STATE_PER_BLOCK * k, STATE_PER_BLOCK * (k + 1))
            ytb[:, LANES * k:LANES * (k + 1)] = (
                _dot(bure[:, kc].astype(BF16), wcre_ref[k])
                - _dot(buim[:, kc].astype(BF16), wcim_ref[k]))

    permute_in()
    for st in range(n_strips):
        input_map(st)
        recurrence(st)
        output_map(st)

    z = _gelu_tanh(ytb[...] + d_ref[...] * utb[...])
    gate = _sigmoid(_dot(z.astype(BF16), wglu_ref[...]) + bglu_ref[...])
    res = (z * gate).astype(BF16)
    for j in range(n_j):
        rb = _dot(permt_ref[...], res[rows_j * j:rows_j * (j + 1), :]).astype(BF16)
        for b in range(batch):
            outb_ref[b, PERM_T * j:PERM_T * (j + 1), :] = rb[PERM_T * b:PERM_T * (b + 1), :]

    @pl.when(step == n_steps - 1)
    def _():
        sre_out[...] = sre[...]
        sim_out[...] = sim[...]


def _s5(u, s0re, s0im, lbre, lbim, wb, wcre, wcim, d_skip, perm, permt, wglu, bglu, *, tb):
    b, s_len, _ = u.shape
    n_steps = s_len // tb
    rows = b * tb
    kernel = functools.partial(_s5_kernel, tb=tb, n_steps=n_steps)
    state = jax.ShapeDtypeStruct((b, S5_N), F32)
    return pl.pallas_call(
        kernel,
        grid=(n_steps,),
        in_specs=[
            pl.BlockSpec((b, tb, D_B), lambda i: (0, i, 0)),
            _resident((b, S5_N)),
            _resident((b, S5_N)),
            _resident((1, S5_N)),
            _resident((1, S5_N)),
            _resident((N_BLOCKS, LANES, 2 * STATE_PER_BLOCK)),
            _resident((N_BLOCKS, STATE_PER_BLOCK, LANES)),
            _resident((N_BLOCKS, STATE_PER_BLOCK, LANES)),
            _resident((1, D_B)),
            _resident((b * PERM_T, b * PERM_T)),
            _resident((b * PERM_T, b * PERM_T)),
            _resident((D_B, D_B)),
            _resident((1, D_B)),
        ],
        out_specs=[
            pl.BlockSpec((b, tb, D_B), lambda i: (0, i, 0)),
            pl.BlockSpec((b, S5_N), lambda i: (0, 0)),
            pl.BlockSpec((b, S5_N), lambda i: (0, 0)),
        ],
        out_shape=(jax.ShapeDtypeStruct((b, s_len, D_B), BF16), state, state),
        scratch_shapes=[
            pltpu.VMEM((rows, S5_N), F32),
            pltpu.VMEM((rows, S5_N), F32),
            pltpu.VMEM((b, S5_N), F32),
            pltpu.VMEM((b, S5_N), F32),
            pltpu.VMEM((rows, D_B), F32),
            pltpu.VMEM((rows, D_B), BF16),
            pltpu.VMEM((rows, D_B), F32),
        ],
        compiler_params=pltpu.CompilerParams(
            dimension_semantics=("arbitrary",), vmem_limit_bytes=VMEM_LIMIT),
        name="s5",
    )(u, s0re, s0im, lbre, lbim, wb, wcre, wcim, d_skip, perm, permt, wglu, bglu)


def _outproj_kernel(a_ref, b_ref, x_ref, wo_ref, gpost_ref, gffn_ref, *rest, n_sub, n_cast):
    cast_in = rest[:n_cast]
    x1_ref, hf_ref = rest[n_cast:n_cast + 2]
    cast_out = rest[n_cast + 2:]
    for src, dst in zip(cast_in, cast_out):
        dst[...] = src[...].astype(BF16)
    for rows in _row_blocks(a_ref.shape[0], n_sub):
        mix = (_dot(a_ref[rows, :], wo_ref[0:D_A, :])
               + _dot(b_ref[rows, :], wo_ref[D_A:D_A + D_B, :]))
        x1 = x_ref[rows, :] + _rms(mix, gpost_ref[...])
        x1_ref[rows, :] = x1
        hf_ref[rows, :] = _rms(x1, gffn_ref[...]).astype(BF16)


def _out_proj(a, b, x2d, wo, g_post, g_ffn, tm, n_sub, cast=()):
    t = x2d.shape[0]
    n_steps = t // tm
    tok = lambda i: (i, 0)
    cast_specs = [pl.BlockSpec((w.shape[0] // n_steps, w.shape[1]), tok) for w in cast]
    return pl.pallas_call(
        functools.partial(_outproj_kernel, n_sub=n_sub, n_cast=len(cast)),
        grid=(n_steps,),
        in_specs=[
            pl.BlockSpec((tm, D_A), tok),
            pl.BlockSpec((tm, D_B), tok),
            pl.BlockSpec((tm, D_MODEL), tok),
            _resident((D_A + D_B, D_MODEL)),
            _resident((1, D_MODEL)),
            _resident((1, D_MODEL)),
        ] + cast_specs,
        out_specs=[pl.BlockSpec((tm, D_MODEL), tok), pl.BlockSpec((tm, D_MODEL), tok)] + cast_specs,
        out_shape=(jax.ShapeDtypeStruct((t, D_MODEL), F32),
                   jax.ShapeDtypeStruct((t, D_MODEL), BF16))
        + tuple(jax.ShapeDtypeStruct(w.shape, BF16) for w in cast),
        compiler_params=pltpu.CompilerParams(
            dimension_semantics=("arbitrary",), vmem_limit_bytes=VMEM_LIMIT),
        name="out_proj",
    )(a, b, x2d, wo, g_post, g_ffn, *cast)


def _ffn_up_kernel(hfp_ref, hfs_ref, wg_ref, wu_ref, actp_ref, acts_ref, wgb, wub):
    m = pl.program_id(1)

    @pl.when(m == 0)
    def _():
        wgb[...] = wg_ref[...].astype(BF16)
        wub[...] = wu_ref[...].astype(BF16)

    def body(hf_ref, act_ref):
        hf = hf_ref[...]
        a = _dot(hf, wgb[...])
        act_ref[...] = (a * _sigmoid(a) * _dot(hf, wub[...])).astype(BF16)

    @pl.when(m == 0)
    def _():
        body(hfs_ref, acts_ref)

    @pl.when(m > 0)
    def _():
        body(hfp_ref, actp_ref)


def _ffn_up(hf_p, hf_s, wg, wu, tm, tf):
    t_p, t_s = hf_p.shape[0], hf_s.shape[0]
    n_p = t_p // tm
    n_f = D_FF // tf
    prompt_tile = lambda m: jnp.maximum(m - 1, 0)
    return pl.pallas_call(
        _ffn_up_kernel,
        grid=(n_f, n_p + 1),
        in_specs=[
            pl.BlockSpec((tm, D_MODEL), lambda f, m: (prompt_tile(m), 0)),
            pl.BlockSpec((t_s, D_MODEL), lambda f, m: (0, 0)),
            pl.BlockSpec((D_MODEL, tf), lambda f, m: (0, f)),
            pl.BlockSpec((D_MODEL, tf), lambda f, m: (0, f)),
        ],
        out_specs=[
            pl.BlockSpec((tm, tf), lambda f, m: (prompt_tile(m), f)),
            pl.BlockSpec((t_s, tf), lambda f, m: (0, f)),
        ],
        out_shape=(jax.ShapeDtypeStruct((t_p, D_FF), BF16),
                   jax.ShapeDtypeStruct((t_s, D_FF), BF16)),
        scratch_shapes=[pltpu.VMEM((D_MODEL, tf), BF16), pltpu.VMEM((D_MODEL, tf), BF16)],
        compiler_params=pltpu.CompilerParams(
            dimension_semantics=("arbitrary", "arbitrary"), vmem_limit_bytes=VMEM_LIMIT),
        name="ffn_up",
    )(hf_p, hf_s, wg, wu)


def _ffn_down_kernel(act_ref, wd_ref, x1_ref, gp_ref, out_ref, *, n_sub):
    for rows in _row_blocks(act_ref.shape[0], n_sub):
        ff = _dot(act_ref[rows, :], wd_ref[...])
        out_ref[rows, :] = x1_ref[rows, :] + _rms(ff, gp_ref[...])


def _ffn_down(act, wd, x1, g_post, tm, n_sub):
    t = act.shape[0]
    tok = lambda i: (i, 0)
    return pl.pallas_call(
        functools.partial(_ffn_down_kernel, n_sub=n_sub),
        grid=(t // tm,),
        in_specs=[
            pl.BlockSpec((tm, D_FF), tok),
            _resident((D_FF, D_MODEL)),
            pl.BlockSpec((tm, D_MODEL), tok),
            _resident((1, D_MODEL)),
        ],
        out_specs=pl.BlockSpec((tm, D_MODEL), tok),
        out_shape=jax.ShapeDtypeStruct((t, D_MODEL), F32),
        compiler_params=pltpu.CompilerParams(
            dimension_semantics=("arbitrary",), vmem_limit_bytes=VMEM_LIMIT),
        name="ffn_down",
    )(act, wd, x1, g_post)


def _block_diag_in(bbt):
    x = bbt.reshape(B_GROUP, N_BLOCKS, GROUPS_PER_BLOCK, B_STATE)
    x = jnp.transpose(x, (1, 2, 0, 3))
    eye = jnp.eye(GROUPS_PER_BLOCK, dtype=bool)[None, :, None, :, None]
    y = jnp.where(eye, x[:, :, :, None, :], 0.0)
    return y.reshape(N_BLOCKS, LANES, STATE_PER_BLOCK)


def _block_diag_out(c):
    x = c.reshape(N_BLOCKS, GROUPS_PER_BLOCK, B_GROUP, B_STATE)
    x = jnp.transpose(x, (0, 1, 3, 2))
    eye = jnp.eye(GROUPS_PER_BLOCK, dtype=bool)[None, :, None, :, None]
    y = jnp.where(eye, x[:, :, :, None, :], 0.0)
    return y.reshape(N_BLOCKS, STATE_PER_BLOCK, LANES)


def _row_permutation(batch):
    n = batch * PERM_T
    src = np.arange(n)
    b, t = src // PERM_T, src % PERM_T
    p = np.zeros((n, n), np.float32)
    p[t * batch + b, src] = 1.0
    return jnp.asarray(p, BF16), jnp.asarray(p.T, BF16)


def _pad_rows_front(x, rows):
    pad = [(0, 0)] * x.ndim
    pad[-2] = (rows - x.shape[-2], 0)
    return jnp.pad(x, pad)


def _pad_rows_back(x, rows):
    pad = [(0, 0)] * x.ndim
    pad[-2] = (0, rows - x.shape[-2])
    return jnp.pad(x, pad)


def _mix_layer(x, conv0, c0, n0, m0, sre0, sim0, p, *, tm, n_sub, chunk, tb):
    b, s_len, _ = x.shape
    t = b * s_len
    x2d = x.reshape(t, D_MODEL)
    assert s_len % chunk == 0 or s_len < chunk
    seg = min(s_len, chunk)
    if "w_glu" in p:
        qk, v, og, u, gates, gates_c = _in_proj(x2d, p, tm=tm, n_sub=n_sub, seg=seg)
    else:
        qk, v, og, u, gates, gates_c, p["w_glu"], p["w_out"] = _in_proj(
            x2d, p, tm=tm, n_sub=n_sub, seg=seg, cast=(p["w_glu_f32"], p["w_out_f32"]))

    s_pad = -(-s_len // chunk) * chunk
    extra = s_pad - s_len
    def seq3(a, mode="constant"):
        a = a.reshape(b, s_len, a.shape[-1])
        return a if extra == 0 else jnp.pad(a, ((0, 0), (0, extra), (0, 0)), mode=mode)
    g3 = gates.reshape(2 * SUBLANES, b, s_len)
    if extra:
        pad_t = ((0, 0), (0, 0), (0, extra))
        g3 = jnp.concatenate([
            jnp.pad(g3[:SUBLANES], pad_t, constant_values=-jnp.inf),
            jnp.pad(g3[SUBLANES:], pad_t, mode="edge")], axis=0)
    out_a, conv_o, c_o, n_o, m_o = _mlstm(
        seq3(qk), seq3(v), seq3(og), g3.reshape(2 * SUBLANES, b * s_pad),
        seq3(gates_c, "edge"), _pad_rows_front(conv0, SUBLANES), c0, n0,
        jnp.broadcast_to(_pad_rows_back(m0[..., None], SUBLANES), (b, SUBLANES, LANES)),
        p["w_conv"], p["b_conv"], p["g_mh"], chunk=chunk, valid=seg)
    out_a = out_a[:, :s_len].reshape(t, D_A)

    out_b, sre_o, sim_o = _s5(
        u.reshape(b, s_len, D_B), sre0.reshape(b, S5_N), sim0.reshape(b, S5_N),
        p["lbre"], p["lbim"], p["wb"], p["wcre"], p["wcim"], p["s5_d"],
        p["perm"], p["permt"], p["w_glu"], p["b_glu"], tb=tb)

    cast = () if "w_down" in p else (p["w_down_f32"],)
    x1, hf, *cast_out = _out_proj(out_a, out_b.reshape(t, D_B), x2d, p["w_out"],
                                  p["g_post_mix"], p["g_pre_ffn"], tm, 2 * n_sub, cast=cast)
    if cast_out:
        p["w_down"] = cast_out[0]
    states = (
        conv_o[:, SUBLANES - (CONV_W - 1):, :],
        c_o,
        n_o,
        m_o[:, :A_HEADS, 0],
        sre_o.reshape(b, B_GROUPS, B_STATE),
        sim_o.reshape(b, B_GROUPS, B_STATE),
    )
    return x1, hf, states


def kernel(x_prompt, x_sample, state_conv, state_mlstm_c, state_mlstm_n, state_mlstm_m, state_s5_re, state_s5_im, g_pre_mix, w_in, w_conv, b_conv, b_igate, b_fgate, g_mh, s5_lam_re, s5_lam_im, s5_log_dt, s5_b_re, s5_b_im, s5_c_re, s5_c_im, s5_d, w_glu, b_glu, w_out, g_post_mix, g_pre_ffn, w_gate, w_up, w_down, g_post_ffn):
    l = 0
    bp = x_prompt.shape[0]
    lbre, lbim, bbt_re, bbt_im = _discretise(
        s5_lam_re[l], s5_lam_im[l], s5_log_dt[l],
        jnp.transpose(s5_b_re[l], (2, 0, 1)), jnp.transpose(s5_b_im[l], (2, 0, 1)))

    o1 = QK_W
    o2 = o1 + D_A
    o3 = o2 + D_A
    o4 = o3 + A_HEADS
    o5 = o4 + A_HEADS
    wt = jnp.swapaxes(w_in[l], 0, 1)
    perm, permt = _row_permutation(bp)
    row = lambda a: a.reshape(1, -1)
    p = {
        "g_pre_mix": row(g_pre_mix[l]),
        "w_in_t": wt[:o3].astype(BF16),
        "w_in_gt": jnp.pad(wt[o3:o5], ((0, 2 * SUBLANES - 2 * A_HEADS), (0, 0))).astype(BF16),
        "w_in_ut": wt[o5:].astype(BF16),
        "gbias": jnp.concatenate(
            [b_igate[l], b_fgate[l], jnp.zeros((SUBLANES,), F32)]).reshape(2 * SUBLANES, 1),
        "w_conv": w_conv[l],
        "b_conv": row(b_conv[l]),
        "g_mh": row(g_mh[l]),
        "lbre": lbre.reshape(1, S5_N),
        "lbim": lbim.reshape(1, S5_N),
        "wb": jnp.concatenate([_block_diag_in(bbt_re), _block_diag_in(bbt_im)], axis=-1).astype(BF16),
        "wcre": _block_diag_out(s5_c_re[l]).astype(BF16),
        "wcim": _block_diag_out(s5_c_im[l]).astype(BF16),
        "s5_d": row(s5_d[l]),
        "perm": perm,
        "permt": permt,
        "w_glu_f32": w_glu[l],
        "b_glu": row(b_glu[l]),
        "w_out_f32": w_out[l],
        "w_down_f32": w_down[l],
        "g_post_mix": row(g_post_mix[l]),
        "g_pre_ffn": row(g_pre_ffn[l]),
    }

    x1_p, hf_p, st_p = _mix_layer(
        x_prompt,
        jnp.zeros((bp, CONV_W - 1, QK_W), F32),
        jnp.zeros((bp, A_HEADS, A_DV, A_DQK), F32),
        jnp.zeros((bp, A_HEADS, A_DQK), F32),
        jnp.zeros((bp, A_HEADS), F32),
        jnp.zeros((bp, B_GROUPS, B_STATE), F32),
        jnp.zeros((bp, B_GROUPS, B_STATE), F32),
        p, tm=512, n_sub=2, chunk=128, tb=64)
    x1_s, hf_s, st_s = _mix_layer(
        x_sample, state_conv[l], state_mlstm_c[l], state_mlstm_n[l], state_mlstm_m[l],
        state_s5_re[l], state_s5_im[l],
        p, tm=128, n_sub=1, chunk=128, tb=16)

    act_p, act_s = _ffn_up(hf_p, hf_s, w_gate[l], w_up[l], tm=1024, tf=512)
    wd = p["w_down"]
    g_post = row(g_post_ffn[l])
    yp = _ffn_down(act_p, wd, x1_p, g_post, tm=512, n_sub=2).reshape(x_prompt.shape)
    ys = _ffn_down(act_s, wd, x1_s, g_post, tm=128, n_sub=1).reshape(x_sample.shape)
    return (yp, ys) + tuple(a[None] for a in st_p) + tuple(a[None] for a in st_s)
```

```python
import functools

import numpy as np
import jax
import jax.numpy as jnp
from jax import lax
from jax.experimental import pallas as pl
from jax.experimental.pallas import tpu as pltpu

F32 = jnp.float32
BF16 = jnp.bfloat16

D_MODEL = 2048
D_A = 1024
D_B = 1024
A_HEADS = 4
A_DV = 256
A_DQK = 128
QK_W = 1024
CONV_W = 4
B_GROUP = 16
B_GROUPS = 64
B_STATE = 64
S5_N = B_GROUPS * B_STATE
D_FF = 5632
EPS = 1e-6

SUBLANES = 8
LANES = 128
GROUPS_PER_BLOCK = LANES // B_GROUP
N_BLOCKS = B_GROUPS // GROUPS_PER_BLOCK
STATE_PER_BLOCK = GROUPS_PER_BLOCK * B_STATE
PERM_T = 16
VMEM_LIMIT = 56 * 1024 * 1024


def _sigmoid(x):
    return 1.0 / (1.0 + jnp.exp(-x))


def _gelu_tanh(x):
    c = np.sqrt(2 / np.pi).astype(np.float32)
    return x * (0.5 * (1.0 + jnp.tanh(c * (x + 0.044715 * (x ** 3)))))


def _rms(x, g):
    return x * lax.rsqrt(jnp.mean(x * x, axis=-1, keepdims=True) + EPS) * g


def _dot(a, b):
    return jnp.dot(a, b, preferred_element_type=F32)


def _log_sigmoid(z):
    return jnp.minimum(z, 0.0) - jnp.log1p(jnp.exp(-jnp.abs(z)))


def _split3(x):
    hi = x.astype(BF16)
    r = x - hi.astype(F32)
    mid = r.astype(BF16)
    lo = (r - mid.astype(F32)).astype(BF16)
    return hi, mid, lo


def _dot3_right(x, sel):
    hi, mid, lo = _split3(x)
    return _dot(hi, sel) + _dot(mid, sel) + _dot(lo, sel)


def _dot3_left(sel, x):
    hi, mid, lo = _split3(x)
    return _dot(sel, hi) + _dot(sel, mid) + _dot(sel, lo)


def _row_blocks(rows, n_sub):
    assert rows % n_sub == 0
    size = rows // n_sub
    return [slice(size * r, size * (r + 1)) for r in range(n_sub)]


def _resident(shape):
    zeros = (0,) * len(shape)
    return pl.BlockSpec(shape, lambda *_: zeros, pipeline_mode=pl.Buffered(1))


def _disc_kernel(lam_re_ref, lam_im_ref, logdt_ref, bre_ref, bim_ref,
                 lbre_ref, lbim_ref, bbre_ref, bbim_ref):
    lam_re = lam_re_ref[...]
    lam_im = lam_im_ref[...]
    dt = jnp.exp(logdt_ref[...])
    mag = jnp.exp(lam_re * dt)
    ang = lam_im * dt
    lb_re = mag * jnp.cos(ang)
    lb_im = mag * jnp.sin(ang)
    den = lam_re * lam_re + lam_im * lam_im
    nr = lb_re - 1.0
    f_re = (nr * lam_re + lb_im * lam_im) / den
    f_im = (lb_im * lam_re - nr * lam_im) / den
    lbre_ref[...] = lb_re
    lbim_ref[...] = lb_im
    b_re = bre_ref[...]
    b_im = bim_ref[...]
    bbre_ref[...] = f_re[None] * b_re - f_im[None] * b_im
    bbim_ref[...] = f_re[None] * b_im + f_im[None] * b_re


def _discretise(lam_re, lam_im, log_dt, bt_re, bt_im):
    gp = jax.ShapeDtypeStruct((B_GROUPS, B_STATE), F32)
    igp = jax.ShapeDtypeStruct((B_GROUP, B_GROUPS, B_STATE), F32)
    return pl.pallas_call(
        _disc_kernel, out_shape=(gp, gp, igp, igp), name="s5_discretise",
    )(lam_re, lam_im, log_dt.reshape(B_GROUPS, 1), bt_re, bt_im)


GATE_COL = QK_W + 2 * D_A
GATE_LANE_CMX = A_HEADS
GATE_LANE_FC = 2 * A_HEADS


def _dot_t(a, bt):
    return lax.dot_general(a, bt, (((1,), (1,)), ((), ())), preferred_element_type=F32)


def _inproj_kernel(x_ref, gpre_ref, wt_ref, wut_ref, wgt_ref, gb_ref, triu_ref, tril_ref, *rest,
                   n_sub, seg, n_cast):
    cast_in = rest[:n_cast]
    qk_out, v_out, og_out, u_out, g_out, c_out = rest[n_cast:n_cast + 6]
    cast_out = rest[n_cast + 6:2 * n_cast + 6]
    for src, dst in zip(cast_in, cast_out):
        dst[...] = src[...].astype(BF16)

    for rows in _row_blocks(x_ref.shape[0], n_sub):
        hm = _rms(x_ref[rows, :], gpre_ref[...]).astype(BF16)

        n = rows.stop - rows.start
        z = _dot_t(wgt_ref[...], hm) + gb_ref[...]
        qk_out[rows, :] = _dot_t(hm, wt_ref[0:QK_W, :])

        row = lax.broadcasted_iota(jnp.int32, z.shape, 0)
        zf = jnp.where((row >= A_HEADS) & (row < 2 * A_HEADS), _log_sigmoid(z), z)
        zf8 = zf[0:SUBLANES]
        zc = jnp.concatenate([zf8, jnp.zeros((LANES - SUBLANES, n), F32)], axis=0).T
        lane = lax.broadcasted_iota(jnp.int32, zc.shape, 1)
        lf_c = jnp.where((lane >= A_HEADS) & (lane < 2 * A_HEADS), zc, 0.0)
        fc = pltpu.roll(_dot3_right(zf, triu_ref[...])[0:SUBLANES], A_HEADS, axis=0)
        fc_c = _dot3_left(tril_ref[...], lf_c)
        v_out[rows, :] = _dot_t(hm, wt_ref[QK_W:QK_W + D_A, :]).astype(BF16)
        og_out[rows, :] = _dot_t(hm, wt_ref[QK_W + D_A:QK_W + 2 * D_A, :])
        u_out[rows, :] = _dot_t(hm, wut_ref[...])

        g_out[0:SUBLANES, rows] = zf8 - fc
        g_out[SUBLANES:2 * SUBLANES, rows] = fc
        cm = pltpu.roll(zc, A_HEADS, axis=1) - fc_c
        t_in = lax.broadcasted_iota(jnp.int32, zc.shape, 0) & (seg - 1)
        sh = 1
        while sh < seg:
            cm = jnp.where(t_in >= sh, jnp.maximum(cm, pltpu.roll(cm, sh, axis=0)), cm)
            sh *= 2
        c_out[rows, :] = jnp.where(lane < GATE_LANE_FC, cm, pltpu.roll(fc_c, A_HEADS, axis=1))


def _segment_prefix_matrices(n, seg):
    idx = np.arange(n)
    u = ((idx[:, None] // seg == idx[None, :] // seg) & (idx[:, None] <= idx[None, :]))
    u = u.astype(np.float32)
    return jnp.asarray(u, BF16), jnp.asarray(u.T, BF16)


def _in_proj(x2d, p, *, tm, n_sub, seg, cast=()):
    t = x2d.shape[0]
    n_steps = t // tm
    assert seg & (seg - 1) == 0
    triu, tril = _segment_prefix_matrices(tm // n_sub, seg)
    tok = lambda i: (i, 0)
    cast_specs = [pl.BlockSpec((w.shape[0] // n_steps, w.shape[1]), tok) for w in cast]
    return pl.pallas_call(
        functools.partial(_inproj_kernel, n_sub=n_sub, seg=seg, n_cast=len(cast)),
        grid=(n_steps,),
        in_specs=[
            pl.BlockSpec((tm, D_MODEL), tok),
            _resident((1, D_MODEL)),
            _resident((GATE_COL, D_MODEL)),
            _resident((D_B, D_MODEL)),
            _resident((2 * SUBLANES, D_MODEL)),
            _resident((2 * SUBLANES, 1)),
            _resident(triu.shape),
            _resident(tril.shape),
        ] + cast_specs,
        out_specs=[
            pl.BlockSpec((tm, QK_W), tok),
            pl.BlockSpec((tm, D_A), tok),
            pl.BlockSpec((tm, D_A), tok),
            pl.BlockSpec((tm, D_B), tok),
            pl.BlockSpec((2 * SUBLANES, tm), lambda i: (0, i)),
            pl.BlockSpec((tm, LANES), tok),
        ] + cast_specs,
        out_shape=(
            jax.ShapeDtypeStruct((t, QK_W), F32),
            jax.ShapeDtypeStruct((t, D_A), BF16),
            jax.ShapeDtypeStruct((t, D_A), F32),
            jax.ShapeDtypeStruct((t, D_B), F32),
            jax.ShapeDtypeStruct((2 * SUBLANES, t), F32),
            jax.ShapeDtypeStruct((t, LANES), F32),
        ) + tuple(jax.ShapeDtypeStruct(w.shape, BF16) for w in cast),
        compiler_params=pltpu.CompilerParams(
            dimension_semantics=("arbitrary",), vmem_limit_bytes=VMEM_LIMIT),
        name="in_proj",
    )(x2d, p["g_pre_mix"], p["w_in_t"], p["w_in_ut"], p["w_in_gt"], p["gbias"], triu, tril, *cast)


MLSTM_STREAMS = 2


def _mlstm_kernel(*refs, chunk, valid, n_chunks):
    ns = MLSTM_STREAMS
    qk_ref, v_ref, og_ref = refs[0:3]
    g_refs = refs[3:3 + ns]
    (gc_ref, rep_ref, tail0_ref, c0_ref, n0_ref, m0_ref, wconv_ref, bconv_ref, gmh_ref,
     outa_ref, convo_ref, co_ref, no_ref, mo_ref, xp, ct, ncr, m_s) = refs[3 + ns:]
    L = chunk
    c = pl.program_id(1)
    reps = L // LANES

    @pl.when(c == 0)
    def _():
        for st in range(ns):
            xp[st, 0:SUBLANES, :] = tail0_ref[st]
            for h in range(A_HEADS):
                ct[st, h] = c0_ref[st, h].T
                ncr[st, h] = jnp.broadcast_to(n0_ref[st, h:h + 1, :], (A_DQK, A_DQK)).T
            m_s[st] = m0_ref[st]

    def lanes(x, n):
        return x if n == 1 else jnp.concatenate([x] * n, axis=1)

    row_i = lax.broadcasted_iota(jnp.int32, (L, L), 0)
    col_i = lax.broadcasted_iota(jnp.int32, (L, L), 1)
    tri = col_i <= row_i
    lane = lax.broadcasted_iota(jnp.int32, (SUBLANES, L), 1)

    for st in range(ns):
        xp[st, SUBLANES:SUBLANES + L, :] = qk_ref[st]
        xs = xp[st]
        assert CONV_W == 4
        x1 = pltpu.roll(xs, 1, axis=0)
        near = xs * wconv_ref[3:4, :] + x1 * wconv_ref[2:3, :]
        far = xs * wconv_ref[1:2, :] + x1 * wconv_ref[0:1, :]
        acc = (near + pltpu.roll(far, 2, axis=0))[SUBLANES:, :] + bconv_ref[...]
        qk = acc * _sigmoid(acc)
        q_all = (qk[:, :QK_W // 2] * (A_DQK ** -0.5)).astype(BF16)
        k_all = qk[:, QK_W // 2:].astype(BF16)

        rep = _dot3_right(gc_ref[st], rep_ref[...])

        g = g_refs[st][...]
        gr = g[0:SUBLANES]
        fc = g[SUBLANES:2 * SUBLANES]
        f_last = jnp.sum(jnp.where(lane == L - 1, fc, 0.0), axis=1, keepdims=True)
        m_prev = m_s[st]
        m0 = jnp.max(m_prev, axis=1, keepdims=True)
        lw = f_last + gr
        m_new = jnp.maximum(f_last + m0, jnp.max(lw, axis=1, keepdims=True))
        ws = jnp.exp(lw - m_new)
        decay = jnp.broadcast_to(jnp.exp(f_last + m0 - m_new), (SUBLANES, LANES))

        for h in range(A_HEADS):
            qb = q_all[:, A_DQK * h:A_DQK * (h + 1)]
            kb = k_all[:, A_DQK * h:A_DQK * (h + 1)]
            vb = v_ref[st, :, A_DV * h:A_DV * (h + 1)]
            m0_h = m_prev[h:h + 1, :]
            mx = jnp.maximum(rep[:, LANES * h:LANES * (h + 1)], m0_h)
            fc_t = rep[:, LANES * (A_HEADS + h):LANES * (A_HEADS + h + 1)]
            e = jnp.exp(jnp.where(tri, gr[h:h + 1, :] - lanes(mx, reps), -jnp.inf))
            s = lax.dot_general(qb, kb, (((1,), (1,)), ((), ())),
                                preferred_element_type=F32) * e
            g_int = jnp.exp(m0_h - mx)
            ct_h = ct[st, h]
            ncr_h = ncr[st, h]
            num = (_dot(s.astype(BF16), vb)
                   + lanes(g_int, A_DV // LANES) * _dot(qb, ct_h.astype(BF16)))
            den = jnp.sum(s, axis=1, keepdims=True) + g_int * _dot(qb, ncr_h.astype(BF16))
            scale = jnp.maximum(jnp.abs(den), jnp.exp(-(fc_t + mx)))
            hh = num / lanes(scale, A_DV // LANES)
            hn = _rms(hh, gmh_ref[:, A_DV * h:A_DV * (h + 1)])
            og = og_ref[st, :, A_DV * h:A_DV * (h + 1)]
            outa_ref[st, :, A_DV * h:A_DV * (h + 1)] = (hn * _sigmoid(og)).astype(BF16)
            kws = kb.astype(F32).T * ws[h:h + 1, :]
            dec_h = decay[h:h + 1, :]
            ct[st, h] = lanes(dec_h, A_DV // LANES) * ct_h + _dot(kws.astype(BF16), vb)
            ncr[st, h] = dec_h * ncr_h + jnp.sum(kws, axis=1, keepdims=True)
        m_s[st] = jnp.broadcast_to(m_new, (SUBLANES, LANES))

    @pl.when(c == n_chunks - 1)
    def _():
        for st in range(ns):
            convo_ref[st] = xp[st, valid:valid + SUBLANES, :]
            for h in range(A_HEADS):
                co_ref[st, h] = ct[st, h].T
                no_ref[st, h:h + 1, :] = ncr[st, h].T[0:1, :]
            mo_ref[st] = m_s[st]

    for st in range(ns):
        xp[st, 0:SUBLANES, :] = xp[st, L:L + SUBLANES, :]


def _gate_replication_matrix():
    r = np.zeros((LANES, 2 * A_HEADS * LANES), np.float32)
    for h in range(A_HEADS):
        r[GATE_LANE_CMX + h, LANES * h:LANES * (h + 1)] = 1.0
        r[GATE_LANE_FC + h, LANES * (A_HEADS + h):LANES * (A_HEADS + h + 1)] = 1.0
    return jnp.asarray(r, BF16)


def _mlstm(qk, v, og, gates, gates_c, tail0, c0, n0, m0, wconv, bconv, gmh, *, chunk, valid):
    b, s_len, _ = qk.shape
    ns = MLSTM_STREAMS
    assert b % ns == 0
    n_chunks = s_len // chunk
    rep = _gate_replication_matrix()
    seq = lambda i, c: (i, c, 0)
    per_b3 = lambda i, c: (i, 0, 0)
    per_b4 = lambda i, c: (i, 0, 0, 0)
    const2 = lambda i, c: (0, 0)
    gate_specs = [
        pl.BlockSpec((2 * SUBLANES, chunk), lambda i, c, st=st: (0, (ns * i + st) * n_chunks + c))
        for st in range(ns)]
    kernel = functools.partial(_mlstm_kernel, chunk=chunk, valid=valid, n_chunks=n_chunks)
    return pl.pallas_call(
        kernel,
        grid=(b // ns, n_chunks),
        in_specs=[
            pl.BlockSpec((ns, chunk, QK_W), seq),
            pl.BlockSpec((ns, chunk, D_A), seq),
            pl.BlockSpec((ns, chunk, D_A), seq),
        ] + gate_specs + [
            pl.BlockSpec((ns, chunk, LANES), seq),
            pl.BlockSpec(rep.shape, const2),
            pl.BlockSpec((ns, SUBLANES, QK_W), per_b3),
            pl.BlockSpec((ns, A_HEADS, A_DV, A_DQK), per_b4),
            pl.BlockSpec((ns, A_HEADS, A_DQK), per_b3),
            pl.BlockSpec((ns, SUBLANES, LANES), per_b3),
            pl.BlockSpec((CONV_W, QK_W), const2),
            pl.BlockSpec((1, QK_W), const2),
            pl.BlockSpec((1, D_A), const2),
        ],
        out_specs=[
            pl.BlockSpec((ns, chunk, D_A), seq),
            pl.BlockSpec((ns, SUBLANES, QK_W), per_b3),
            pl.BlockSpec((ns, A_HEADS, A_DV, A_DQK), per_b4),
            pl.BlockSpec((ns, A_HEADS, A_DQK), per_b3),
            pl.BlockSpec((ns, SUBLANES, LANES), per_b3),
        ],
        out_shape=(
            jax.ShapeDtypeStruct((b, s_len, D_A), BF16),
            jax.ShapeDtypeStruct((b, SUBLANES, QK_W), F32),
            jax.ShapeDtypeStruct((b, A_HEADS, A_DV, A_DQK), F32),
            jax.ShapeDtypeStruct((b, A_HEADS, A_DQK), F32),
            jax.ShapeDtypeStruct((b, SUBLANES, LANES), F32),
        ),
        scratch_shapes=[
            pltpu.VMEM((ns, chunk + SUBLANES, QK_W), F32),
            pltpu.VMEM((ns, A_HEADS, A_DQK, A_DV), F32),
            pltpu.VMEM((ns, A_HEADS, A_DQK, LANES), F32),
            pltpu.VMEM((ns, SUBLANES, LANES), F32),
        ],
        compiler_params=pltpu.CompilerParams(
            dimension_semantics=("arbitrary", "arbitrary"), vmem_limit_bytes=VMEM_LIMIT),
        name="mlstm",
    )(qk, v, og, *([gates] * ns), gates_c, rep, tail0, c0, n0, m0, wconv, bconv, gmh)


def _s5_kernel(u_ref, s0re_ref, s0im_ref, lbre_ref, lbim_ref, wb_ref, wcre_ref, wcim_ref,
               d_ref, perm_ref, permt_ref, wglu_ref, bglu_ref,
               outb_ref, sre_out, sim_out,
               bure, buim, sre, sim, utb, xtb, ytb, *, tb, n_steps):
    step = pl.program_id(0)
    batch = u_ref.shape[0]
    rows_j = batch * PERM_T
    n_j = tb // PERM_T

    @pl.when(step == 0)
    def _():
        sre[...] = s0re_ref[...]
        sim[...] = s0im_ref[...]

    strip = 8 * LANES
    blocks_per_strip = strip // STATE_PER_BLOCK
    n_strips = S5_N // strip

    def permute_in():
        for j in range(n_j):
            xj = jnp.concatenate(
                [u_ref[b, PERM_T * j:PERM_T * (j + 1), :] for b in range(batch)], axis=0)
            hi = xj.astype(BF16)
            lo = (xj - hi.astype(F32)).astype(BF16)
            uh = _dot(perm_ref[...], hi)
            rows = slice(rows_j * j, rows_j * (j + 1))
            xtb[rows, :] = uh.astype(BF16)
            utb[rows, :] = uh + _dot(perm_ref[...], lo)

    def input_map(st):
        for k in range(blocks_per_strip * st, blocks_per_strip * (st + 1)):
            r = _dot(xtb[:, LANES * k:LANES * (k + 1)], wb_ref[k])
            kc = slice(STATE_PER_BLOCK * k, STATE_PER_BLOCK * (k + 1))
            bure[:, kc] = r[:, :STATE_PER_BLOCK]
            buim[:, kc] = r[:, STATE_PER_BLOCK:]

    def recurrence(st):
        cols = slice(strip * st, strip * (st + 1))
        a_re = jnp.broadcast_to(lbre_ref[:, cols], (batch, strip))
        a_im = jnp.broadcast_to(lbim_ref[:, cols], (batch, strip))
        s_r = sre[:, cols]
        s_i = sim[:, cols]
        for t in range(tb):
            tr = slice(t * batch, (t + 1) * batch)
            n_r = a_re * s_r - a_im * s_i + bure[tr, cols]
            n_i = a_re * s_i + a_im * s_r + buim[tr, cols]
            bure[tr, cols] = n_r
            buim[tr, cols] = n_i
            s_r, s_i = n_r, n_i
        sre[:, cols] = s_r
        sim[:, cols] = s_i

    def output_map(st):
        for k in range(blocks_per_strip * st, blocks_per_strip * (st + 1)):
            kc = slice(STATE_PER_BLOCK * k, STATE_PER_BLOCK * (k + 1))
            ytb[:, LANES * k:LANES * (k + 1)] = (
                _dot(bure[:, kc].astype(BF16), wcre_ref[k])
                - _dot(buim[:, kc].astype(BF16), wcim_ref[k]))

    permute_in()
    for st in range(n_strips):
        input_map(st)
        recurrence(st)
        output_map(st)

    z = _gelu_tanh(ytb[...] + d_ref[...] * utb[...])
    gate = _sigmoid(_dot(z.astype(BF16), wglu_ref[...]) + bglu_ref[...])
    res = (z * gate).astype(BF16)
    for j in range(n_j):
        rb = _dot(permt_ref[...], res[rows_j * j:rows_j * (j + 1), :]).astype(BF16)
        for b in range(batch):
            outb_ref[b, PERM_T * j:PERM_T * (j + 1), :] = rb[PERM_T * b:PERM_T * (b + 1), :]

    @pl.when(step == n_steps - 1)
    def _():
        sre_out[...] = sre[...]
        sim_out[...] = sim[...]


def _s5(u, s0re, s0im, lbre, lbim, wb, wcre, wcim, d_skip, perm, permt, wglu, bglu, *, tb):
    b, s_len, _ = u.shape
    n_steps = s_len // tb
    rows = b * tb
    kernel = functools.partial(_s5_kernel, tb=tb, n_steps=n_steps)
    state = jax.ShapeDtypeStruct((b, S5_N), F32)
    return pl.pallas_call(
        kernel,
        grid=(n_steps,),
        in_specs=[
            pl.BlockSpec((b, tb, D_B), lambda i: (0, i, 0)),
            _resident((b, S5_N)),
            _resident((b, S5_N)),
            _resident((1, S5_N)),
            _resident((1, S5_N)),
            _resident((N_BLOCKS, LANES, 2 * STATE_PER_BLOCK)),
            _resident((N_BLOCKS, STATE_PER_BLOCK, LANES)),
            _resident((N_BLOCKS, STATE_PER_BLOCK, LANES)),
            _resident((1, D_B)),
            _resident((b * PERM_T, b * PERM_T)),
            _resident((b * PERM_T, b * PERM_T)),
            _resident((D_B, D_B)),
            _resident((1, D_B)),
        ],
        out_specs=[
            pl.BlockSpec((b, tb, D_B), lambda i: (0, i, 0)),
            pl.BlockSpec((b, S5_N), lambda i: (0, 0)),
            pl.BlockSpec((b, S5_N), lambda i: (0, 0)),
        ],
        out_shape=(jax.ShapeDtypeStruct((b, s_len, D_B), BF16), state, state),
        scratch_shapes=[
            pltpu.VMEM((rows, S5_N), F32),
            pltpu.VMEM((rows, S5_N), F32),
            pltpu.VMEM((b, S5_N), F32),
            pltpu.VMEM((b, S5_N), F32),
            pltpu.VMEM((rows, D_B), F32),
            pltpu.VMEM((rows, D_B), BF16),
            pltpu.VMEM((rows, D_B), F32),
        ],
        compiler_params=pltpu.CompilerParams(
            dimension_semantics=("arbitrary",), vmem_limit_bytes=VMEM_LIMIT),
        name="s5",
    )(u, s0re, s0im, lbre, lbim, wb, wcre, wcim, d_skip, perm, permt, wglu, bglu)


def _outproj_kernel(a_ref, b_ref, x_ref, wo_ref, gpost_ref, gffn_ref, *rest, n_sub, n_cast):
    cast_in = rest[:n_cast]
    x1_ref, hf_ref = rest[n_cast:n_cast + 2]
    cast_out = rest[n_cast + 2:]
    for src, dst in zip(cast_in, cast_out):
        dst[...] = src[...].astype(BF16)
    for rows in _row_blocks(a_ref.shape[0], n_sub):
        mix = (_dot(a_ref[rows, :], wo_ref[0:D_A, :])
               + _dot(b_ref[rows, :], wo_ref[D_A:D_A + D_B, :]))
        x1 = x_ref[rows, :] + _rms(mix, gpost_ref[...])
        x1_ref[rows, :] = x1
        hf_ref[rows, :] = _rms(x1, gffn_ref[...]).astype(BF16)


def _out_proj(a, b, x2d, wo, g_post, g_ffn, tm, n_sub, cast=()):
    t = x2d.shape[0]
    n_steps = t // tm
    tok = lambda i: (i, 0)
    cast_specs = [pl.BlockSpec((w.shape[0] // n_steps, w.shape[1]), tok) for w in cast]
    return pl.pallas_call(
        functools.partial(_outproj_kernel, n_sub=n_sub, n_cast=len(cast)),
        grid=(n_steps,),
        in_specs=[
            pl.BlockSpec((tm, D_A), tok),
            pl.BlockSpec((tm, D_B), tok),
            pl.BlockSpec((tm, D_MODEL), tok),
            _resident((D_A + D_B, D_MODEL)),
            _resident((1, D_MODEL)),
            _resident((1, D_MODEL)),
        ] + cast_specs,
        out_specs=[pl.BlockSpec((tm, D_MODEL), tok), pl.BlockSpec((tm, D_MODEL), tok)] + cast_specs,
        out_shape=(jax.ShapeDtypeStruct((t, D_MODEL), F32),
                   jax.ShapeDtypeStruct((t, D_MODEL), BF16))
        + tuple(jax.ShapeDtypeStruct(w.shape, BF16) for w in cast),
        compiler_params=pltpu.CompilerParams(
            dimension_semantics=("arbitrary",), vmem_limit_bytes=VMEM_LIMIT),
        name="out_proj",
    )(a, b, x2d, wo, g_post, g_ffn, *cast)


def _ffn_up_kernel(hfp_ref, hfs_ref, wg_ref, wu_ref, actp_ref, acts_ref, wgb, wub, *, n_sub):
    m = pl.program_id(1)

    @pl.when(m == 0)
    def _():
        wgb[...] = wg_ref[...].astype(BF16)
        wub[...] = wu_ref[...].astype(BF16)

    def body(hf_ref, act_ref, n_sub):
        for rows in _row_blocks(hf_ref.shape[0], n_sub):
            hf = hf_ref[rows, :]
            a = _dot(hf, wgb[...])
            act_ref[rows, :] = (a * _sigmoid(a) * _dot(hf, wub[...])).astype(BF16)

    @pl.when(m == 0)
    def _():
        body(hfs_ref, acts_ref, 1)

    @pl.when(m > 0)
    def _():
        body(hfp_ref, actp_ref, n_sub)


def _ffn_up(hf_p, hf_s, wg, wu, tm, tf, n_sub):
    t_p, t_s = hf_p.shape[0], hf_s.shape[0]
    n_p = t_p // tm
    n_f = D_FF // tf
    prompt_tile = lambda m: jnp.maximum(m - 1, 0)
    return pl.pallas_call(
        functools.partial(_ffn_up_kernel, n_sub=n_sub),
        grid=(n_f, n_p + 1),
        in_specs=[
            pl.BlockSpec((tm, D_MODEL), lambda f, m: (prompt_tile(m), 0)),
            pl.BlockSpec((t_s, D_MODEL), lambda f, m: (0, 0)),
            pl.BlockSpec((D_MODEL, tf), lambda f, m: (0, f)),
            pl.BlockSpec((D_MODEL, tf), lambda f, m: (0, f)),
        ],
        out_specs=[
            pl.BlockSpec((tm, tf), lambda f, m: (prompt_tile(m), f)),
            pl.BlockSpec((t_s, tf), lambda f, m: (0, f)),
        ],
        out_shape=(jax.ShapeDtypeStruct((t_p, D_FF), BF16),
                   jax.ShapeDtypeStruct((t_s, D_FF), BF16)),
        scratch_shapes=[pltpu.VMEM((D_MODEL, tf), BF16), pltpu.VMEM((D_MODEL, tf), BF16)],
        compiler_params=pltpu.CompilerParams(
            dimension_semantics=("arbitrary", "arbitrary"), vmem_limit_bytes=VMEM_LIMIT),
        name="ffn_up",
    )(hf_p, hf_s, wg, wu)


def _ffn_down_kernel(act_ref, wd_ref, x1_ref, gp_ref, out_ref, *, n_sub):
    for rows in _row_blocks(act_ref.shape[0], n_sub):
        ff = _dot(act_ref[rows, :], wd_ref[...])
        out_ref[rows, :] = x1_ref[rows, :] + _rms(ff, gp_ref[...])


def _ffn_down(act, wd, x1, g_post, tm, n_sub):
    t = act.shape[0]
    tok = lambda i: (i, 0)
    return pl.pallas_call(
        functools.partial(_ffn_down_kernel, n_sub=n_sub),
        grid=(t // tm,),
        in_specs=[
            pl.BlockSpec((tm, D_FF), tok),
            _resident((D_FF, D_MODEL)),
            pl.BlockSpec((tm, D_MODEL), tok),
            _resident((1, D_MODEL)),
        ],
        out_specs=pl.BlockSpec((tm, D_MODEL), tok),
        out_shape=jax.ShapeDtypeStruct((t, D_MODEL), F32),
        compiler_params=pltpu.CompilerParams(
            dimension_semantics=("arbitrary",), vmem_limit_bytes=VMEM_LIMIT),
        name="ffn_down",
    )(act, wd, x1, g_post)


def _block_diag_in(bbt):
    x = bbt.reshape(B_GROUP, N_BLOCKS, GROUPS_PER_BLOCK, B_STATE)
    x = jnp.transpose(x, (1, 2, 0, 3))
    eye = jnp.eye(GROUPS_PER_BLOCK, dtype=bool)[None, :, None, :, None]
    y = jnp.where(eye, x[:, :, :, None, :], 0.0)
    return y.reshape(N_BLOCKS, LANES, STATE_PER_BLOCK)


def _block_diag_out(c):
    x = c.reshape(N_BLOCKS, GROUPS_PER_BLOCK, B_GROUP, B_STATE)
    x = jnp.transpose(x, (0, 1, 3, 2))
    eye = jnp.eye(GROUPS_PER_BLOCK, dtype=bool)[None, :, None, :, None]
    y = jnp.where(eye, x[:, :, :, None, :], 0.0)
    return y.reshape(N_BLOCKS, STATE_PER_BLOCK, LANES)


def _row_permutation(batch):
    n = batch * PERM_T
    src = np.arange(n)
    b, t = src // PERM_T, src % PERM_T
    p = np.zeros((n, n), np.float32)
    p[t * batch + b, src] = 1.0
    return jnp.asarray(p, BF16), jnp.asarray(p.T, BF16)


def _pad_rows_front(x, rows):
    pad = [(0, 0)] * x.ndim
    pad[-2] = (rows - x.shape[-2], 0)
    return jnp.pad(x, pad)


def _pad_rows_back(x, rows):
    pad = [(0, 0)] * x.ndim
    pad[-2] = (0, rows - x.shape[-2])
    return jnp.pad(x, pad)


def _mix_layer(x, conv0, c0, n0, m0, sre0, sim0, p, *, tm, n_sub, chunk, tb):
    b, s_len, _ = x.shape
    t = b * s_len
    x2d = x.reshape(t, D_MODEL)
    assert s_len % chunk == 0 or s_len < chunk
    seg = min(s_len, chunk)
    if "w_glu" in p:
        qk, v, og, u, gates, gates_c = _in_proj(x2d, p, tm=tm, n_sub=n_sub, seg=seg)
    else:
        qk, v, og, u, gates, gates_c, p["w_glu"], p["w_out"] = _in_proj(
            x2d, p, tm=tm, n_sub=n_sub, seg=seg, cast=(p["w_glu_f32"], p["w_out_f32"]))

    s_pad = -(-s_len // chunk) * chunk
    extra = s_pad - s_len
    def seq3(a, mode="constant"):
        a = a.reshape(b, s_len, a.shape[-1])
        return a if extra == 0 else jnp.pad(a, ((0, 0), (0, extra), (0, 0)), mode=mode)
    g3 = gates.reshape(2 * SUBLANES, b, s_len)
    if extra:
        pad_t = ((0, 0), (0, 0), (0, extra))
        g3 = jnp.concatenate([
            jnp.pad(g3[:SUBLANES], pad_t, constant_values=-jnp.inf),
            jnp.pad(g3[SUBLANES:], pad_t, mode="edge")], axis=0)
    out_a, conv_o, c_o, n_o, m_o = _mlstm(
        seq3(qk), seq3(v), seq3(og), g3.reshape(2 * SUBLANES, b * s_pad),
        seq3(gates_c, "edge"), _pad_rows_front(conv0, SUBLANES), c0, n0,
        jnp.broadcast_to(_pad_rows_back(m0[..., None], SUBLANES), (b, SUBLANES, LANES)),
        p["w_conv"], p["b_conv"], p["g_mh"], chunk=chunk, valid=seg)
    out_a = out_a[:, :s_len].reshape(t, D_A)

    out_b, sre_o, sim_o = _s5(
        u.reshape(b, s_len, D_B), sre0.reshape(b, S5_N), sim0.reshape(b, S5_N),
        p["lbre"], p["lbim"], p["wb"], p["wcre"], p["wcim"], p["s5_d"],
        p["perm"], p["permt"], p["w_glu"], p["b_glu"], tb=tb)

    cast = () if "w_down" in p else (p["w_down_f32"],)
    x1, hf, *cast_out = _out_proj(out_a, out_b.reshape(t, D_B), x2d, p["w_out"],
                                  p["g_post_mix"], p["g_pre_ffn"], tm, 2 * n_sub, cast=cast)
    if cast_out:
        p["w_down"] = cast_out[0]
    states = (
        conv_o[:, SUBLANES - (CONV_W - 1):, :],
        c_o,
        n_o,
        m_o[:, :A_HEADS, 0],
        sre_o.reshape(b, B_GROUPS, B_STATE),
        sim_o.reshape(b, B_GROUPS, B_STATE),
    )
    return x1, hf, states


def kernel(x_prompt, x_sample, state_conv, state_mlstm_c, state_mlstm_n, state_mlstm_m, state_s5_re, state_s5_im, g_pre_mix, w_in, w_conv, b_conv, b_igate, b_fgate, g_mh, s5_lam_re, s5_lam_im, s5_log_dt, s5_b_re, s5_b_im, s5_c_re, s5_c_im, s5_d, w_glu, b_glu, w_out, g_post_mix, g_pre_ffn, w_gate, w_up, w_down, g_post_ffn):
    l = 0
    bp = x_prompt.shape[0]
    lbre, lbim, bbt_re, bbt_im = _discretise(
        s5_lam_re[l], s5_lam_im[l], s5_log_dt[l],
        jnp.transpose(s5_b_re[l], (2, 0, 1)), jnp.transpose(s5_b_im[l], (2, 0, 1)))

    o1 = QK_W
    o2 = o1 + D_A
    o3 = o2 + D_A
    o4 = o3 + A_HEADS
    o5 = o4 + A_HEADS
    wt = jnp.swapaxes(w_in[l], 0, 1)
    perm, permt = _row_permutation(bp)
    row = lambda a: a.reshape(1, -1)
    p = {
        "g_pre_mix": row(g_pre_mix[l]),
        "w_in_t": wt[:o3].astype(BF16),
        "w_in_gt": jnp.pad(wt[o3:o5], ((0, 2 * SUBLANES - 2 * A_HEADS), (0, 0))).astype(BF16),
        "w_in_ut": wt[o5:].astype(BF16),
        "gbias": jnp.concatenate(
            [b_igate[l], b_fgate[l], jnp.zeros((SUBLANES,), F32)]).reshape(2 * SUBLANES, 1),
        "w_conv": w_conv[l],
        "b_conv": row(b_conv[l]),
        "g_mh": row(g_mh[l]),
        "lbre": lbre.reshape(1, S5_N),
        "lbim": lbim.reshape(1, S5_N),
        "wb": jnp.concatenate([_block_diag_in(bbt_re), _block_diag_in(bbt_im)], axis=-1).astype(BF16),
        "wcre": _block_diag_out(s5_c_re[l]).astype(BF16),
        "wcim": _block_diag_out(s5_c_im[l]).astype(BF16),
        "s5_d": row(s5_d[l]),
        "perm": perm,
        "permt": permt,
        "w_glu_f32": w_glu[l],
        "b_glu": row(b_glu[l]),
        "w_out_f32": w_out[l],
        "w_down_f32": w_down[l],
        "g_post_mix": row(g_post_mix[l]),
        "g_pre_ffn": row(g_pre_ffn[l]),
    }

    x1_p, hf_p, st_p = _mix_layer(
        x_prompt,
        jnp.zeros((bp, CONV_W - 1, QK_W), F32),
        jnp.zeros((bp, A_HEADS, A_DV, A_DQK), F32),
        jnp.zeros((bp, A_HEADS, A_DQK), F32),
        jnp.zeros((bp, A_HEADS), F32),
        jnp.zeros((bp, B_GROUPS, B_STATE), F32),
        jnp.zeros((bp, B_GROUPS, B_STATE), F32),
        p, tm=512, n_sub=2, chunk=128, tb=64)
    x1_s, hf_s, st_s = _mix_layer(
        x_sample, state_conv[l], state_mlstm_c[l], state_mlstm_n[l], state_mlstm_m[l],
        state_s5_re[l], state_s5_im[l],
        p, tm=128, n_sub=1, chunk=128, tb=16)

    act_p, act_s = _ffn_up(hf_p, hf_s, w_gate[l], w_up[l], tm=2048, tf=512, n_sub=2)
    wd = p["w_down"]
    g_post = row(g_post_ffn[l])
    yp = _ffn_down(act_p, wd, x1_p, g_post, tm=512, n_sub=2).reshape(x_prompt.shape)
    ys = _ffn_down(act_s, wd, x1_s, g_post, tm=128, n_sub=1).reshape(x_sample.shape)
    return (yp, ys) + tuple(a[None] for a in st_p) + tuple(a[None] for a in st_s)
```

```python
import functools

import numpy as np
import jax
import jax.numpy as jnp
from jax import lax
from jax.experimental import pallas as pl
from jax.experimental.pallas import tpu as pltpu

F32 = jnp.float32
BF16 = jnp.bfloat16

D_MODEL = 2048
D_A = 1024
D_B = 1024
A_HEADS = 4
A_DV = 256
A_DQK = 128
QK_W = 1024
CONV_W = 4
B_GROUP = 16
B_GROUPS = 64
B_STATE = 64
S5_N = B_GROUPS * B_STATE
D_FF = 5632
EPS = 1e-6

SUBLANES = 8
LANES = 128
GROUPS_PER_BLOCK = LANES // B_GROUP
N_BLOCKS = B_GROUPS // GROUPS_PER_BLOCK
STATE_PER_BLOCK = GROUPS_PER_BLOCK * B_STATE
PERM_T = 16
VMEM_LIMIT = 56 * 1024 * 1024


def _sigmoid(x):
    return 1.0 / (1.0 + jnp.exp(-x))


def _gelu_tanh(x):
    c = np.sqrt(2 / np.pi).astype(np.float32)
    return x * (0.5 * (1.0 + jnp.tanh(c * (x + 0.044715 * (x ** 3)))))


def _rms(x, g):
    return x * lax.rsqrt(jnp.mean(x * x, axis=-1, keepdims=True) + EPS) * g


def _dot(a, b):
    return jnp.dot(a, b, preferred_element_type=F32)


def _log_sigmoid(z):
    return jnp.minimum(z, 0.0) - jnp.log1p(jnp.exp(-jnp.abs(z)))


def _split3(x):
    hi = x.astype(BF16)
    r = x - hi.astype(F32)
    mid = r.astype(BF16)
    lo = (r - mid.astype(F32)).astype(BF16)
    return hi, mid, lo


def _dot3_right(x, sel):
    hi, mid, lo = _split3(x)
    return _dot(hi, sel) + _dot(mid, sel) + _dot(lo, sel)


def _dot3_left(sel, x):
    hi, mid, lo = _split3(x)
    return _dot(sel, hi) + _dot(sel, mid) + _dot(sel, lo)


def _row_blocks(rows, n_sub):
    assert rows % n_sub == 0
    size = rows // n_sub
    return [slice(size * r, size * (r + 1)) for r in range(n_sub)]


def _resident(shape):
    zeros = (0,) * len(shape)
    return pl.BlockSpec(shape, lambda *_: zeros, pipeline_mode=pl.Buffered(1))


def _disc_kernel(lam_re_ref, lam_im_ref, logdt_ref, bre_ref, bim_ref,
                 lbre_ref, lbim_ref, bbre_ref, bbim_ref):
    lam_re = lam_re_ref[...]
    lam_im = lam_im_ref[...]
    dt = jnp.exp(logdt_ref[...])
    mag = jnp.exp(lam_re * dt)
    ang = lam_im * dt
    lb_re = mag * jnp.cos(ang)
    lb_im = mag * jnp.sin(ang)
    den = lam_re * lam_re + lam_im * lam_im
    nr = lb_re - 1.0
    f_re = (nr * lam_re + lb_im * lam_im) / den
    f_im = (lb_im * lam_re - nr * lam_im) / den
    lbre_ref[...] = lb_re
    lbim_ref[...] = lb_im
    b_re = bre_ref[...]
    b_im = bim_ref[...]
    bbre_ref[...] = f_re[None] * b_re - f_im[None] * b_im
    bbim_ref[...] = f_re[None] * b_im + f_im[None] * b_re


def _discretise(lam_re, lam_im, log_dt, bt_re, bt_im):
    gp = jax.ShapeDtypeStruct((B_GROUPS, B_STATE), F32)
    igp = jax.ShapeDtypeStruct((B_GROUP, B_GROUPS, B_STATE), F32)
    return pl.pallas_call(
        _disc_kernel, out_shape=(gp, gp, igp, igp), name="s5_discretise",
    )(lam_re, lam_im, log_dt.reshape(B_GROUPS, 1), bt_re, bt_im)


GATE_COL = QK_W + 2 * D_A
GATE_LANE_CMX = A_HEADS
GATE_LANE_FC = 2 * A_HEADS


def _dot_t(a, bt):
    return lax.dot_general(a, bt, (((1,), (1,)), ((), ())), preferred_element_type=F32)


def _inproj_tile(x_ref, gpre_ref, wt_ref, wut_ref, wgt_ref, gb_ref, triu_ref, tril_ref, outs,
                 n_sub, seg):
    qk_out, v_out, og_out, u_out, g_out, c_out = outs
    for rows in _row_blocks(x_ref.shape[0], n_sub):
        hm = _rms(x_ref[rows, :], gpre_ref[...]).astype(BF16)

        n = rows.stop - rows.start
        z = _dot_t(wgt_ref[...], hm) + gb_ref[...]
        qk_out[rows, :] = _dot_t(hm, wt_ref[0:QK_W, :])

        row = lax.broadcasted_iota(jnp.int32, z.shape, 0)
        zf = jnp.where((row >= A_HEADS) & (row < 2 * A_HEADS), _log_sigmoid(z), z)
        zf8 = zf[0:SUBLANES]
        zc = jnp.concatenate([zf8, jnp.zeros((LANES - SUBLANES, n), F32)], axis=0).T
        lane = lax.broadcasted_iota(jnp.int32, zc.shape, 1)
        lf_c = jnp.where((lane >= A_HEADS) & (lane < 2 * A_HEADS), zc, 0.0)
        fc = pltpu.roll(_dot3_right(zf, triu_ref[...])[0:SUBLANES], A_HEADS, axis=0)
        fc_c = _dot3_left(tril_ref[...], lf_c)
        v_out[rows, :] = _dot_t(hm, wt_ref[QK_W:QK_W + D_A, :]).astype(BF16)
        og_out[rows, :] = _dot_t(hm, wt_ref[QK_W + D_A:QK_W + 2 * D_A, :])
        u_out[rows, :] = _dot_t(hm, wut_ref[...])

        g_out[0:SUBLANES, rows] = zf8 - fc
        g_out[SUBLANES:2 * SUBLANES, rows] = fc
        cm = pltpu.roll(zc, A_HEADS, axis=1) - fc_c
        t_in = lax.broadcasted_iota(jnp.int32, zc.shape, 0) & (seg - 1)
        sh = 1
        while sh < seg:
            cm = jnp.where(t_in >= sh, jnp.maximum(cm, pltpu.roll(cm, sh, axis=0)), cm)
            sh *= 2
        c_out[rows, :] = jnp.where(lane < GATE_LANE_FC, cm, pltpu.roll(fc_c, A_HEADS, axis=1))


N_PROJ_OUT = 6


def _inproj_kernel(xp_ref, xs_ref, gpre_ref, wt_ref, wut_ref, wgt_ref, gb_ref,
                   triu_p, tril_p, triu_s, tril_s, *rest, cfg_p, cfg_s, n_cast, n_p):
    cast_in = rest[:n_cast]
    outs_p = rest[n_cast:n_cast + N_PROJ_OUT]
    outs_s = rest[n_cast + N_PROJ_OUT:n_cast + 2 * N_PROJ_OUT]
    cast_out = rest[n_cast + 2 * N_PROJ_OUT:]
    shared = (gpre_ref, wt_ref, wut_ref, wgt_ref, gb_ref)
    i = pl.program_id(0)

    @pl.when(i < n_p)
    def _():
        for src, dst in zip(cast_in, cast_out):
            dst[...] = src[...].astype(BF16)
        _inproj_tile(xp_ref, *shared, triu_p, tril_p, outs_p, *cfg_p)

    @pl.when(i == n_p)
    def _():
        _inproj_tile(xs_ref, *shared, triu_s, tril_s, outs_s, *cfg_s)


def _segment_prefix_matrices(n, seg):
    idx = np.arange(n)
    u = ((idx[:, None] // seg == idx[None, :] // seg) & (idx[:, None] <= idx[None, :]))
    u = u.astype(np.float32)
    return jnp.asarray(u, BF16), jnp.asarray(u.T, BF16)


def _in_proj(xp2d, xs2d, p, *, tm, n_sub, seg_p, seg_s, cast=()):
    t_p, t_s = xp2d.shape[0], xs2d.shape[0]
    n_p = t_p // tm
    assert seg_p & (seg_p - 1) == 0 and seg_s & (seg_s - 1) == 0
    triu_p, tril_p = _segment_prefix_matrices(tm // n_sub, seg_p)
    triu_s, tril_s = _segment_prefix_matrices(t_s, seg_s)
    tile = lambda i: jnp.minimum(i, n_p - 1)
    tok = lambda i: (tile(i), 0)
    once = lambda i: (0, 0)
    cast_specs = [pl.BlockSpec((w.shape[0] // n_p, w.shape[1]), tok) for w in cast]

    def out_specs(rows, row_map, gate_map):
        return [
            pl.BlockSpec((rows, QK_W), row_map),
            pl.BlockSpec((rows, D_A), row_map),
            pl.BlockSpec((rows, D_A), row_map),
            pl.BlockSpec((rows, D_B), row_map),
            pl.BlockSpec((2 * SUBLANES, rows), gate_map),
            pl.BlockSpec((rows, LANES), row_map),
        ]

    def out_shapes(t):
        return (
            jax.ShapeDtypeStruct((t, QK_W), F32),
            jax.ShapeDtypeStruct((t, D_A), BF16),
            jax.ShapeDtypeStruct((t, D_A), F32),
            jax.ShapeDtypeStruct((t, D_B), F32),
            jax.ShapeDtypeStruct((2 * SUBLANES, t), F32),
            jax.ShapeDtypeStruct((t, LANES), F32),
        )

    kernel = functools.partial(_inproj_kernel, cfg_p=(n_sub, seg_p), cfg_s=(1, seg_s),
                               n_cast=len(cast), n_p=n_p)
    return pl.pallas_call(
        kernel,
        grid=(n_p + 1,),
        in_specs=[
            pl.BlockSpec((tm, D_MODEL), tok),
            pl.BlockSpec((t_s, D_MODEL), once),
            _resident((1, D_MODEL)),
            _resident((GATE_COL, D_MODEL)),
            _resident((D_B, D_MODEL)),
            _resident((2 * SUBLANES, D_MODEL)),
            _resident((2 * SUBLANES, 1)),
            _resident(triu_p.shape),
            _resident(tril_p.shape),
            _resident(triu_s.shape),
            _resident(tril_s.shape),
        ] + cast_specs,
        out_specs=(out_specs(tm, tok, lambda i: (0, tile(i))) + out_specs(t_s, once, once)
                   + cast_specs),
        out_shape=(out_shapes(t_p) + out_shapes(t_s)
                   + tuple(jax.ShapeDtypeStruct(w.shape, BF16) for w in cast)),
        compiler_params=pltpu.CompilerParams(
            dimension_semantics=("arbitrary",), vmem_limit_bytes=VMEM_LIMIT),
        name="in_proj",
    )(xp2d, xs2d, p["g_pre_mix"], p["w_in_t"], p["w_in_ut"], p["w_in_gt"], p["gbias"],
      triu_p, tril_p, triu_s, tril_s, *cast)


MLSTM_STREAMS = 4


def _mlstm_kernel(*refs, chunk, valid, n_chunks):
    ns = MLSTM_STREAMS
    qk_ref, v_ref, og_ref = refs[0:3]
    g_refs = refs[3:3 + ns]
    (gc_ref, rep_ref, tail0_ref, c0_ref, n0_ref, m0_ref, wconv_ref, bconv_ref, gmh_ref,
     outa_ref, convo_ref, co_ref, no_ref, mo_ref, xp, ct, ncr, m_s) = refs[3 + ns:]
    L = chunk
    c = pl.program_id(1)
    reps = L // LANES

    @pl.when(c == 0)
    def _():
        for st in range(ns):
            xp[st, 0:SUBLANES, :] = tail0_ref[st]
            for h in range(A_HEADS):
                ct[st, h] = c0_ref[st, h].T
                ncr[st, h] = jnp.broadcast_to(n0_ref[st, h:h + 1, :], (A_DQK, A_DQK)).T
            m_s[st] = m0_ref[st]

    def lanes(x, n):
        return x if n == 1 else jnp.concatenate([x] * n, axis=1)

    row_i = lax.broadcasted_iota(jnp.int32, (L, L), 0)
    col_i = lax.broadcasted_iota(jnp.int32, (L, L), 1)
    tri = col_i <= row_i
    lane = lax.broadcasted_iota(jnp.int32, (SUBLANES, L), 1)

    for st in range(ns):
        xp[st, SUBLANES:SUBLANES + L, :] = qk_ref[st]
        xs = xp[st]
        assert CONV_W == 4
        x1 = pltpu.roll(xs, 1, axis=0)
        near = xs * wconv_ref[3:4, :] + x1 * wconv_ref[2:3, :]
        far = xs * wconv_ref[1:2, :] + x1 * wconv_ref[0:1, :]
        acc = (near + pltpu.roll(far, 2, axis=0))[SUBLANES:, :] + bconv_ref[...]
        qk = acc * _sigmoid(acc)
        q_all = (qk[:, :QK_W // 2] * (A_DQK ** -0.5)).astype(BF16)
        k_all = qk[:, QK_W // 2:].astype(BF16)

        rep = _dot3_right(gc_ref[st], rep_ref[...])

        g = g_refs[st][...]
        gr = g[0:SUBLANES]
        fc = g[SUBLANES:2 * SUBLANES]
        f_last = jnp.sum(jnp.where(lane == L - 1, fc, 0.0), axis=1, keepdims=True)
        m_prev = m_s[st]
        m0 = jnp.max(m_prev, axis=1, keepdims=True)
        lw = f_last + gr
        m_new = jnp.maximum(f_last + m0, jnp.max(lw, axis=1, keepdims=True))
        ws = jnp.exp(lw - m_new)
        decay = jnp.broadcast_to(jnp.exp(f_last + m0 - m_new), (SUBLANES, LANES))

        for h in range(A_HEADS):
            qb = q_all[:, A_DQK * h:A_DQK * (h + 1)]
            kb = k_all[:, A_DQK * h:A_DQK * (h + 1)]
            vb = v_ref[st, :, A_DV * h:A_DV * (h + 1)]
            m0_h = m_prev[h:h + 1, :]
            mx = jnp.maximum(rep[:, LANES * h:LANES * (h + 1)], m0_h)
            fc_t = rep[:, LANES * (A_HEADS + h):LANES * (A_HEADS + h + 1)]
            e = jnp.exp(jnp.where(tri, gr[h:h + 1, :] - lanes(mx, reps), -jnp.inf))
            s = lax.dot_general(qb, kb, (((1,), (1,)), ((), ())),
                                preferred_element_type=F32) * e
            g_int = jnp.exp(m0_h - mx)
            ct_h = ct[st, h]
            ncr_h = ncr[st, h]
            num = (_dot(s.astype(BF16), vb)
                   + lanes(g_int, A_DV // LANES) * _dot(qb, ct_h.astype(BF16)))
            den = jnp.sum(s, axis=1, keepdims=True) + g_int * _dot(qb, ncr_h.astype(BF16))
            scale = jnp.maximum(jnp.abs(den), jnp.exp(-(fc_t + mx)))
            hh = num / lanes(scale, A_DV // LANES)
            hn = _rms(hh, gmh_ref[:, A_DV * h:A_DV * (h + 1)])
            og = og_ref[st, :, A_DV * h:A_DV * (h + 1)]
            outa_ref[st, :, A_DV * h:A_DV * (h + 1)] = (hn * _sigmoid(og)).astype(BF16)
            kws = kb.astype(F32).T * ws[h:h + 1, :]
            dec_h = decay[h:h + 1, :]
            ct[st, h] = lanes(dec_h, A_DV // LANES) * ct_h + _dot(kws.astype(BF16), vb)
            ncr[st, h] = dec_h * ncr_h + jnp.sum(kws, axis=1, keepdims=True)
        m_s[st] = jnp.broadcast_to(m_new, (SUBLANES, LANES))

    @pl.when(c == n_chunks - 1)
    def _():
        for st in range(ns):
            convo_ref[st] = xp[st, valid:valid + SUBLANES, :]
            for h in range(A_HEADS):
                co_ref[st, h] = ct[st, h].T
                no_ref[st, h:h + 1, :] = ncr[st, h].T[0:1, :]
            mo_ref[st] = m_s[st]

    for st in range(ns):
        xp[st, 0:SUBLANES, :] = xp[st, L:L + SUBLANES, :]


def _gate_replication_matrix():
    r = np.zeros((LANES, 2 * A_HEADS * LANES), np.float32)
    for h in range(A_HEADS):
        r[GATE_LANE_CMX + h, LANES * h:LANES * (h + 1)] = 1.0
        r[GATE_LANE_FC + h, LANES * (A_HEADS + h):LANES * (A_HEADS + h + 1)] = 1.0
    return jnp.asarray(r, BF16)


def _mlstm(qk, v, og, gates, gates_c, tail0, c0, n0, m0, wconv, bconv, gmh, *, chunk, valid):
    b, s_len, _ = qk.shape
    ns = MLSTM_STREAMS
    assert b % ns == 0
    n_chunks = s_len // chunk
    rep = _gate_replication_matrix()
    seq = lambda i, c: (i, c, 0)
    per_b3 = lambda i, c: (i, 0, 0)
    per_b4 = lambda i, c: (i, 0, 0, 0)
    const2 = lambda i, c: (0, 0)
    gate_specs = [
        pl.BlockSpec((2 * SUBLANES, chunk), lambda i, c, st=st: (0, (ns * i + st) * n_chunks + c))
        for st in range(ns)]
    kernel = functools.partial(_mlstm_kernel, chunk=chunk, valid=valid, n_chunks=n_chunks)
    return pl.pallas_call(
        kernel,
        grid=(b // ns, n_chunks),
        in_specs=[
            pl.BlockSpec((ns, chunk, QK_W), seq),
            pl.BlockSpec((ns, chunk, D_A), seq),
            pl.BlockSpec((ns, chunk, D_A), seq),
        ] + gate_specs + [
            pl.BlockSpec((ns, chunk, LANES), seq),
            pl.BlockSpec(rep.shape, const2),
            pl.BlockSpec((ns, SUBLANES, QK_W), per_b3),
            pl.BlockSpec((ns, A_HEADS, A_DV, A_DQK), per_b4),
            pl.BlockSpec((ns, A_HEADS, A_DQK), per_b3),
            pl.BlockSpec((ns, SUBLANES, LANES), per_b3),
            pl.BlockSpec((CONV_W, QK_W), const2),
            pl.BlockSpec((1, QK_W), const2),
            pl.BlockSpec((1, D_A), const2),
        ],
        out_specs=[
            pl.BlockSpec((ns, chunk, D_A), seq),
            pl.BlockSpec((ns, SUBLANES, QK_W), per_b3),
            pl.BlockSpec((ns, A_HEADS, A_DV, A_DQK), per_b4),
            pl.BlockSpec((ns, A_HEADS, A_DQK), per_b3),
            pl.BlockSpec((ns, SUBLANES, LANES), per_b3),
        ],
        out_shape=(
            jax.ShapeDtypeStruct((b, s_len, D_A), BF16),
            jax.ShapeDtypeStruct((b, SUBLANES, QK_W), F32),
            jax.ShapeDtypeStruct((b, A_HEADS, A_DV, A_DQK), F32),
            jax.ShapeDtypeStruct((b, A_HEADS, A_DQK), F32),
            jax.ShapeDtypeStruct((b, SUBLANES, LANES), F32),
        ),
        scratch_shapes=[
            pltpu.VMEM((ns, chunk + SUBLANES, QK_W), F32),
            pltpu.VMEM((ns, A_HEADS, A_DQK, A_DV), F32),
            pltpu.VMEM((ns, A_HEADS, A_DQK, LANES), F32),
            pltpu.VMEM((ns, SUBLANES, LANES), F32),
        ],
        compiler_params=pltpu.CompilerParams(
            dimension_semantics=("arbitrary", "arbitrary"), vmem_limit_bytes=VMEM_LIMIT),
        name="mlstm",
    )(qk, v, og, *([gates] * ns), gates_c, rep, tail0, c0, n0, m0, wconv, bconv, gmh)


def _s5_kernel(u_ref, s0re_ref, s0im_ref, lbre_ref, lbim_ref, wb_ref, wcre_ref, wcim_ref,
               d_ref, perm_ref, permt_ref, wglu_ref, bglu_ref,
               outb_ref, sre_out, sim_out,
               bure, buim, sre, sim, utb, xtb, ytb, *, tb, n_steps):
    step = pl.program_id(0)
    batch = u_ref.shape[0]
    rows_j = batch * PERM_T
    n_j = tb // PERM_T

    @pl.when(step == 0)
    def _():
        sre[...] = s0re_ref[...]
        sim[...] = s0im_ref[...]

    strip = 8 * LANES
    blocks_per_strip = strip // STATE_PER_BLOCK
    n_strips = S5_N // strip

    def permute_in():
        for j in range(n_j):
            xj = jnp.concatenate(
                [u_ref[b, PERM_T * j:PERM_T * (j + 1), :] for b in range(batch)], axis=0)
            hi = xj.astype(BF16)
            lo = (xj - hi.astype(F32)).astype(BF16)
            uh = _dot(perm_ref[...], hi)
            rows = slice(rows_j * j, rows_j * (j + 1))
            xtb[rows, :] = uh.astype(BF16)
            utb[rows, :] = uh + _dot(perm_ref[...], lo)

    def input_map(st):
        for k in range(blocks_per_strip * st, blocks_per_strip * (st + 1)):
            r = _dot(xtb[:, LANES * k:LANES * (k + 1)], wb_ref[k])
            kc = slice(STATE_PER_BLOCK * k, STATE_PER_BLOCK * (k + 1))
            bure[:, kc] = r[:, :STATE_PER_BLOCK]
            buim[:, kc] = r[:, STATE_PER_BLOCK:]

    def recurrence(st):
        cols = slice(strip * st, strip * (st + 1))
        a_re = jnp.broadcast_to(lbre_ref[:, cols], (batch, strip))
        a_im = jnp.broadcast_to(lbim_ref[:, cols], (batch, strip))
        s_r = sre[:, cols]
        s_i = sim[:, cols]
        for t in range(tb):
            tr = slice(t * batch, (t + 1) * batch)
            n_r = a_re * s_r - a_im * s_i + bure[tr, cols]
            n_i = a_re * s_i + a_im * s_r + buim[tr, cols]
            bure[tr, cols] = n_r
            buim[tr, cols] = n_i
            s_r, s_i = n_r, n_i
        sre[:, cols] = s_r
        sim[:, cols] = s_i

    def output_map(st):
        for k in range(blocks_per_strip * st, blocks_per_strip * (st + 1)):
            kc = slice(STATE_PER_BLOCK * k, STATE_PER_BLOCK * (k + 1))
            ytb[:, LANES * k:LANES * (k + 1)] = (
                _dot(bure[:, kc].astype(BF16), wcre_ref[k])
                - _dot(buim[:, kc].astype(BF16), wcim_ref[k]))

    permute_in()
    for st in range(n_strips):
        input_map(st)
        recurrence(st)
        output_map(st)

    z = _gelu_tanh(ytb[...] + d_ref[...] * utb[...])
    gate = _sigmoid(_dot(z.astype(BF16), wglu_ref[...]) + bglu_ref[...])
    res = (z * gate).astype(BF16)
    for j in range(n_j):
        rb = _dot(permt_ref[...], res[rows_j * j:rows_j * (j + 1), :]).astype(BF16)
        for b in range(batch):
            outb_ref[b, PERM_T * j:PERM_T * (j + 1), :] = rb[PERM_T * b:PERM_T * (b + 1), :]

    @pl.when(step == n_steps - 1)
    def _():
        sre_out[...] = sre[...]
        sim_out[...] = sim[...]


def _s5(u, s0re, s0im, lbre, lbim, wb, wcre, wcim, d_skip, perm, permt, wglu, bglu, *, tb):
    b, s_len, _ = u.shape
    n_steps = s_len // tb
    rows = b * tb
    kernel = functools.partial(_s5_kernel, tb=tb, n_steps=n_steps)
    state = jax.ShapeDtypeStruct((b, S5_N), F32)
    return pl.pallas_call(
        kernel,
        grid=(n_steps,),
        in_specs=[
            pl.BlockSpec((b, tb, D_B), lambda i: (0, i, 0)),
            _resident((b, S5_N)),
            _resident((b, S5_N)),
            _resident((1, S5_N)),
            _resident((1, S5_N)),
            _resident((N_BLOCKS, LANES, 2 * STATE_PER_BLOCK)),
            _resident((N_BLOCKS, STATE_PER_BLOCK, LANES)),
            _resident((N_BLOCKS, STATE_PER_BLOCK, LANES)),
            _resident((1, D_B)),
            _resident((b * PERM_T, b * PERM_T)),
            _resident((b * PERM_T, b * PERM_T)),
            _resident((D_B, D_B)),
            _resident((1, D_B)),
        ],
        out_specs=[
            pl.BlockSpec((b, tb, D_B), lambda i: (0, i, 0)),
            pl.BlockSpec((b, S5_N), lambda i: (0, 0)),
            pl.BlockSpec((b, S5_N), lambda i: (0, 0)),
        ],
        out_shape=(jax.ShapeDtypeStruct((b, s_len, D_B), BF16), state, state),
        scratch_shapes=[
            pltpu.VMEM((rows, S5_N), F32),
            pltpu.VMEM((rows, S5_N), F32),
            pltpu.VMEM((b, S5_N), F32),
            pltpu.VMEM((b, S5_N), F32),
            pltpu.VMEM((rows, D_B), F32),
            pltpu.VMEM((rows, D_B), BF16),
            pltpu.VMEM((rows, D_B), F32),
        ],
        compiler_params=pltpu.CompilerParams(
            dimension_semantics=("arbitrary",), vmem_limit_bytes=VMEM_LIMIT),
        name="s5",
    )(u, s0re, s0im, lbre, lbim, wb, wcre, wcim, d_skip, perm, permt, wglu, bglu)


def _outproj_tile(a_ref, b_ref, x_ref, wo_ref, gpost_ref, gffn_ref, x1_ref, hf_ref, n_sub):
    for rows in _row_blocks(a_ref.shape[0], n_sub):
        mix = (_dot(a_ref[rows, :], wo_ref[0:D_A, :])
               + _dot(b_ref[rows, :], wo_ref[D_A:D_A + D_B, :]))
        x1 = x_ref[rows, :] + _rms(mix, gpost_ref[...])
        x1_ref[rows, :] = x1
        hf_ref[rows, :] = _rms(x1, gffn_ref[...]).astype(BF16)


def _outproj_kernel(ap_ref, bp_ref, xp_ref, as_ref, bs_ref, xs_ref, wo_ref, gpost_ref, gffn_ref,
                    *rest, n_sub, n_cast, n_p):
    cast_in = rest[:n_cast]
    x1p_ref, hfp_ref, x1s_ref, hfs_ref = rest[n_cast:n_cast + 4]
    cast_out = rest[n_cast + 4:]
    shared = (wo_ref, gpost_ref, gffn_ref)
    i = pl.program_id(0)

    @pl.when(i < n_p)
    def _():
        for src, dst in zip(cast_in, cast_out):
            dst[...] = src[...].astype(BF16)
        _outproj_tile(ap_ref, bp_ref, xp_ref, *shared, x1p_ref, hfp_ref, n_sub)

    @pl.when(i == n_p)
    def _():
        _outproj_tile(as_ref, bs_ref, xs_ref, *shared, x1s_ref, hfs_ref, 1)


def _out_proj(prompt, sample, wo, g_post, g_ffn, tm, n_sub, cast=()):
    t_p, t_s = prompt[2].shape[0], sample[2].shape[0]
    n_p = t_p // tm
    tok = lambda i: (jnp.minimum(i, n_p - 1), 0)
    once = lambda i: (0, 0)
    cast_specs = [pl.BlockSpec((w.shape[0] // n_p, w.shape[1]), tok) for w in cast]
    rows_specs = lambda rows, m: [pl.BlockSpec((rows, D_A), m), pl.BlockSpec((rows, D_B), m),
                                  pl.BlockSpec((rows, D_MODEL), m)]
    out_specs = lambda rows, m: [pl.BlockSpec((rows, D_MODEL), m), pl.BlockSpec((rows, D_MODEL), m)]
    out_shapes = lambda t: (jax.ShapeDtypeStruct((t, D_MODEL), F32),
                            jax.ShapeDtypeStruct((t, D_MODEL), BF16))
    return pl.pallas_call(
        functools.partial(_outproj_kernel, n_sub=n_sub, n_cast=len(cast), n_p=n_p),
        grid=(n_p + 1,),
        in_specs=rows_specs(tm, tok) + rows_specs(t_s, once) + [
            _resident((D_A + D_B, D_MODEL)),
            _resident((1, D_MODEL)),
            _resident((1, D_MODEL)),
        ] + cast_specs,
        out_specs=out_specs(tm, tok) + out_specs(t_s, once) + cast_specs,
        out_shape=(out_shapes(t_p) + out_shapes(t_s)
                   + tuple(jax.ShapeDtypeStruct(w.shape, BF16) for w in cast)),
        compiler_params=pltpu.CompilerParams(
            dimension_semantics=("arbitrary",), vmem_limit_bytes=VMEM_LIMIT),
        name="out_proj",
    )(*prompt, *sample, wo, g_post, g_ffn, *cast)


def _ffn_up_kernel(hfp_ref, hfs_ref, wg_ref, wu_ref, actp_ref, acts_ref, wgb, wub, *, n_sub):
    m = pl.program_id(1)

    @pl.when(m == 0)
    def _():
        wgb[...] = wg_ref[...].astype(BF16)
        wub[...] = wu_ref[...].astype(BF16)

    def body(hf_ref, act_ref, n_sub):
        for rows in _row_blocks(hf_ref.shape[0], n_sub):
            hf = hf_ref[rows, :]
            a = _dot(hf, wgb[...])
            act_ref[rows, :] = (a * _sigmoid(a) * _dot(hf, wub[...])).astype(BF16)

    @pl.when(m == 0)
    def _():
        body(hfs_ref, acts_ref, 1)

    @pl.when(m > 0)
    def _():
        body(hfp_ref, actp_ref, n_sub)


def _ffn_up(hf_p, hf_s, wg, wu, tm, tf, n_sub):
    t_p, t_s = hf_p.shape[0], hf_s.shape[0]
    n_p = t_p // tm
    n_f = D_FF // tf
    prompt_tile = lambda m: jnp.maximum(m - 1, 0)
    return pl.pallas_call(
        functools.partial(_ffn_up_kernel, n_sub=n_sub),
        grid=(n_f, n_p + 1),
        in_specs=[
            pl.BlockSpec((tm, D_MODEL), lambda f, m: (prompt_tile(m), 0)),
            pl.BlockSpec((t_s, D_MODEL), lambda f, m: (0, 0)),
            pl.BlockSpec((D_MODEL, tf), lambda f, m: (0, f)),
            pl.BlockSpec((D_MODEL, tf), lambda f, m: (0, f)),
        ],
        out_specs=[
            pl.BlockSpec((tm, tf), lambda f, m: (prompt_tile(m), f)),
            pl.BlockSpec((t_s, tf), lambda f, m: (0, f)),
        ],
        out_shape=(jax.ShapeDtypeStruct((t_p, D_FF), BF16),
                   jax.ShapeDtypeStruct((t_s, D_FF), BF16)),
        scratch_shapes=[pltpu.VMEM((D_MODEL, tf), BF16), pltpu.VMEM((D_MODEL, tf), BF16)],
        compiler_params=pltpu.CompilerParams(
            dimension_semantics=("arbitrary", "arbitrary"), vmem_limit_bytes=VMEM_LIMIT),
        name="ffn_up",
    )(hf_p, hf_s, wg, wu)


def _ffn_down_kernel(act_ref, wd_ref, x1_ref, gp_ref, out_ref, *, n_sub):
    for rows in _row_blocks(act_ref.shape[0], n_sub):
        ff = _dot(act_ref[rows, :], wd_ref[...])
        out_ref[rows, :] = x1_ref[rows, :] + _rms(ff, gp_ref[...])


def _ffn_down(act, wd, x1, g_post, tm, n_sub):
    t = act.shape[0]
    tok = lambda i: (i, 0)
    return pl.pallas_call(
        functools.partial(_ffn_down_kernel, n_sub=n_sub),
        grid=(t // tm,),
        in_specs=[
            pl.BlockSpec((tm, D_FF), tok),
            _resident((D_FF, D_MODEL)),
            pl.BlockSpec((tm, D_MODEL), tok),
            _resident((1, D_MODEL)),
        ],
        out_specs=pl.BlockSpec((tm, D_MODEL), tok),
        out_shape=jax.ShapeDtypeStruct((t, D_MODEL), F32),
        compiler_params=pltpu.CompilerParams(
            dimension_semantics=("arbitrary",), vmem_limit_bytes=VMEM_LIMIT),
        name="ffn_down",
    )(act, wd, x1, g_post)


def _block_diag_in(bbt):
    x = bbt.reshape(B_GROUP, N_BLOCKS, GROUPS_PER_BLOCK, B_STATE)
    x = jnp.transpose(x, (1, 2, 0, 3))
    eye = jnp.eye(GROUPS_PER_BLOCK, dtype=bool)[None, :, None, :, None]
    y = jnp.where(eye, x[:, :, :, None, :], 0.0)
    return y.reshape(N_BLOCKS, LANES, STATE_PER_BLOCK)


def _block_diag_out(c):
    x = c.reshape(N_BLOCKS, GROUPS_PER_BLOCK, B_GROUP, B_STATE)
    x = jnp.transpose(x, (0, 1, 3, 2))
    eye = jnp.eye(GROUPS_PER_BLOCK, dtype=bool)[None, :, None, :, None]
    y = jnp.where(eye, x[:, :, :, None, :], 0.0)
    return y.reshape(N_BLOCKS, STATE_PER_BLOCK, LANES)


def _row_permutation(batch):
    n = batch * PERM_T
    src = np.arange(n)
    b, t = src // PERM_T, src % PERM_T
    p = np.zeros((n, n), np.float32)
    p[t * batch + b, src] = 1.0
    return jnp.asarray(p, BF16), jnp.asarray(p.T, BF16)


def _pad_rows_front(x, rows):
    pad = [(0, 0)] * x.ndim
    pad[-2] = (rows - x.shape[-2], 0)
    return jnp.pad(x, pad)


def _pad_rows_back(x, rows):
    pad = [(0, 0)] * x.ndim
    pad[-2] = (0, rows - x.shape[-2])
    return jnp.pad(x, pad)


def _mixers(proj, conv0, c0, n0, m0, sre0, sim0, p, *, b, chunk, tb):
    qk, v, og, u, gates, gates_c = proj
    t = qk.shape[0]
    s_len = t // b
    assert s_len % chunk == 0 or s_len < chunk
    seg = min(s_len, chunk)
    s_pad = -(-s_len // chunk) * chunk
    extra = s_pad - s_len
    def seq3(a, mode="constant"):
        a = a.reshape(b, s_len, a.shape[-1])
        return a if extra == 0 else jnp.pad(a, ((0, 0), (0, extra), (0, 0)), mode=mode)
    g3 = gates.reshape(2 * SUBLANES, b, s_len)
    if extra:
        pad_t = ((0, 0), (0, 0), (0, extra))
        g3 = jnp.concatenate([
            jnp.pad(g3[:SUBLANES], pad_t, constant_values=-jnp.inf),
            jnp.pad(g3[SUBLANES:], pad_t, mode="edge")], axis=0)
    out_a, conv_o, c_o, n_o, m_o = _mlstm(
        seq3(qk), seq3(v), seq3(og), g3.reshape(2 * SUBLANES, b * s_pad),
        seq3(gates_c, "edge"), _pad_rows_front(conv0, SUBLANES), c0, n0,
        jnp.broadcast_to(_pad_rows_back(m0[..., None], SUBLANES), (b, SUBLANES, LANES)),
        p["w_conv"], p["b_conv"], p["g_mh"], chunk=chunk, valid=seg)
    out_a = out_a[:, :s_len].reshape(t, D_A)

    out_b, sre_o, sim_o = _s5(
        u.reshape(b, s_len, D_B), sre0.reshape(b, S5_N), sim0.reshape(b, S5_N),
        p["lbre"], p["lbim"], p["wb"], p["wcre"], p["wcim"], p["s5_d"],
        p["perm"], p["permt"], p["w_glu"], p["b_glu"], tb=tb)
    states = (
        conv_o[:, SUBLANES - (CONV_W - 1):, :],
        c_o,
        n_o,
        m_o[:, :A_HEADS, 0],
        sre_o.reshape(b, B_GROUPS, B_STATE),
        sim_o.reshape(b, B_GROUPS, B_STATE),
    )
    return out_a, out_b.reshape(t, D_B), states


def kernel(x_prompt, x_sample, state_conv, state_mlstm_c, state_mlstm_n, state_mlstm_m, state_s5_re, state_s5_im, g_pre_mix, w_in, w_conv, b_conv, b_igate, b_fgate, g_mh, s5_lam_re, s5_lam_im, s5_log_dt, s5_b_re, s5_b_im, s5_c_re, s5_c_im, s5_d, w_glu, b_glu, w_out, g_post_mix, g_pre_ffn, w_gate, w_up, w_down, g_post_ffn):
    l = 0
    bp = x_prompt.shape[0]
    lbre, lbim, bbt_re, bbt_im = _discretise(
        s5_lam_re[l], s5_lam_im[l], s5_log_dt[l],
        jnp.transpose(s5_b_re[l], (2, 0, 1)), jnp.transpose(s5_b_im[l], (2, 0, 1)))

    o1 = QK_W
    o2 = o1 + D_A
    o3 = o2 + D_A
    o4 = o3 + A_HEADS
    o5 = o4 + A_HEADS
    wt = jnp.swapaxes(w_in[l], 0, 1)
    perm, permt = _row_permutation(bp)
    row = lambda a: a.reshape(1, -1)
    p = {
        "g_pre_mix": row(g_pre_mix[l]),
        "w_in_t": wt[:o3].astype(BF16),
        "w_in_gt": jnp.pad(wt[o3:o5], ((0, 2 * SUBLANES - 2 * A_HEADS), (0, 0))).astype(BF16),
        "w_in_ut": wt[o5:].astype(BF16),
        "gbias": jnp.concatenate(
            [b_igate[l], b_fgate[l], jnp.zeros((SUBLANES,), F32)]).reshape(2 * SUBLANES, 1),
        "w_conv": w_conv[l],
        "b_conv": row(b_conv[l]),
        "g_mh": row(g_mh[l]),
        "lbre": lbre.reshape(1, S5_N),
        "lbim": lbim.reshape(1, S5_N),
        "wb": jnp.concatenate([_block_diag_in(bbt_re), _block_diag_in(bbt_im)], axis=-1).astype(BF16),
        "wcre": _block_diag_out(s5_c_re[l]).astype(BF16),
        "wcim": _block_diag_out(s5_c_im[l]).astype(BF16),
        "s5_d": row(s5_d[l]),
        "perm": perm,
        "permt": permt,
        "b_glu": row(b_glu[l]),
        "g_post_mix": row(g_post_mix[l]),
        "g_pre_ffn": row(g_pre_ffn[l]),
    }

    bs, s_sample = x_sample.shape[0], x_sample.shape[1]
    xp2d = x_prompt.reshape(-1, D_MODEL)
    xs2d = x_sample.reshape(-1, D_MODEL)
    chunk = 128
    proj = _in_proj(xp2d, xs2d, p, tm=512, n_sub=2, seg_p=chunk, seg_s=min(s_sample, chunk),
                    cast=(w_glu[l], w_out[l]))
    proj_p, proj_s = proj[:N_PROJ_OUT], proj[N_PROJ_OUT:2 * N_PROJ_OUT]
    p["w_glu"], p["w_out"] = proj[2 * N_PROJ_OUT:]

    a_p, b_p, st_p = _mixers(
        proj_p,
        jnp.zeros((bp, CONV_W - 1, QK_W), F32),
        jnp.zeros((bp, A_HEADS, A_DV, A_DQK), F32),
        jnp.zeros((bp, A_HEADS, A_DQK), F32),
        jnp.zeros((bp, A_HEADS), F32),
        jnp.zeros((bp, B_GROUPS, B_STATE), F32),
        jnp.zeros((bp, B_GROUPS, B_STATE), F32),
        p, b=bp, chunk=chunk, tb=64)
    a_s, b_s, st_s = _mixers(
        proj_s, state_conv[l], state_mlstm_c[l], state_mlstm_n[l], state_mlstm_m[l],
        state_s5_re[l], state_s5_im[l], p, b=bs, chunk=chunk, tb=s_sample)

    x1_p, hf_p, x1_s, hf_s, wd = _out_proj(
        (a_p, b_p, xp2d), (a_s, b_s, xs2d), p["w_out"], p["g_post_mix"], p["g_pre_ffn"],
        tm=512, n_sub=4, cast=(w_down[l],))

    act_p, act_s = _ffn_up(hf_p, hf_s, w_gate[l], w_up[l], tm=2048, tf=512, n_sub=2)
    g_post = row(g_post_ffn[l])
    yp = _ffn_down(act_p, wd, x1_p, g_post, tm=512, n_sub=2).reshape(x_prompt.shape)
    ys = _ffn_down(act_s, wd, x1_s, g_post, tm=128, n_sub=1).reshape(x_sample.shape)
    return (yp, ys) + tuple(a[None] for a in st_p) + tuple(a[None] for a in st_s)
```

```python
import functools

import numpy as np
import jax
import jax.numpy as jnp
from jax import lax
from jax.experimental import pallas as pl
from jax.experimental.pallas import tpu as pltpu

F32 = jnp.float32
BF16 = jnp.bfloat16

D_MODEL = 2048
D_A = 1024
D_B = 1024
A_HEADS = 4
A_DV = 256
A_DQK = 128
QK_W = 1024
CONV_W = 4
B_GROUP = 16
B_GROUPS = 64
B_STATE = 64
S5_N = B_GROUPS * B_STATE
D_FF = 5632
EPS = 1e-6

SUBLANES = 8
LANES = 128
GROUPS_PER_BLOCK = LANES // B_GROUP
N_BLOCKS = B_GROUPS // GROUPS_PER_BLOCK
STATE_PER_BLOCK = GROUPS_PER_BLOCK * B_STATE
PERM_T = 16
VMEM_LIMIT = 56 * 1024 * 1024


def _sigmoid(x):
    return 1.0 / (1.0 + jnp.exp(-x))


def _gelu_tanh(x):
    c = np.sqrt(2 / np.pi).astype(np.float32)
    return x * (0.5 * (1.0 + jnp.tanh(c * (x + 0.044715 * (x ** 3)))))


def _rms(x, g):
    return x * lax.rsqrt(jnp.mean(x * x, axis=-1, keepdims=True) + EPS) * g


def _dot(a, b):
    return jnp.dot(a, b, preferred_element_type=F32)


def _log_sigmoid(z):
    return jnp.minimum(z, 0.0) - jnp.log1p(jnp.exp(-jnp.abs(z)))


def _split3(x):
    hi = x.astype(BF16)
    r = x - hi.astype(F32)
    mid = r.astype(BF16)
    lo = (r - mid.astype(F32)).astype(BF16)
    return hi, mid, lo


def _dot3_right(x, sel):
    hi, mid, lo = _split3(x)
    return _dot(hi, sel) + _dot(mid, sel) + _dot(lo, sel)


def _dot3_left(sel, x):
    hi, mid, lo = _split3(x)
    return _dot(sel, hi) + _dot(sel, mid) + _dot(sel, lo)


def _row_blocks(rows, n_sub):
    assert rows % n_sub == 0
    size = rows // n_sub
    return [slice(size * r, size * (r + 1)) for r in range(n_sub)]


def _resident(shape):
    zeros = (0,) * len(shape)
    return pl.BlockSpec(shape, lambda *_: zeros, pipeline_mode=pl.Buffered(1))


def _disc_kernel(lam_re_ref, lam_im_ref, logdt_ref, bre_ref, bim_ref,
                 lbre_ref, lbim_ref, bbre_ref, bbim_ref):
    lam_re = lam_re_ref[...]
    lam_im = lam_im_ref[...]
    dt = jnp.exp(logdt_ref[...])
    mag = jnp.exp(lam_re * dt)
    ang = lam_im * dt
    lb_re = mag * jnp.cos(ang)
    lb_im = mag * jnp.sin(ang)
    den = lam_re * lam_re + lam_im * lam_im
    nr = lb_re - 1.0
    f_re = (nr * lam_re + lb_im * lam_im) / den
    f_im = (lb_im * lam_re - nr * lam_im) / den
    lbre_ref[...] = lb_re
    lbim_ref[...] = lb_im
    b_re = bre_ref[...]
    b_im = bim_ref[...]
    bbre_ref[...] = f_re[None] * b_re - f_im[None] * b_im
    bbim_ref[...] = f_re[None] * b_im + f_im[None] * b_re


def _discretise(lam_re, lam_im, log_dt, bt_re, bt_im):
    gp = jax.ShapeDtypeStruct((B_GROUPS, B_STATE), F32)
    igp = jax.ShapeDtypeStruct((B_GROUP, B_GROUPS, B_STATE), F32)
    return pl.pallas_call(
        _disc_kernel, out_shape=(gp, gp, igp, igp), name="s5_discretise",
    )(lam_re, lam_im, log_dt.reshape(B_GROUPS, 1), bt_re, bt_im)


GATE_COL = QK_W + 2 * D_A
GATE_LANE_CMX = A_HEADS
GATE_LANE_FC = 2 * A_HEADS


def _dot_t(a, bt):
    return lax.dot_general(a, bt, (((1,), (1,)), ((), ())), preferred_element_type=F32)


def _inproj_tile(x_ref, gpre_ref, wt_ref, wut_ref, wgt_ref, gb_ref, triu_ref, tril_ref, outs,
                 n_sub, seg):
    qk_out, v_out, og_out, u_out, g_out, c_out = outs
    for rows in _row_blocks(x_ref.shape[0], n_sub):
        hm = _rms(x_ref[rows, :], gpre_ref[...]).astype(BF16)

        n = rows.stop - rows.start
        z = _dot_t(wgt_ref[...], hm) + gb_ref[...]
        qk_out[rows, :] = _dot_t(hm, wt_ref[0:QK_W, :])

        row = lax.broadcasted_iota(jnp.int32, z.shape, 0)
        zf = jnp.where((row >= A_HEADS) & (row < 2 * A_HEADS), _log_sigmoid(z), z)
        zf8 = zf[0:SUBLANES]
        zc = jnp.concatenate([zf8, jnp.zeros((LANES - SUBLANES, n), F32)], axis=0).T
        lane = lax.broadcasted_iota(jnp.int32, zc.shape, 1)
        lf_c = jnp.where((lane >= A_HEADS) & (lane < 2 * A_HEADS), zc, 0.0)
        fc = pltpu.roll(_dot3_right(zf, triu_ref[...])[0:SUBLANES], A_HEADS, axis=0)
        fc_c = _dot3_left(tril_ref[...], lf_c)
        v_out[rows, :] = _dot_t(hm, wt_ref[QK_W:QK_W + D_A, :]).astype(BF16)
        og_out[rows, :] = _dot_t(hm, wt_ref[QK_W + D_A:QK_W + 2 * D_A, :])
        u_out[rows, :] = _dot_t(hm, wut_ref[...])

        g_out[0:SUBLANES, rows] = zf8 - fc
        g_out[SUBLANES:2 * SUBLANES, rows] = fc
        cm = pltpu.roll(zc, A_HEADS, axis=1) - fc_c
        t_in = lax.broadcasted_iota(jnp.int32, zc.shape, 0) & (seg - 1)
        sh = 1
        while sh < seg:
            cm = jnp.where(t_in >= sh, jnp.maximum(cm, pltpu.roll(cm, sh, axis=0)), cm)
            sh *= 2
        c_out[rows, :] = jnp.where(lane < GATE_LANE_FC, cm, pltpu.roll(fc_c, A_HEADS, axis=1))


N_PROJ_OUT = 6


def _inproj_kernel(xp_ref, xs_ref, gpre_ref, wt_ref, wut_ref, wgt_ref, gb_ref,
                   triu_p, tril_p, triu_s, tril_s, *rest, cfg_p, cfg_s, n_cast, n_p):
    cast_in = rest[:n_cast]
    outs_p = rest[n_cast:n_cast + N_PROJ_OUT]
    outs_s = rest[n_cast + N_PROJ_OUT:n_cast + 2 * N_PROJ_OUT]
    cast_out = rest[n_cast + 2 * N_PROJ_OUT:]
    shared = (gpre_ref, wt_ref, wut_ref, wgt_ref, gb_ref)
    i = pl.program_id(0)

    @pl.when(i < n_p)
    def _():
        for src, dst in zip(cast_in, cast_out):
            dst[...] = src[...].astype(BF16)
        _inproj_tile(xp_ref, *shared, triu_p, tril_p, outs_p, *cfg_p)

    @pl.when(i == n_p)
    def _():
        _inproj_tile(xs_ref, *shared, triu_s, tril_s, outs_s, *cfg_s)


def _segment_prefix_matrices(n, seg):
    idx = np.arange(n)
    u = ((idx[:, None] // seg == idx[None, :] // seg) & (idx[:, None] <= idx[None, :]))
    u = u.astype(np.float32)
    return jnp.asarray(u, BF16), jnp.asarray(u.T, BF16)


def _in_proj(xp2d, xs2d, p, *, tm, n_sub, seg_p, seg_s, cast=()):
    t_p, t_s = xp2d.shape[0], xs2d.shape[0]
    n_p = t_p // tm
    assert seg_p & (seg_p - 1) == 0 and seg_s & (seg_s - 1) == 0
    triu_p, tril_p = _segment_prefix_matrices(tm // n_sub, seg_p)
    triu_s, tril_s = _segment_prefix_matrices(t_s, seg_s)
    tile = lambda i: jnp.minimum(i, n_p - 1)
    tok = lambda i: (tile(i), 0)
    once = lambda i: (0, 0)
    cast_specs = [pl.BlockSpec((w.shape[0] // n_p, w.shape[1]), tok) for w in cast]

    def out_specs(rows, row_map, gate_map):
        return [
            pl.BlockSpec((rows, QK_W), row_map),
            pl.BlockSpec((rows, D_A), row_map),
            pl.BlockSpec((rows, D_A), row_map),
            pl.BlockSpec((rows, D_B), row_map),
            pl.BlockSpec((2 * SUBLANES, rows), gate_map),
            pl.BlockSpec((rows, LANES), row_map),
        ]

    def out_shapes(t):
        return (
            jax.ShapeDtypeStruct((t, QK_W), F32),
            jax.ShapeDtypeStruct((t, D_A), BF16),
            jax.ShapeDtypeStruct((t, D_A), F32),
            jax.ShapeDtypeStruct((t, D_B), F32),
            jax.ShapeDtypeStruct((2 * SUBLANES, t), F32),
            jax.ShapeDtypeStruct((t, LANES), F32),
        )

    kernel = functools.partial(_inproj_kernel, cfg_p=(n_sub, seg_p), cfg_s=(1, seg_s),
                               n_cast=len(cast), n_p=n_p)
    return pl.pallas_call(
        kernel,
        grid=(n_p + 1,),
        in_specs=[
            pl.BlockSpec((tm, D_MODEL), tok),
            pl.BlockSpec((t_s, D_MODEL), once),
            _resident((1, D_MODEL)),
            _resident((GATE_COL, D_MODEL)),
            _resident((D_B, D_MODEL)),
            _resident((2 * SUBLANES, D_MODEL)),
            _resident((2 * SUBLANES, 1)),
            _resident(triu_p.shape),
            _resident(tril_p.shape),
            _resident(triu_s.shape),
            _resident(tril_s.shape),
        ] + cast_specs,
        out_specs=(out_specs(tm, tok, lambda i: (0, tile(i))) + out_specs(t_s, once, once)
                   + cast_specs),
        out_shape=(out_shapes(t_p) + out_shapes(t_s)
                   + tuple(jax.ShapeDtypeStruct(w.shape, BF16) for w in cast)),
        compiler_params=pltpu.CompilerParams(
            dimension_semantics=("arbitrary",), vmem_limit_bytes=VMEM_LIMIT),
        name="in_proj",
    )(xp2d, xs2d, p["g_pre_mix"], p["w_in_t"], p["w_in_ut"], p["w_in_gt"], p["gbias"],
      triu_p, tril_p, triu_s, tril_s, *cast)


MLSTM_STREAMS = 2


def _mlstm_kernel(*refs, chunk, valid, n_chunks):
    ns = MLSTM_STREAMS
    qk_ref, v_ref, og_ref = refs[0:3]
    g_refs = refs[3:3 + ns]
    (gc_ref, rep_ref, tail0_ref, c0_ref, n0_ref, m0_ref, wconv_ref, bconv_ref, gmh_ref,
     outa_ref, convo_ref, co_ref, no_ref, mo_ref, xp, ct, ncr, m_s) = refs[3 + ns:]
    L = chunk
    c = pl.program_id(1)
    reps = L // LANES

    @pl.when(c == 0)
    def _():
        for st in range(ns):
            xp[st, 0:SUBLANES, :] = tail0_ref[st]
            for h in range(A_HEADS):
                ct[st, h] = c0_ref[st, h].T
                ncr[st, h] = jnp.broadcast_to(n0_ref[st, h:h + 1, :], (A_DQK, A_DQK)).T
            m_s[st] = m0_ref[st]

    def lanes(x, n):
        return x if n == 1 else jnp.concatenate([x] * n, axis=1)

    row_i = lax.broadcasted_iota(jnp.int32, (L, L), 0)
    col_i = lax.broadcasted_iota(jnp.int32, (L, L), 1)
    tri = col_i <= row_i
    lane = lax.broadcasted_iota(jnp.int32, (SUBLANES, L), 1)

    for st in range(ns):
        xp[st, SUBLANES:SUBLANES + L, :] = qk_ref[st]
        xs = xp[st]
        assert CONV_W == 4
        x1 = pltpu.roll(xs, 1, axis=0)
        near = xs * wconv_ref[3:4, :] + x1 * wconv_ref[2:3, :]
        far = xs * wconv_ref[1:2, :] + x1 * wconv_ref[0:1, :]
        acc = (near + pltpu.roll(far, 2, axis=0))[SUBLANES:, :] + bconv_ref[...]
        qk = acc * _sigmoid(acc)
        q_all = (qk[:, :QK_W // 2] * (A_DQK ** -0.5)).astype(BF16)
        k_all = qk[:, QK_W // 2:].astype(BF16)

        rep = _dot3_right(gc_ref[st], rep_ref[...])

        g = g_refs[st][...]
        gr = g[0:SUBLANES]
        fc = g[SUBLANES:2 * SUBLANES]
        f_last = jnp.sum(jnp.where(lane == L - 1, fc, 0.0), axis=1, keepdims=True)
        m_prev = m_s[st]
        m0 = jnp.max(m_prev, axis=1, keepdims=True)
        lw = f_last + gr
        m_new = jnp.maximum(f_last + m0, jnp.max(lw, axis=1, keepdims=True))
        ws = jnp.exp(lw - m_new)
        decay = jnp.broadcast_to(jnp.exp(f_last + m0 - m_new), (SUBLANES, LANES))

        for h in range(A_HEADS):
            qb = q_all[:, A_DQK * h:A_DQK * (h + 1)]
            kb = k_all[:, A_DQK * h:A_DQK * (h + 1)]
            vb = v_ref[st, :, A_DV * h:A_DV * (h + 1)]
            m0_h = m_prev[h:h + 1, :]
            mx = jnp.maximum(rep[:, LANES * h:LANES * (h + 1)], m0_h)
            fc_t = rep[:, LANES * (A_HEADS + h):LANES * (A_HEADS + h + 1)]
            e = jnp.exp(jnp.where(tri, gr[h:h + 1, :] - lanes(mx, reps), -jnp.inf))
            s = lax.dot_general(qb, kb, (((1,), (1,)), ((), ())),
                                preferred_element_type=F32) * e
            g_int = jnp.exp(m0_h - mx)
            ct_h = ct[st, h]
            ncr_h = ncr[st, h]
            num = (_dot(s.astype(BF16), vb)
                   + lanes(g_int, A_DV // LANES) * _dot(qb, ct_h.astype(BF16)))
            den = jnp.sum(s, axis=1, keepdims=True) + g_int * _dot(qb, ncr_h.astype(BF16))
            scale = jnp.maximum(jnp.abs(den), jnp.exp(-(fc_t + mx)))
            hh = num / lanes(scale, A_DV // LANES)
            hn = _rms(hh, gmh_ref[:, A_DV * h:A_DV * (h + 1)])
            og = og_ref[st, :, A_DV * h:A_DV * (h + 1)]
            outa_ref[st, :, A_DV * h:A_DV * (h + 1)] = (hn * _sigmoid(og)).astype(BF16)
            kws = kb.astype(F32).T * ws[h:h + 1, :]
            dec_h = decay[h:h + 1, :]
            ct[st, h] = lanes(dec_h, A_DV // LANES) * ct_h + _dot(kws.astype(BF16), vb)
            ncr[st, h] = dec_h * ncr_h + jnp.sum(kws, axis=1, keepdims=True)
        m_s[st] = jnp.broadcast_to(m_new, (SUBLANES, LANES))

    @pl.when(c == n_chunks - 1)
    def _():
        for st in range(ns):
            convo_ref[st] = xp[st, valid:valid + SUBLANES, :]
            for h in range(A_HEADS):
                co_ref[st, h] = ct[st, h].T
                no_ref[st, h:h + 1, :] = ncr[st, h].T[0:1, :]
            mo_ref[st] = m_s[st]

    for st in range(ns):
        xp[st, 0:SUBLANES, :] = xp[st, L:L + SUBLANES, :]


def _gate_replication_matrix():
    r = np.zeros((LANES, 2 * A_HEADS * LANES), np.float32)
    for h in range(A_HEADS):
        r[GATE_LANE_CMX + h, LANES * h:LANES * (h + 1)] = 1.0
        r[GATE_LANE_FC + h, LANES * (A_HEADS + h):LANES * (A_HEADS + h + 1)] = 1.0
    return jnp.asarray(r, BF16)


def _mlstm(qk, v, og, gates, gates_c, tail0, c0, n0, m0, wconv, bconv, gmh, *, chunk, valid):
    b, s_len, _ = qk.shape
    ns = MLSTM_STREAMS
    assert b % ns == 0
    n_chunks = s_len // chunk
    rep = _gate_replication_matrix()
    seq = lambda i, c: (i, c, 0)
    per_b3 = lambda i, c: (i, 0, 0)
    per_b4 = lambda i, c: (i, 0, 0, 0)
    const2 = lambda i, c: (0, 0)
    gate_specs = [
        pl.BlockSpec((2 * SUBLANES, chunk), lambda i, c, st=st: (0, (ns * i + st) * n_chunks + c))
        for st in range(ns)]
    kernel = functools.partial(_mlstm_kernel, chunk=chunk, valid=valid, n_chunks=n_chunks)
    return pl.pallas_call(
        kernel,
        grid=(b // ns, n_chunks),
        in_specs=[
            pl.BlockSpec((ns, chunk, QK_W), seq),
            pl.BlockSpec((ns, chunk, D_A), seq),
            pl.BlockSpec((ns, chunk, D_A), seq),
        ] + gate_specs + [
            pl.BlockSpec((ns, chunk, LANES), seq),
            pl.BlockSpec(rep.shape, const2),
            pl.BlockSpec((ns, SUBLANES, QK_W), per_b3),
            pl.BlockSpec((ns, A_HEADS, A_DV, A_DQK), per_b4),
            pl.BlockSpec((ns, A_HEADS, A_DQK), per_b3),
            pl.BlockSpec((ns, SUBLANES, LANES), per_b3),
            pl.BlockSpec((CONV_W, QK_W), const2),
            pl.BlockSpec((1, QK_W), const2),
            pl.BlockSpec((1, D_A), const2),
        ],
        out_specs=[
            pl.BlockSpec((ns, chunk, D_A), seq),
            pl.BlockSpec((ns, SUBLANES, QK_W), per_b3),
            pl.BlockSpec((ns, A_HEADS, A_DV, A_DQK), per_b4),
            pl.BlockSpec((ns, A_HEADS, A_DQK), per_b3),
            pl.BlockSpec((ns, SUBLANES, LANES), per_b3),
        ],
        out_shape=(
            jax.ShapeDtypeStruct((b, s_len, D_A), BF16),
            jax.ShapeDtypeStruct((b, SUBLANES, QK_W), F32),
            jax.ShapeDtypeStruct((b, A_HEADS, A_DV, A_DQK), F32),
            jax.ShapeDtypeStruct((b, A_HEADS, A_DQK), F32),
            jax.ShapeDtypeStruct((b, SUBLANES, LANES), F32),
        ),
        scratch_shapes=[
            pltpu.VMEM((ns, chunk + SUBLANES, QK_W), F32),
            pltpu.VMEM((ns, A_HEADS, A_DQK, A_DV), F32),
            pltpu.VMEM((ns, A_HEADS, A_DQK, LANES), F32),
            pltpu.VMEM((ns, SUBLANES, LANES), F32),
        ],
        compiler_params=pltpu.CompilerParams(
            dimension_semantics=("arbitrary", "arbitrary"), vmem_limit_bytes=VMEM_LIMIT),
        name="mlstm",
    )(qk, v, og, *([gates] * ns), gates_c, rep, tail0, c0, n0, m0, wconv, bconv, gmh)


def _s5_kernel(u_ref, s0re_ref, s0im_ref, lbre_ref, lbim_ref, wb_ref, wcre_ref, wcim_ref,
               d_ref, perm_ref, permt_ref, wglu_ref, bglu_ref,
               outb_ref, sre_out, sim_out,
               bure, buim, sre, sim, utb, xtb, ytb, *, tb, n_steps):
    step = pl.program_id(0)
    batch = u_ref.shape[0]
    rows_j = batch * PERM_T
    n_j = tb // PERM_T

    @pl.when(step == 0)
    def _():
        sre[...] = s0re_ref[...]
        sim[...] = s0im_ref[...]

    strip = 8 * LANES
    blocks_per_strip = strip // STATE_PER_BLOCK
    n_strips = S5_N // strip

    def permute_in():
        for j in range(n_j):
            xj = jnp.concatenate(
                [u_ref[b, PERM_T * j:PERM_T * (j + 1), :] for b in range(batch)], axis=0)
            hi = xj.astype(BF16)
            lo = (xj - hi.astype(F32)).astype(BF16)
            uh = _dot(perm_ref[...], hi)
            rows = slice(rows_j * j, rows_j * (j + 1))
            xtb[rows, :] = uh.astype(BF16)
            utb[rows, :] = uh + _dot(perm_ref[...], lo)

    def input_map(st):
        for k in range(blocks_per_strip * st, blocks_per_strip * (st + 1)):
            r = _dot(xtb[:, LANES * k:LANES * (k + 1)], wb_ref[k])
            kc = slice(STATE_PER_BLOCK * k, STATE_PER_BLOCK * (k + 1))
            bure[:, kc] = r[:, :STATE_PER_BLOCK]
            buim[:, kc] = r[:, STATE_PER_BLOCK:]

    def recurrence(st):
        cols = slice(strip * st, strip * (st + 1))
        a_re = jnp.broadcast_to(lbre_ref[:, cols], (batch, strip))
        a_im = jnp.broadcast_to(lbim_ref[:, cols], (batch, strip))
        s_r = sre[:, cols]
        s_i = sim[:, cols]
        for t in range(tb):
            tr = slice(t * batch, (t + 1) * batch)
            n_r = a_re * s_r - a_im * s_i + bure[tr, cols]
            n_i = a_re * s_i + a_im * s_r + buim[tr, cols]
            bure[tr, cols] = n_r
            buim[tr, cols] = n_i
            s_r, s_i = n_r, n_i
        sre[:, cols] = s_r
        sim[:, cols] = s_i

    def output_map(st):
        for k in range(blocks_per_strip * st, blocks_per_strip * (st + 1)):
            kc = slice(STATE_PER_BLOCK * k, STATE_PER_BLOCK * (k + 1))
            ytb[:, LANES * k:LANES * (k + 1)] = (
                _dot(bure[:, kc].astype(BF16), wcre_ref[k])
                - _dot(buim[:, kc].astype(BF16), wcim_ref[k]))

    permute_in()
    input_map(0)
    for st in range(n_strips):
        if st + 1 < n_strips:
            input_map(st + 1)
        recurrence(st)
        output_map(st)

    z = _gelu_tanh(ytb[...] + d_ref[...] * utb[...])
    gate = _sigmoid(_dot(z.astype(BF16), wglu_ref[...]) + bglu_ref[...])
    res = (z * gate).astype(BF16)
    for j in range(n_j):
        rb = _dot(permt_ref[...], res[rows_j * j:rows_j * (j + 1), :]).astype(BF16)
        for b in range(batch):
            outb_ref[b, PERM_T * j:PERM_T * (j + 1), :] = rb[PERM_T * b:PERM_T * (b + 1), :]

    @pl.when(step == n_steps - 1)
    def _():
        sre_out[...] = sre[...]
        sim_out[...] = sim[...]


def _s5(u, s0re, s0im, lbre, lbim, wb, wcre, wcim, d_skip, perm, permt, wglu, bglu, *, tb):
    b, s_len, _ = u.shape
    n_steps = s_len // tb
    rows = b * tb
    kernel = functools.partial(_s5_kernel, tb=tb, n_steps=n_steps)
    state = jax.ShapeDtypeStruct((b, S5_N), F32)
    return pl.pallas_call(
        kernel,
        grid=(n_steps,),
        in_specs=[
            pl.BlockSpec((b, tb, D_B), lambda i: (0, i, 0)),
            _resident((b, S5_N)),
            _resident((b, S5_N)),
            _resident((1, S5_N)),
            _resident((1, S5_N)),
            _resident((N_BLOCKS, LANES, 2 * STATE_PER_BLOCK)),
            _resident((N_BLOCKS, STATE_PER_BLOCK, LANES)),
            _resident((N_BLOCKS, STATE_PER_BLOCK, LANES)),
            _resident((1, D_B)),
            _resident((b * PERM_T, b * PERM_T)),
            _resident((b * PERM_T, b * PERM_T)),
            _resident((D_B, D_B)),
            _resident((1, D_B)),
        ],
        out_specs=[
            pl.BlockSpec((b, tb, D_B), lambda i: (0, i, 0)),
            pl.BlockSpec((b, S5_N), lambda i: (0, 0)),
            pl.BlockSpec((b, S5_N), lambda i: (0, 0)),
        ],
        out_shape=(jax.ShapeDtypeStruct((b, s_len, D_B), BF16), state, state),
        scratch_shapes=[
            pltpu.VMEM((rows, S5_N), F32),
            pltpu.VMEM((rows, S5_N), F32),
            pltpu.VMEM((b, S5_N), F32),
            pltpu.VMEM((b, S5_N), F32),
            pltpu.VMEM((rows, D_B), F32),
            pltpu.VMEM((rows, D_B), BF16),
            pltpu.VMEM((rows, D_B), F32),
        ],
        compiler_params=pltpu.CompilerParams(
            dimension_semantics=("arbitrary",), vmem_limit_bytes=VMEM_LIMIT),
        name="s5",
    )(u, s0re, s0im, lbre, lbim, wb, wcre, wcim, d_skip, perm, permt, wglu, bglu)


def _outproj_tile(a_ref, b_ref, x_ref, wo_ref, gpost_ref, gffn_ref, x1_ref, hf_ref, n_sub):
    for rows in _row_blocks(a_ref.shape[0], n_sub):
        mix = (_dot(a_ref[rows, :], wo_ref[0:D_A, :])
               + _dot(b_ref[rows, :], wo_ref[D_A:D_A + D_B, :]))
        x1 = x_ref[rows, :] + _rms(mix, gpost_ref[...])
        x1_ref[rows, :] = x1
        hf_ref[rows, :] = _rms(x1, gffn_ref[...]).astype(BF16)


def _outproj_kernel(ap_ref, bp_ref, xp_ref, as_ref, bs_ref, xs_ref, wo_ref, gpost_ref, gffn_ref,
                    *rest, n_sub, n_cast, n_p):
    cast_in = rest[:n_cast]
    x1p_ref, hfp_ref, x1s_ref, hfs_ref = rest[n_cast:n_cast + 4]
    cast_out = rest[n_cast + 4:]
    shared = (wo_ref, gpost_ref, gffn_ref)
    i = pl.program_id(0)

    @pl.when(i < n_p)
    def _():
        for src, dst in zip(cast_in, cast_out):
            dst[...] = src[...].astype(BF16)
        _outproj_tile(ap_ref, bp_ref, xp_ref, *shared, x1p_ref, hfp_ref, n_sub)

    @pl.when(i == n_p)
    def _():
        _outproj_tile(as_ref, bs_ref, xs_ref, *shared, x1s_ref, hfs_ref, 1)


def _out_proj(prompt, sample, wo, g_post, g_ffn, tm, n_sub, cast=()):
    t_p, t_s = prompt[2].shape[0], sample[2].shape[0]
    n_p = t_p // tm
    tok = lambda i: (jnp.minimum(i, n_p - 1), 0)
    once = lambda i: (0, 0)
    cast_specs = [pl.BlockSpec((w.shape[0] // n_p, w.shape[1]), tok) for w in cast]
    rows_specs = lambda rows, m: [pl.BlockSpec((rows, D_A), m), pl.BlockSpec((rows, D_B), m),
                                  pl.BlockSpec((rows, D_MODEL), m)]
    out_specs = lambda rows, m: [pl.BlockSpec((rows, D_MODEL), m), pl.BlockSpec((rows, D_MODEL), m)]
    out_shapes = lambda t: (jax.ShapeDtypeStruct((t, D_MODEL), F32),
                            jax.ShapeDtypeStruct((t, D_MODEL), BF16))
    return pl.pallas_call(
        functools.partial(_outproj_kernel, n_sub=n_sub, n_cast=len(cast), n_p=n_p),
        grid=(n_p + 1,),
        in_specs=rows_specs(tm, tok) + rows_specs(t_s, once) + [
            _resident((D_A + D_B, D_MODEL)),
            _resident((1, D_MODEL)),
            _resident((1, D_MODEL)),
        ] + cast_specs,
        out_specs=out_specs(tm, tok) + out_specs(t_s, once) + cast_specs,
        out_shape=(out_shapes(t_p) + out_shapes(t_s)
                   + tuple(jax.ShapeDtypeStruct(w.shape, BF16) for w in cast)),
        compiler_params=pltpu.CompilerParams(
            dimension_semantics=("arbitrary",), vmem_limit_bytes=VMEM_LIMIT),
        name="out_proj",
    )(*prompt, *sample, wo, g_post, g_ffn, *cast)


def _ffn_up_kernel(hfp_ref, hfs_ref, wg_ref, wu_ref, actp_ref, acts_ref, wgb, wub, *, n_sub):
    m = pl.program_id(1)

    @pl.when(m == 0)
    def _():
        wgb[...] = wg_ref[...].astype(BF16)
        wub[...] = wu_ref[...].astype(BF16)

    def body(hf_ref, act_ref, n_sub):
        for rows in _row_blocks(hf_ref.shape[0], n_sub):
            hf = hf_ref[rows, :]
            a = _dot(hf, wgb[...])
            act_ref[rows, :] = (a * _sigmoid(a) * _dot(hf, wub[...])).astype(BF16)

    @pl.when(m == 0)
    def _():
        body(hfs_ref, acts_ref, 1)

    @pl.when(m > 0)
    def _():
        body(hfp_ref, actp_ref, n_sub)


def _ffn_up(hf_p, hf_s, wg, wu, tm, tf, n_sub):
    t_p, t_s = hf_p.shape[0], hf_s.shape[0]
    n_p = t_p // tm
    n_f = D_FF // tf
    prompt_tile = lambda m: jnp.maximum(m - 1, 0)
    return pl.pallas_call(
        functools.partial(_ffn_up_kernel, n_sub=n_sub),
        grid=(n_f, n_p + 1),
        in_specs=[
            pl.BlockSpec((tm, D_MODEL), lambda f, m: (prompt_tile(m), 0)),
            pl.BlockSpec((t_s, D_MODEL), lambda f, m: (0, 0)),
            pl.BlockSpec((D_MODEL, tf), lambda f, m: (0, f)),
            pl.BlockSpec((D_MODEL, tf), lambda f, m: (0, f)),
        ],
        out_specs=[
            pl.BlockSpec((tm, tf), lambda f, m: (prompt_tile(m), f)),
            pl.BlockSpec((t_s, tf), lambda f, m: (0, f)),
        ],
        out_shape=(jax.ShapeDtypeStruct((t_p, D_FF), BF16),
                   jax.ShapeDtypeStruct((t_s, D_FF), BF16)),
        scratch_shapes=[pltpu.VMEM((D_MODEL, tf), BF16), pltpu.VMEM((D_MODEL, tf), BF16)],
        compiler_params=pltpu.CompilerParams(
            dimension_semantics=("arbitrary", "arbitrary"), vmem_limit_bytes=VMEM_LIMIT),
        name="ffn_up",
    )(hf_p, hf_s, wg, wu)


def _ffn_down_kernel(act_ref, wd_ref, x1_ref, gp_ref, out_ref, *, n_sub):
    for rows in _row_blocks(act_ref.shape[0], n_sub):
        ff = _dot(act_ref[rows, :], wd_ref[...])
        out_ref[rows, :] = x1_ref[rows, :] + _rms(ff, gp_ref[...])


def _ffn_down(act, wd, x1, g_post, tm, n_sub):
    t = act.shape[0]
    tok = lambda i: (i, 0)
    return pl.pallas_call(
        functools.partial(_ffn_down_kernel, n_sub=n_sub),
        grid=(t // tm,),
        in_specs=[
            pl.BlockSpec((tm, D_FF), tok),
            _resident((D_FF, D_MODEL)),
            pl.BlockSpec((tm, D_MODEL), tok),
            _resident((1, D_MODEL)),
        ],
        out_specs=pl.BlockSpec((tm, D_MODEL), tok),
        out_shape=jax.ShapeDtypeStruct((t, D_MODEL), F32),
        compiler_params=pltpu.CompilerParams(
            dimension_semantics=("arbitrary",), vmem_limit_bytes=VMEM_LIMIT),
        name="ffn_down",
    )(act, wd, x1, g_post)


def _block_diag_in(bbt):
    x = bbt.reshape(B_GROUP, N_BLOCKS, GROUPS_PER_BLOCK, B_STATE)
    x = jnp.transpose(x, (1, 2, 0, 3))
    eye = jnp.eye(GROUPS_PER_BLOCK, dtype=bool)[None, :, None, :, None]
    y = jnp.where(eye, x[:, :, :, None, :], 0.0)
    return y.reshape(N_BLOCKS, LANES, STATE_PER_BLOCK)


def _block_diag_out(c):
    x = c.reshape(N_BLOCKS, GROUPS_PER_BLOCK, B_GROUP, B_STATE)
    x = jnp.transpose(x, (0, 1, 3, 2))
    eye = jnp.eye(GROUPS_PER_BLOCK, dtype=bool)[None, :, None, :, None]
    y = jnp.where(eye, x[:, :, :, None, :], 0.0)
    return y.reshape(N_BLOCKS, STATE_PER_BLOCK, LANES)


def _row_permutation(batch):
    n = batch * PERM_T
    src = np.arange(n)
    b, t = src // PERM_T, src % PERM_T
    p = np.zeros((n, n), np.float32)
    p[t * batch + b, src] = 1.0
    return jnp.asarray(p, BF16), jnp.asarray(p.T, BF16)


def _pad_rows_front(x, rows):
    pad = [(0, 0)] * x.ndim
    pad[-2] = (rows - x.shape[-2], 0)
    return jnp.pad(x, pad)


def _pad_rows_back(x, rows):
    pad = [(0, 0)] * x.ndim
    pad[-2] = (0, rows - x.shape[-2])
    return jnp.pad(x, pad)


def _mixers(proj, conv0, c0, n0, m0, sre0, sim0, p, *, b, chunk, tb):
    qk, v, og, u, gates, gates_c = proj
    t = qk.shape[0]
    s_len = t // b
    assert s_len % chunk == 0 or s_len < chunk
    seg = min(s_len, chunk)
    s_pad = -(-s_len // chunk) * chunk
    extra = s_pad - s_len
    def seq3(a, mode="constant"):
        a = a.reshape(b, s_len, a.shape[-1])
        return a if extra == 0 else jnp.pad(a, ((0, 0), (0, extra), (0, 0)), mode=mode)
    g3 = gates.reshape(2 * SUBLANES, b, s_len)
    if extra:
        pad_t = ((0, 0), (0, 0), (0, extra))
        g3 = jnp.concatenate([
            jnp.pad(g3[:SUBLANES], pad_t, constant_values=-jnp.inf),
            jnp.pad(g3[SUBLANES:], pad_t, mode="edge")], axis=0)
    out_a, conv_o, c_o, n_o, m_o = _mlstm(
        seq3(qk), seq3(v), seq3(og), g3.reshape(2 * SUBLANES, b * s_pad),
        seq3(gates_c, "edge"), _pad_rows_front(conv0, SUBLANES), c0, n0,
        jnp.broadcast_to(_pad_rows_back(m0[..., None], SUBLANES), (b, SUBLANES, LANES)),
        p["w_conv"], p["b_conv"], p["g_mh"], chunk=chunk, valid=seg)
    out_a = out_a[:, :s_len].reshape(t, D_A)

    out_b, sre_o, sim_o = _s5(
        u.reshape(b, s_len, D_B), sre0.reshape(b, S5_N), sim0.reshape(b, S5_N),
        p["lbre"], p["lbim"], p["wb"], p["wcre"], p["wcim"], p["s5_d"],
        p["perm"], p["permt"], p["w_glu"], p["b_glu"], tb=tb)
    states = (
        conv_o[:, SUBLANES - (CONV_W - 1):, :],
        c_o,
        n_o,
        m_o[:, :A_HEADS, 0],
        sre_o.reshape(b, B_GROUPS, B_STATE),
        sim_o.reshape(b, B_GROUPS, B_STATE),
    )
    return out_a, out_b.reshape(t, D_B), states


def kernel(x_prompt, x_sample, state_conv, state_mlstm_c, state_mlstm_n, state_mlstm_m, state_s5_re, state_s5_im, g_pre_mix, w_in, w_conv, b_conv, b_igate, b_fgate, g_mh, s5_lam_re, s5_lam_im, s5_log_dt, s5_b_re, s5_b_im, s5_c_re, s5_c_im, s5_d, w_glu, b_glu, w_out, g_post_mix, g_pre_ffn, w_gate, w_up, w_down, g_post_ffn):
    l = 0
    bp = x_prompt.shape[0]
    lbre, lbim, bbt_re, bbt_im = _discretise(
        s5_lam_re[l], s5_lam_im[l], s5_log_dt[l],
        jnp.transpose(s5_b_re[l], (2, 0, 1)), jnp.transpose(s5_b_im[l], (2, 0, 1)))

    o1 = QK_W
    o2 = o1 + D_A
    o3 = o2 + D_A
    o4 = o3 + A_HEADS
    o5 = o4 + A_HEADS
    wt = jnp.swapaxes(w_in[l], 0, 1)
    perm, permt = _row_permutation(bp)
    row = lambda a: a.reshape(1, -1)
    p = {
        "g_pre_mix": row(g_pre_mix[l]),
        "w_in_t": wt[:o3].astype(BF16),
        "w_in_gt": jnp.pad(wt[o3:o5], ((0, 2 * SUBLANES - 2 * A_HEADS), (0, 0))).astype(BF16),
        "w_in_ut": wt[o5:].astype(BF16),
        "gbias": jnp.concatenate(
            [b_igate[l], b_fgate[l], jnp.zeros((SUBLANES,), F32)]).reshape(2 * SUBLANES, 1),
        "w_conv": w_conv[l],
        "b_conv": row(b_conv[l]),
        "g_mh": row(g_mh[l]),
        "lbre": lbre.reshape(1, S5_N),
        "lbim": lbim.reshape(1, S5_N),
        "wb": jnp.concatenate([_block_diag_in(bbt_re), _block_diag_in(bbt_im)], axis=-1).astype(BF16),
        "wcre": _block_diag_out(s5_c_re[l]).astype(BF16),
        "wcim": _block_diag_out(s5_c_im[l]).astype(BF16),
        "s5_d": row(s5_d[l]),
        "perm": perm,
        "permt": permt,
        "b_glu": row(b_glu[l]),
        "g_post_mix": row(g_post_mix[l]),
        "g_pre_ffn": row(g_pre_ffn[l]),
    }

    bs, s_sample = x_sample.shape[0], x_sample.shape[1]
    xp2d = x_prompt.reshape(-1, D_MODEL)
    xs2d = x_sample.reshape(-1, D_MODEL)
    chunk = 128
    proj = _in_proj(xp2d, xs2d, p, tm=512, n_sub=2, seg_p=chunk, seg_s=min(s_sample, chunk),
                    cast=(w_glu[l], w_out[l]))
    proj_p, proj_s = proj[:N_PROJ_OUT], proj[N_PROJ_OUT:2 * N_PROJ_OUT]
    p["w_glu"], p["w_out"] = proj[2 * N_PROJ_OUT:]

    a_p, b_p, st_p = _mixers(
        proj_p,
        jnp.zeros((bp, CONV_W - 1, QK_W), F32),
        jnp.zeros((bp, A_HEADS, A_DV, A_DQK), F32),
        jnp.zeros((bp, A_HEADS, A_DQK), F32),
        jnp.zeros((bp, A_HEADS), F32),
        jnp.zeros((bp, B_GROUPS, B_STATE), F32),
        jnp.zeros((bp, B_GROUPS, B_STATE), F32),
        p, b=bp, chunk=chunk, tb=64)
    a_s, b_s, st_s = _mixers(
        proj_s, state_conv[l], state_mlstm_c[l], state_mlstm_n[l], state_mlstm_m[l],
        state_s5_re[l], state_s5_im[l], p, b=bs, chunk=chunk, tb=s_sample)

    x1_p, hf_p, x1_s, hf_s, wd = _out_proj(
        (a_p, b_p, xp2d), (a_s, b_s, xs2d), p["w_out"], p["g_post_mix"], p["g_pre_ffn"],
        tm=512, n_sub=4, cast=(w_down[l],))

    act_p, act_s = _ffn_up(hf_p, hf_s, w_gate[l], w_up[l], tm=2048, tf=512, n_sub=2)
    g_post = row(g_post_ffn[l])
    yp = _ffn_down(act_p, wd, x1_p, g_post, tm=512, n_sub=2).reshape(x_prompt.shape)
    ys = _ffn_down(act_s, wd, x1_s, g_post, tm=128, n_sub=1).reshape(x_sample.shape)
    return (yp, ys) + tuple(a[None] for a in st_p) + tuple(a[None] for a in st_s)
```

```python
import functools

import numpy as np
import jax
import jax.numpy as jnp
from jax import lax
from jax.experimental import pallas as pl
from jax.experimental.pallas import tpu as pltpu

F32 = jnp.float32
BF16 = jnp.bfloat16

D_MODEL = 2048
D_A = 1024
D_B = 1024
A_HEADS = 4
A_DV = 256
A_DQK = 128
QK_W = 1024
CONV_W = 4
B_GROUP = 16
B_GROUPS = 64
B_STATE = 64
S5_N = B_GROUPS * B_STATE
D_FF = 5632
EPS = 1e-6

SUBLANES = 8
LANES = 128
GROUPS_PER_BLOCK = LANES // B_GROUP
N_BLOCKS = B_GROUPS // GROUPS_PER_BLOCK
STATE_PER_BLOCK = GROUPS_PER_BLOCK * B_STATE
PERM_T = 16
VMEM_LIMIT = 56 * 1024 * 1024


def _sigmoid(x):
    return 1.0 / (1.0 + jnp.exp(-x))


def _gelu_tanh(x):
    c = np.sqrt(2 / np.pi).astype(np.float32)
    return x * (0.5 * (1.0 + jnp.tanh(c * (x + 0.044715 * (x ** 3)))))


def _rms(x, g):
    return x * lax.rsqrt(jnp.mean(x * x, axis=-1, keepdims=True) + EPS) * g


def _dot(a, b):
    return jnp.dot(a, b, preferred_element_type=F32)


def _log_sigmoid(z):
    return jnp.minimum(z, 0.0) - jnp.log1p(jnp.exp(-jnp.abs(z)))


def _split3(x):
    hi = x.astype(BF16)
    r = x - hi.astype(F32)
    mid = r.astype(BF16)
    lo = (r - mid.astype(F32)).astype(BF16)
    return hi, mid, lo


def _dot3_right(x, sel):
    hi, mid, lo = _split3(x)
    return _dot(hi, sel) + _dot(mid, sel) + _dot(lo, sel)


def _dot3_left(sel, x):
    hi, mid, lo = _split3(x)
    return _dot(sel, hi) + _dot(sel, mid) + _dot(sel, lo)


def _row_blocks(rows, n_sub):
    assert rows % n_sub == 0
    size = rows // n_sub
    return [slice(size * r, size * (r + 1)) for r in range(n_sub)]


def _resident(shape):
    zeros = (0,) * len(shape)
    return pl.BlockSpec(shape, lambda *_: zeros, pipeline_mode=pl.Buffered(1))


def _disc_kernel(lam_re_ref, lam_im_ref, logdt_ref, bre_ref, bim_ref,
                 lbre_ref, lbim_ref, bbre_ref, bbim_ref):
    lam_re = lam_re_ref[...]
    lam_im = lam_im_ref[...]
    dt = jnp.exp(logdt_ref[...])
    mag = jnp.exp(lam_re * dt)
    ang = lam_im * dt
    lb_re = mag * jnp.cos(ang)
    lb_im = mag * jnp.sin(ang)
    den = lam_re * lam_re + lam_im * lam_im
    nr = lb_re - 1.0
    f_re = (nr * lam_re + lb_im * lam_im) / den
    f_im = (lb_im * lam_re - nr * lam_im) / den
    lbre_ref[...] = lb_re
    lbim_ref[...] = lb_im
    b_re = bre_ref[...]
    b_im = bim_ref[...]
    bbre_ref[...] = f_re[None] * b_re - f_im[None] * b_im
    bbim_ref[...] = f_re[None] * b_im + f_im[None] * b_re


def _discretise(lam_re, lam_im, log_dt, bt_re, bt_im):
    gp = jax.ShapeDtypeStruct((B_GROUPS, B_STATE), F32)
    igp = jax.ShapeDtypeStruct((B_GROUP, B_GROUPS, B_STATE), F32)
    return pl.pallas_call(
        _disc_kernel, out_shape=(gp, gp, igp, igp), name="s5_discretise",
    )(lam_re, lam_im, log_dt.reshape(B_GROUPS, 1), bt_re, bt_im)


GATE_COL = QK_W + 2 * D_A
GATE_LANE_CMX = A_HEADS
GATE_LANE_FC = 2 * A_HEADS


def _dot_t(a, bt):
    return lax.dot_general(a, bt, (((1,), (1,)), ((), ())), preferred_element_type=F32)


def _inproj_tile(x_ref, gpre_ref, wt_ref, wut_ref, wgt_ref, gb_ref, triu_ref, tril_ref, outs,
                 n_sub, seg):
    qk_out, v_out, og_out, u_out, g_out, c_out = outs
    for rows in _row_blocks(x_ref.shape[0], n_sub):
        hm = _rms(x_ref[rows, :], gpre_ref[...]).astype(BF16)

        n = rows.stop - rows.start
        z = _dot_t(wgt_ref[...], hm) + gb_ref[...]
        qk_out[rows, :] = _dot_t(hm, wt_ref[0:QK_W, :])

        row = lax.broadcasted_iota(jnp.int32, z.shape, 0)
        zf = jnp.where((row >= A_HEADS) & (row < 2 * A_HEADS), _log_sigmoid(z), z)
        zf8 = zf[0:SUBLANES]
        zc = jnp.concatenate([zf8, jnp.zeros((LANES - SUBLANES, n), F32)], axis=0).T
        lane = lax.broadcasted_iota(jnp.int32, zc.shape, 1)
        lf_c = jnp.where((lane >= A_HEADS) & (lane < 2 * A_HEADS), zc, 0.0)
        fc = pltpu.roll(_dot3_right(zf, triu_ref[...])[0:SUBLANES], A_HEADS, axis=0)
        fc_c = _dot3_left(tril_ref[...], lf_c)
        v_out[rows, :] = _dot_t(hm, wt_ref[QK_W:QK_W + D_A, :]).astype(BF16)
        og_out[rows, :] = _dot_t(hm, wt_ref[QK_W + D_A:QK_W + 2 * D_A, :])
        u_out[rows, :] = _dot_t(hm, wut_ref[...])

        g_out[0:SUBLANES, rows] = zf8 - fc
        g_out[SUBLANES:2 * SUBLANES, rows] = fc
        cm = pltpu.roll(zc, A_HEADS, axis=1) - fc_c
        t_in = lax.broadcasted_iota(jnp.int32, zc.shape, 0) & (seg - 1)
        sh = 1
        while sh < seg:
            cm = jnp.where(t_in >= sh, jnp.maximum(cm, pltpu.roll(cm, sh, axis=0)), cm)
            sh *= 2
        c_out[rows, :] = jnp.where(lane < GATE_LANE_FC, cm, pltpu.roll(fc_c, A_HEADS, axis=1))


N_PROJ_OUT = 6


def _inproj_kernel(xp_ref, xs_ref, gpre_ref, wt_ref, wut_ref, wgt_ref, gb_ref,
                   triu_p, tril_p, triu_s, tril_s, *rest, cfg_p, cfg_s, n_cast, n_p):
    cast_in = rest[:n_cast]
    outs_p = rest[n_cast:n_cast + N_PROJ_OUT]
    outs_s = rest[n_cast + N_PROJ_OUT:n_cast + 2 * N_PROJ_OUT]
    cast_out = rest[n_cast + 2 * N_PROJ_OUT:]
    shared = (gpre_ref, wt_ref, wut_ref, wgt_ref, gb_ref)
    i = pl.program_id(0)

    @pl.when(i < n_p)
    def _():
        for src, dst in zip(cast_in, cast_out):
            dst[...] = src[...].astype(BF16)
        _inproj_tile(xp_ref, *shared, triu_p, tril_p, outs_p, *cfg_p)

    @pl.when(i == n_p)
    def _():
        _inproj_tile(xs_ref, *shared, triu_s, tril_s, outs_s, *cfg_s)


def _segment_prefix_matrices(n, seg):
    idx = np.arange(n)
    u = ((idx[:, None] // seg == idx[None, :] // seg) & (idx[:, None] <= idx[None, :]))
    u = u.astype(np.float32)
    return jnp.asarray(u, BF16), jnp.asarray(u.T, BF16)


def _in_proj(xp2d, xs2d, p, *, tm, n_sub, seg_p, seg_s, cast=()):
    t_p, t_s = xp2d.shape[0], xs2d.shape[0]
    n_p = t_p // tm
    assert seg_p & (seg_p - 1) == 0 and seg_s & (seg_s - 1) == 0
    triu_p, tril_p = _segment_prefix_matrices(tm // n_sub, seg_p)
    triu_s, tril_s = _segment_prefix_matrices(t_s, seg_s)
    tile = lambda i: jnp.minimum(i, n_p - 1)
    tok = lambda i: (tile(i), 0)
    once = lambda i: (0, 0)
    cast_specs = [pl.BlockSpec((w.shape[0] // n_p, w.shape[1]), tok) for w in cast]

    def out_specs(rows, row_map, gate_map):
        return [
            pl.BlockSpec((rows, QK_W), row_map),
            pl.BlockSpec((rows, D_A), row_map),
            pl.BlockSpec((rows, D_A), row_map),
            pl.BlockSpec((rows, D_B), row_map),
            pl.BlockSpec((2 * SUBLANES, rows), gate_map),
            pl.BlockSpec((rows, LANES), row_map),
        ]

    def out_shapes(t):
        return (
            jax.ShapeDtypeStruct((t, QK_W), F32),
            jax.ShapeDtypeStruct((t, D_A), BF16),
            jax.ShapeDtypeStruct((t, D_A), F32),
            jax.ShapeDtypeStruct((t, D_B), F32),
            jax.ShapeDtypeStruct((2 * SUBLANES, t), F32),
            jax.ShapeDtypeStruct((t, LANES), F32),
        )

    kernel = functools.partial(_inproj_kernel, cfg_p=(n_sub, seg_p), cfg_s=(1, seg_s),
                               n_cast=len(cast), n_p=n_p)
    return pl.pallas_call(
        kernel,
        grid=(n_p + 1,),
        in_specs=[
            pl.BlockSpec((tm, D_MODEL), tok),
            pl.BlockSpec((t_s, D_MODEL), once),
            _resident((1, D_MODEL)),
            _resident((GATE_COL, D_MODEL)),
            _resident((D_B, D_MODEL)),
            _resident((2 * SUBLANES, D_MODEL)),
            _resident((2 * SUBLANES, 1)),
            _resident(triu_p.shape),
            _resident(tril_p.shape),
            _resident(triu_s.shape),
            _resident(tril_s.shape),
        ] + cast_specs,
        out_specs=(out_specs(tm, tok, lambda i: (0, tile(i))) + out_specs(t_s, once, once)
                   + cast_specs),
        out_shape=(out_shapes(t_p) + out_shapes(t_s)
                   + tuple(jax.ShapeDtypeStruct(w.shape, BF16) for w in cast)),
        compiler_params=pltpu.CompilerParams(
            dimension_semantics=("arbitrary",), vmem_limit_bytes=VMEM_LIMIT),
        name="in_proj",
    )(xp2d, xs2d, p["g_pre_mix"], p["w_in_t"], p["w_in_ut"], p["w_in_gt"], p["gbias"],
      triu_p, tril_p, triu_s, tril_s, *cast)


MLSTM_STREAMS = 2


def _head_cols(h):
    return slice(A_DV * h, A_DV * (h + 1))


def _mlstm_chunk(qk_raw, g, gc, v_at, og_at, out_at, xp, ct, ncr, m_s,
                 rep_ref, wconv_ref, bconv_ref, gmh_ref, after_head=None):
    L = qk_raw.shape[0]
    reps = L // LANES

    def lanes(x, n):
        return x if n == 1 else jnp.concatenate([x] * n, axis=1)

    row_i = lax.broadcasted_iota(jnp.int32, (L, L), 0)
    col_i = lax.broadcasted_iota(jnp.int32, (L, L), 1)
    tri = col_i <= row_i
    lane = lax.broadcasted_iota(jnp.int32, (SUBLANES, L), 1)

    xp[SUBLANES:SUBLANES + L, :] = qk_raw
    xs = xp[...]
    assert CONV_W == 4
    x1 = pltpu.roll(xs, 1, axis=0)
    near = xs * wconv_ref[3:4, :] + x1 * wconv_ref[2:3, :]
    far = xs * wconv_ref[1:2, :] + x1 * wconv_ref[0:1, :]
    acc = (near + pltpu.roll(far, 2, axis=0))[SUBLANES:, :] + bconv_ref[...]
    qk = acc * _sigmoid(acc)
    q_all = (qk[:, :QK_W // 2] * (A_DQK ** -0.5)).astype(BF16)
    k_all = qk[:, QK_W // 2:].astype(BF16)

    rep = _dot3_right(gc, rep_ref[...])

    gr = g[0:SUBLANES]
    fc = g[SUBLANES:2 * SUBLANES]
    f_last = jnp.sum(jnp.where(lane == L - 1, fc, 0.0), axis=1, keepdims=True)
    m_prev = m_s[...]
    m0 = jnp.max(m_prev, axis=1, keepdims=True)
    lw = f_last + gr
    m_new = jnp.maximum(f_last + m0, jnp.max(lw, axis=1, keepdims=True))
    ws = jnp.exp(lw - m_new)
    decay = jnp.broadcast_to(jnp.exp(f_last + m0 - m_new), (SUBLANES, LANES))

    for h in range(A_HEADS):
        qb = q_all[:, A_DQK * h:A_DQK * (h + 1)]
        kb = k_all[:, A_DQK * h:A_DQK * (h + 1)]
        vb = v_at(h)
        m0_h = m_prev[h:h + 1, :]
        mx = jnp.maximum(rep[:, LANES * h:LANES * (h + 1)], m0_h)
        fc_t = rep[:, LANES * (A_HEADS + h):LANES * (A_HEADS + h + 1)]
        e = jnp.exp(jnp.where(tri, gr[h:h + 1, :] - lanes(mx, reps), -jnp.inf))
        s = lax.dot_general(qb, kb, (((1,), (1,)), ((), ())),
                            preferred_element_type=F32) * e
        g_int = jnp.exp(m0_h - mx)
        ct_h = ct[h]
        ncr_h = ncr[h]
        num = (_dot(s.astype(BF16), vb)
               + lanes(g_int, A_DV // LANES) * _dot(qb, ct_h.astype(BF16)))
        den = jnp.sum(s, axis=1, keepdims=True) + g_int * _dot(qb, ncr_h.astype(BF16))
        scale = jnp.maximum(jnp.abs(den), jnp.exp(-(fc_t + mx)))
        hh = num / lanes(scale, A_DV // LANES)
        hn = _rms(hh, gmh_ref[:, _head_cols(h)])
        out_at(h, (hn * _sigmoid(og_at(h))).astype(BF16))
        kws = kb.astype(F32).T * ws[h:h + 1, :]
        dec_h = decay[h:h + 1, :]
        ct[h] = lanes(dec_h, A_DV // LANES) * ct_h + _dot(kws.astype(BF16), vb)
        ncr[h] = dec_h * ncr_h + jnp.sum(kws, axis=1, keepdims=True)
        if after_head is not None:
            after_head(h)
    m_s[...] = jnp.broadcast_to(m_new, (SUBLANES, LANES))


def _mlstm_load_state(xp, ct, ncr, m_s, tail0, c0_ref, n0_ref, m0):
    xp[0:SUBLANES, :] = tail0
    for h in range(A_HEADS):
        ct[h] = c0_ref[h].T
        ncr[h] = jnp.broadcast_to(n0_ref[h:h + 1, :], (A_DQK, A_DQK)).T
    m_s[...] = m0


def _mlstm_store_state(ct, ncr, m_s, co_ref, no_ref, mo_ref):
    for h in range(A_HEADS):
        co_ref[h] = ct[h].T
        no_ref[h:h + 1, :] = ncr[h].T[0:1, :]
    mo_ref[...] = m_s[...]


def _mlstm_kernel(*refs, chunk, valid, n_chunks):
    ns = MLSTM_STREAMS
    qk_ref, v_ref, og_ref = refs[0:3]
    g_refs = refs[3:3 + ns]
    (gc_ref, rep_ref, tail0_ref, c0_ref, n0_ref, m0_ref, wconv_ref, bconv_ref, gmh_ref,
     outa_ref, convo_ref, co_ref, no_ref, mo_ref, xp, ct, ncr, m_s) = refs[3 + ns:]
    L = chunk
    c = pl.program_id(1)

    @pl.when(c == 0)
    def _():
        for st in range(ns):
            _mlstm_load_state(xp.at[st], ct.at[st], ncr.at[st], m_s.at[st],
                              tail0_ref[st], c0_ref.at[st], n0_ref.at[st], m0_ref[st])

    for st in range(ns):
        def store(h, x, st=st):
            outa_ref[st, :, _head_cols(h)] = x

        _mlstm_chunk(qk_ref[st], g_refs[st][...], gc_ref[st],
                     lambda h, st=st: v_ref[st, :, _head_cols(h)],
                     lambda h, st=st: og_ref[st, :, _head_cols(h)],
                     store, xp.at[st], ct.at[st], ncr.at[st], m_s.at[st],
                     rep_ref, wconv_ref, bconv_ref, gmh_ref)

    @pl.when(c == n_chunks - 1)
    def _():
        for st in range(ns):
            convo_ref[st] = xp[st, valid:valid + SUBLANES, :]
            _mlstm_store_state(ct.at[st], ncr.at[st], m_s.at[st],
                               co_ref.at[st], no_ref.at[st], mo_ref.at[st])

    for st in range(ns):
        xp[st, 0:SUBLANES, :] = xp[st, L:L + SUBLANES, :]


def _gate_replication_matrix():
    r = np.zeros((LANES, 2 * A_HEADS * LANES), np.float32)
    for h in range(A_HEADS):
        r[GATE_LANE_CMX + h, LANES * h:LANES * (h + 1)] = 1.0
        r[GATE_LANE_FC + h, LANES * (A_HEADS + h):LANES * (A_HEADS + h + 1)] = 1.0
    return jnp.asarray(r, BF16)


def _mlstm(qk, v, og, gates, gates_c, tail0, c0, n0, m0, wconv, bconv, gmh, *, chunk, valid):
    b, s_len, _ = qk.shape
    ns = MLSTM_STREAMS
    assert b % ns == 0
    n_chunks = s_len // chunk
    rep = _gate_replication_matrix()
    seq = lambda i, c: (i, c, 0)
    per_b3 = lambda i, c: (i, 0, 0)
    per_b4 = lambda i, c: (i, 0, 0, 0)
    const2 = lambda i, c: (0, 0)
    gate_specs = [
        pl.BlockSpec((2 * SUBLANES, chunk), lambda i, c, st=st: (0, (ns * i + st) * n_chunks + c))
        for st in range(ns)]
    kernel = functools.partial(_mlstm_kernel, chunk=chunk, valid=valid, n_chunks=n_chunks)
    return pl.pallas_call(
        kernel,
        grid=(b // ns, n_chunks),
        in_specs=[
            pl.BlockSpec((ns, chunk, QK_W), seq),
            pl.BlockSpec((ns, chunk, D_A), seq),
            pl.BlockSpec((ns, chunk, D_A), seq),
        ] + gate_specs + [
            pl.BlockSpec((ns, chunk, LANES), seq),
            pl.BlockSpec(rep.shape, const2),
            pl.BlockSpec((ns, SUBLANES, QK_W), per_b3),
            pl.BlockSpec((ns, A_HEADS, A_DV, A_DQK), per_b4),
            pl.BlockSpec((ns, A_HEADS, A_DQK), per_b3),
            pl.BlockSpec((ns, SUBLANES, LANES), per_b3),
            pl.BlockSpec((CONV_W, QK_W), const2),
            pl.BlockSpec((1, QK_W), const2),
            pl.BlockSpec((1, D_A), const2),
        ],
        out_specs=[
            pl.BlockSpec((ns, chunk, D_A), seq),
            pl.BlockSpec((ns, SUBLANES, QK_W), per_b3),
            pl.BlockSpec((ns, A_HEADS, A_DV, A_DQK), per_b4),
            pl.BlockSpec((ns, A_HEADS, A_DQK), per_b3),
            pl.BlockSpec((ns, SUBLANES, LANES), per_b3),
        ],
        out_shape=(
            jax.ShapeDtypeStruct((b, s_len, D_A), BF16),
            jax.ShapeDtypeStruct((b, SUBLANES, QK_W), F32),
            jax.ShapeDtypeStruct((b, A_HEADS, A_DV, A_DQK), F32),
            jax.ShapeDtypeStruct((b, A_HEADS, A_DQK), F32),
            jax.ShapeDtypeStruct((b, SUBLANES, LANES), F32),
        ),
        scratch_shapes=[
            pltpu.VMEM((ns, chunk + SUBLANES, QK_W), F32),
            pltpu.VMEM((ns, A_HEADS, A_DQK, A_DV), F32),
            pltpu.VMEM((ns, A_HEADS, A_DQK, LANES), F32),
            pltpu.VMEM((ns, SUBLANES, LANES), F32),
        ],
        compiler_params=pltpu.CompilerParams(
            dimension_semantics=("arbitrary", "arbitrary"), vmem_limit_bytes=VMEM_LIMIT),
        name="mlstm",
    )(qk, v, og, *([gates] * ns), gates_c, rep, tail0, c0, n0, m0, wconv, bconv, gmh)


def _s5_kernel(u_ref, s0re_ref, s0im_ref, lbre_ref, lbim_ref, wb_ref, wcre_ref, wcim_ref,
               d_ref, perm_ref, permt_ref, wglu_ref, bglu_ref,
               outb_ref, sre_out, sim_out,
               bure, buim, sre, sim, utb, xtb, ytb, *, tb, n_steps):
    step = pl.program_id(0)
    batch = u_ref.shape[0]
    rows_j = batch * PERM_T
    n_j = tb // PERM_T

    @pl.when(step == 0)
    def _():
        sre[...] = s0re_ref[...]
        sim[...] = s0im_ref[...]

    strip = 8 * LANES
    blocks_per_strip = strip // STATE_PER_BLOCK
    n_strips = S5_N // strip

    def permute_in():
        for j in range(n_j):
            xj = jnp.concatenate(
                [u_ref[b, PERM_T * j:PERM_T * (j + 1), :] for b in range(batch)], axis=0)
            hi = xj.astype(BF16)
            lo = (xj - hi.astype(F32)).astype(BF16)
            uh = _dot(perm_ref[...], hi)
            rows = slice(rows_j * j, rows_j * (j + 1))
            xtb[rows, :] = uh.astype(BF16)
            utb[rows, :] = uh + _dot(perm_ref[...], lo)

    def input_map(st):
        for k in range(blocks_per_strip * st, blocks_per_strip * (st + 1)):
            r = _dot(xtb[:, LANES * k:LANES * (k + 1)], wb_ref[k])
            kc = slice(STATE_PER_BLOCK * k, STATE_PER_BLOCK * (k + 1))
            bure[:, kc] = r[:, :STATE_PER_BLOCK]
            buim[:, kc] = r[:, STATE_PER_BLOCK:]

    def recurrence(st):
        cols = slice(strip * st, strip * (st + 1))
        a_re = jnp.broadcast_to(lbre_ref[:, cols], (batch, strip))
        a_im = jnp.broadcast_to(lbim_ref[:, cols], (batch, strip))
        s_r = sre[:, cols]
        s_i = sim[:, cols]
        for t in range(tb):
            tr = slice(t * batch, (t + 1) * batch)
            n_r = a_re * s_r - a_im * s_i + bure[tr, cols]
            n_i = a_re * s_i + a_im * s_r + buim[tr, cols]
            bure[tr, cols] = n_r
            buim[tr, cols] = n_i
            s_r, s_i = n_r, n_i
        sre[:, cols] = s_r
        sim[:, cols] = s_i

    def output_map(st):
        for k in range(blocks_per_strip * st, blocks_per_strip * (st + 1)):
            kc = slice(STATE_PER_BLOCK * k, STATE_PER_BLOCK * (k + 1))
            ytb[:, LANES * k:LANES * (k + 1)] = (
                _dot(bure[:, kc].astype(BF16), wcre_ref[k])
                - _dot(buim[:, kc].astype(BF16), wcim_ref[k]))

    permute_in()
    input_map(0)
    for st in range(n_strips):
        if st + 1 < n_strips:
            input_map(st + 1)
        recurrence(st)
        output_map(st)

    z = _gelu_tanh(ytb[...] + d_ref[...] * utb[...])
    gate = _sigmoid(_dot(z.astype(BF16), wglu_ref[...]) + bglu_ref[...])
    res = (z * gate).astype(BF16)
    for j in range(n_j):
        rb = _dot(permt_ref[...], res[rows_j * j:rows_j * (j + 1), :]).astype(BF16)
        for b in range(batch):
            outb_ref[b, PERM_T * j:PERM_T * (j + 1), :] = rb[PERM_T * b:PERM_T * (b + 1), :]

    @pl.when(step == n_steps - 1)
    def _():
        sre_out[...] = sre[...]
        sim_out[...] = sim[...]


def _s5(u, s0re, s0im, lbre, lbim, wb, wcre, wcim, d_skip, perm, permt, wglu, bglu, *, tb):
    b, s_len, _ = u.shape
    n_steps = s_len // tb
    rows = b * tb
    kernel = functools.partial(_s5_kernel, tb=tb, n_steps=n_steps)
    state = jax.ShapeDtypeStruct((b, S5_N), F32)
    return pl.pallas_call(
        kernel,
        grid=(n_steps,),
        in_specs=[
            pl.BlockSpec((b, tb, D_B), lambda i: (0, i, 0)),
            _resident((b, S5_N)),
            _resident((b, S5_N)),
            _resident((1, S5_N)),
            _resident((1, S5_N)),
            _resident((N_BLOCKS, LANES, 2 * STATE_PER_BLOCK)),
            _resident((N_BLOCKS, STATE_PER_BLOCK, LANES)),
            _resident((N_BLOCKS, STATE_PER_BLOCK, LANES)),
            _resident((1, D_B)),
            _resident((b * PERM_T, b * PERM_T)),
            _resident((b * PERM_T, b * PERM_T)),
            _resident((D_B, D_B)),
            _resident((1, D_B)),
        ],
        out_specs=[
            pl.BlockSpec((b, tb, D_B), lambda i: (0, i, 0)),
            pl.BlockSpec((b, S5_N), lambda i: (0, 0)),
            pl.BlockSpec((b, S5_N), lambda i: (0, 0)),
        ],
        out_shape=(jax.ShapeDtypeStruct((b, s_len, D_B), BF16), state, state),
        scratch_shapes=[
            pltpu.VMEM((rows, S5_N), F32),
            pltpu.VMEM((rows, S5_N), F32),
            pltpu.VMEM((b, S5_N), F32),
            pltpu.VMEM((b, S5_N), F32),
            pltpu.VMEM((rows, D_B), F32),
            pltpu.VMEM((rows, D_B), BF16),
            pltpu.VMEM((rows, D_B), F32),
        ],
        compiler_params=pltpu.CompilerParams(
            dimension_semantics=("arbitrary",), vmem_limit_bytes=VMEM_LIMIT),
        name="s5",
    )(u, s0re, s0im, lbre, lbim, wb, wcre, wcim, d_skip, perm, permt, wglu, bglu)


def _outproj_tile(a_ref, b_ref, x_ref, wo_ref, gpost_ref, gffn_ref, x1_ref, hf_ref, n_sub):
    for rows in _row_blocks(a_ref.shape[0], n_sub):
        mix = (_dot(a_ref[rows, :], wo_ref[0:D_A, :])
               + _dot(b_ref[rows, :], wo_ref[D_A:D_A + D_B, :]))
        x1 = x_ref[rows, :] + _rms(mix, gpost_ref[...])
        x1_ref[rows, :] = x1
        hf_ref[rows, :] = _rms(x1, gffn_ref[...]).astype(BF16)


def _out_proj(a, b, x2d, wo, g_post, g_ffn):
    t = x2d.shape[0]
    return pl.pallas_call(
        functools.partial(_outproj_tile, n_sub=1),
        out_shape=(jax.ShapeDtypeStruct((t, D_MODEL), F32),
                   jax.ShapeDtypeStruct((t, D_MODEL), BF16)),
        compiler_params=pltpu.CompilerParams(vmem_limit_bytes=VMEM_LIMIT),
        name="out_proj",
    )(a, b, x2d, wo, g_post, g_ffn)


def _mlstm_outproj_kernel(qk_ref, v_ref, og_ref, g_ref, gc_ref, rep_ref, tail0_ref, c0_ref, n0_ref,
                          m0_ref, wconv_ref, bconv_ref, gmh_ref, b_ref, x_ref, wo_ref, gpost_ref,
                          gffn_ref, *rest, chunk, tiles_per_stream, n_cast):
    cast_in = rest[:n_cast]
    x1_ref, hf_ref, convo_ref, co_ref, no_ref, mo_ref = rest[n_cast:n_cast + 6]
    cast_out = rest[n_cast + 6:2 * n_cast + 6]
    xp, ct, ncr, m_s, oa, mix = rest[2 * n_cast + 6:]
    L = chunk
    n_chunks = qk_ref.shape[0] // L
    tile = lax.rem(pl.program_id(0), tiles_per_stream)

    @pl.when(tile == 0)
    def _():
        _mlstm_load_state(xp, ct, ncr, m_s, tail0_ref[0], c0_ref.at[0], n0_ref.at[0], m0_ref[0])

    for src, dst in zip(cast_in, cast_out):
        dst[...] = src[...].astype(BF16)

    piece = D_MODEL // A_HEADS

    def project(c, h):
        rows = slice(L * c, L * (c + 1))
        cols = slice(piece * h, piece * (h + 1))
        mix[c % 2, :, cols] = (_dot(oa[rows, :], wo_ref[0:D_A, cols])
                               + _dot(b_ref[rows, :], wo_ref[D_A:D_A + D_B, cols]))

    def finish(c):
        rows = slice(L * c, L * (c + 1))
        x1 = x_ref[rows, :] + _rms(mix[c % 2], gpost_ref[...])
        x1_ref[rows, :] = x1
        hf_ref[rows, :] = _rms(x1, gffn_ref[...]).astype(BF16)

    for c in range(n_chunks):
        rows = slice(L * c, L * (c + 1))

        def store(h, x, rows=rows):
            oa[rows, _head_cols(h)] = x

        _mlstm_chunk(qk_ref[rows, :], g_ref[:, rows], gc_ref[rows, :],
                     lambda h, rows=rows: v_ref[rows, _head_cols(h)],
                     lambda h, rows=rows: og_ref[rows, _head_cols(h)],
                     store, xp, ct, ncr, m_s, rep_ref, wconv_ref, bconv_ref, gmh_ref,
                     after_head=(lambda h, c=c: project(c - 1, h)) if c > 0 else None)
        xp[0:SUBLANES, :] = xp[L:L + SUBLANES, :]
        if c > 0:
            finish(c - 1)
    for h in range(A_HEADS):
        project(n_chunks - 1, h)
    finish(n_chunks - 1)

    @pl.when(tile == tiles_per_stream - 1)
    def _():
        convo_ref[0] = xp[0:SUBLANES, :]
        _mlstm_store_state(ct, ncr, m_s, co_ref.at[0], no_ref.at[0], mo_ref.at[0])


def _mlstm_out_proj(qk, v, og, gates, gates_c, tail0, c0, n0, m0, wconv, bconv, gmh,
                    out_b, x2d, wo, g_post, g_ffn, *, b, chunk, tm, cast=()):
    t = x2d.shape[0]
    n_steps = t // tm
    tiles_per_stream = n_steps // b
    assert tiles_per_stream * b * tm == t and tm % chunk == 0
    rep = _gate_replication_matrix()
    tok = lambda i: (i, 0)
    const2 = lambda i: (0, 0)
    per_b3 = lambda i: (i // tiles_per_stream, 0, 0)
    per_b4 = lambda i: (i // tiles_per_stream, 0, 0, 0)
    cast_specs = [pl.BlockSpec((w.shape[0] // n_steps, w.shape[1]), tok) for w in cast]
    kernel = functools.partial(_mlstm_outproj_kernel, chunk=chunk,
                               tiles_per_stream=tiles_per_stream, n_cast=len(cast))
    return pl.pallas_call(
        kernel,
        grid=(n_steps,),
        in_specs=[
            pl.BlockSpec((tm, QK_W), tok),
            pl.BlockSpec((tm, D_A), tok),
            pl.BlockSpec((tm, D_A), tok),
            pl.BlockSpec((2 * SUBLANES, tm), lambda i: (0, i)),
            pl.BlockSpec((tm, LANES), tok),
            pl.BlockSpec(rep.shape, const2),
            pl.BlockSpec((1, SUBLANES, QK_W), per_b3),
            pl.BlockSpec((1, A_HEADS, A_DV, A_DQK), per_b4),
            pl.BlockSpec((1, A_HEADS, A_DQK), per_b3),
            pl.BlockSpec((1, SUBLANES, LANES), per_b3),
            pl.BlockSpec((CONV_W, QK_W), const2),
            pl.BlockSpec((1, QK_W), const2),
            pl.BlockSpec((1, D_A), const2),
            pl.BlockSpec((tm, D_B), tok),
            pl.BlockSpec((tm, D_MODEL), tok),
            _resident((D_A + D_B, D_MODEL)),
            _resident((1, D_MODEL)),
            _resident((1, D_MODEL)),
        ] + cast_specs,
        out_specs=[
            pl.BlockSpec((tm, D_MODEL), tok),
            pl.BlockSpec((tm, D_MODEL), tok),
            pl.BlockSpec((1, SUBLANES, QK_W), per_b3),
            pl.BlockSpec((1, A_HEADS, A_DV, A_DQK), per_b4),
            pl.BlockSpec((1, A_HEADS, A_DQK), per_b3),
            pl.BlockSpec((1, SUBLANES, LANES), per_b3),
        ] + cast_specs,
        out_shape=(
            jax.ShapeDtypeStruct((t, D_MODEL), F32),
            jax.ShapeDtypeStruct((t, D_MODEL), BF16),
            jax.ShapeDtypeStruct((b, SUBLANES, QK_W), F32),
            jax.ShapeDtypeStruct((b, A_HEADS, A_DV, A_DQK), F32),
            jax.ShapeDtypeStruct((b, A_HEADS, A_DQK), F32),
            jax.ShapeDtypeStruct((b, SUBLANES, LANES), F32),
        ) + tuple(jax.ShapeDtypeStruct(w.shape, BF16) for w in cast),
        scratch_shapes=[
            pltpu.VMEM((chunk + SUBLANES, QK_W), F32),
            pltpu.VMEM((A_HEADS, A_DQK, A_DV), F32),
            pltpu.VMEM((A_HEADS, A_DQK, LANES), F32),
            pltpu.VMEM((SUBLANES, LANES), F32),
            pltpu.VMEM((tm, D_A), BF16),
            pltpu.VMEM((2, chunk, D_MODEL), F32),
        ],
        compiler_params=pltpu.CompilerParams(
            dimension_semantics=("arbitrary",), vmem_limit_bytes=VMEM_LIMIT),
        name="mlstm_out_proj",
    )(qk, v, og, gates, gates_c, rep, tail0, c0, n0, m0, wconv, bconv, gmh,
      out_b, x2d, wo, g_post, g_ffn, *cast)


def _ffn_up_kernel(hfp_ref, hfs_ref, wg_ref, wu_ref, actp_ref, acts_ref, wgb, wub, *, n_sub):
    m = pl.program_id(1)

    @pl.when(m == 0)
    def _():
        wgb[...] = wg_ref[...].astype(BF16)
        wub[...] = wu_ref[...].astype(BF16)

    def body(hf_ref, act_ref, n_sub):
        for rows in _row_blocks(hf_ref.shape[0], n_sub):
            hf = hf_ref[rows, :]
            a = _dot(hf, wgb[...])
            act_ref[rows, :] = (a * _sigmoid(a) * _dot(hf, wub[...])).astype(BF16)

    @pl.when(m == 0)
    def _():
        body(hfs_ref, acts_ref, 1)

    @pl.when(m > 0)
    def _():
        body(hfp_ref, actp_ref, n_sub)


def _ffn_up(hf_p, hf_s, wg, wu, tm, tf, n_sub):
    t_p, t_s = hf_p.shape[0], hf_s.shape[0]
    n_p = t_p // tm
    n_f = D_FF // tf
    prompt_tile = lambda m: jnp.maximum(m - 1, 0)
    return pl.pallas_call(
        functools.partial(_ffn_up_kernel, n_sub=n_sub),
        grid=(n_f, n_p + 1),
        in_specs=[
            pl.BlockSpec((tm, D_MODEL), lambda f, m: (prompt_tile(m), 0)),
            pl.BlockSpec((t_s, D_MODEL), lambda f, m: (0, 0)),
            pl.BlockSpec((D_MODEL, tf), lambda f, m: (0, f)),
            pl.BlockSpec((D_MODEL, tf), lambda f, m: (0, f)),
        ],
        out_specs=[
            pl.BlockSpec((tm, tf), lambda f, m: (prompt_tile(m), f)),
            pl.BlockSpec((t_s, tf), lambda f, m: (0, f)),
        ],
        out_shape=(jax.ShapeDtypeStruct((t_p, D_FF), BF16),
                   jax.ShapeDtypeStruct((t_s, D_FF), BF16)),
        scratch_shapes=[pltpu.VMEM((D_MODEL, tf), BF16), pltpu.VMEM((D_MODEL, tf), BF16)],
        compiler_params=pltpu.CompilerParams(
            dimension_semantics=("arbitrary", "arbitrary"), vmem_limit_bytes=VMEM_LIMIT),
        name="ffn_up",
    )(hf_p, hf_s, wg, wu)


def _ffn_down_kernel(act_ref, wd_ref, x1_ref, gp_ref, out_ref, *, n_sub):
    for rows in _row_blocks(act_ref.shape[0], n_sub):
        ff = _dot(act_ref[rows, :], wd_ref[...])
        out_ref[rows, :] = x1_ref[rows, :] + _rms(ff, gp_ref[...])


def _ffn_down(act, wd, x1, g_post, tm, n_sub):
    t = act.shape[0]
    tok = lambda i: (i, 0)
    return pl.pallas_call(
        functools.partial(_ffn_down_kernel, n_sub=n_sub),
        grid=(t // tm,),
        in_specs=[
            pl.BlockSpec((tm, D_FF), tok),
            _resident((D_FF, D_MODEL)),
            pl.BlockSpec((tm, D_MODEL), tok),
            _resident((1, D_MODEL)),
        ],
        out_specs=pl.BlockSpec((tm, D_MODEL), tok),
        out_shape=jax.ShapeDtypeStruct((t, D_MODEL), F32),
        compiler_params=pltpu.CompilerParams(
            dimension_semantics=("arbitrary",), vmem_limit_bytes=VMEM_LIMIT),
        name="ffn_down",
    )(act, wd, x1, g_post)


def _block_diag_in(bbt):
    x = bbt.reshape(B_GROUP, N_BLOCKS, GROUPS_PER_BLOCK, B_STATE)
    x = jnp.transpose(x, (1, 2, 0, 3))
    eye = jnp.eye(GROUPS_PER_BLOCK, dtype=bool)[None, :, None, :, None]
    y = jnp.where(eye, x[:, :, :, None, :], 0.0)
    return y.reshape(N_BLOCKS, LANES, STATE_PER_BLOCK)


def _block_diag_out(c):
    x = c.reshape(N_BLOCKS, GROUPS_PER_BLOCK, B_GROUP, B_STATE)
    x = jnp.transpose(x, (0, 1, 3, 2))
    eye = jnp.eye(GROUPS_PER_BLOCK, dtype=bool)[None, :, None, :, None]
    y = jnp.where(eye, x[:, :, :, None, :], 0.0)
    return y.reshape(N_BLOCKS, STATE_PER_BLOCK, LANES)


def _row_permutation(batch):
    n = batch * PERM_T
    src = np.arange(n)
    b, t = src // PERM_T, src % PERM_T
    p = np.zeros((n, n), np.float32)
    p[t * batch + b, src] = 1.0
    return jnp.asarray(p, BF16), jnp.asarray(p.T, BF16)


def _pad_rows_front(x, rows):
    pad = [(0, 0)] * x.ndim
    pad[-2] = (rows - x.shape[-2], 0)
    return jnp.pad(x, pad)


def _pad_rows_back(x, rows):
    pad = [(0, 0)] * x.ndim
    pad[-2] = (0, rows - x.shape[-2])
    return jnp.pad(x, pad)


def _conv_state_rows(conv0):
    return _pad_rows_front(conv0, SUBLANES)


def _max_state_rows(m0):
    return jnp.broadcast_to(_pad_rows_back(m0[..., None], SUBLANES), m0.shape[:1] + (SUBLANES, LANES))


def _state_outputs(conv_o, c_o, n_o, m_o, sre_o, sim_o):
    b = conv_o.shape[0]
    return (
        conv_o[:, SUBLANES - (CONV_W - 1):, :],
        c_o,
        n_o,
        m_o[:, :A_HEADS, 0],
        sre_o.reshape(b, B_GROUPS, B_STATE),
        sim_o.reshape(b, B_GROUPS, B_STATE),
    )


def _mixers(proj, conv0, c0, n0, m0, sre0, sim0, p, *, b, chunk, tb):
    qk, v, og, u, gates, gates_c = proj
    t = qk.shape[0]
    s_len = t // b
    assert s_len % chunk == 0 or s_len < chunk
    seg = min(s_len, chunk)
    s_pad = -(-s_len // chunk) * chunk
    extra = s_pad - s_len
    def seq3(a, mode="constant"):
        a = a.reshape(b, s_len, a.shape[-1])
        return a if extra == 0 else jnp.pad(a, ((0, 0), (0, extra), (0, 0)), mode=mode)
    g3 = gates.reshape(2 * SUBLANES, b, s_len)
    if extra:
        pad_t = ((0, 0), (0, 0), (0, extra))
        g3 = jnp.concatenate([
            jnp.pad(g3[:SUBLANES], pad_t, constant_values=-jnp.inf),
            jnp.pad(g3[SUBLANES:], pad_t, mode="edge")], axis=0)
    out_a, conv_o, c_o, n_o, m_o = _mlstm(
        seq3(qk), seq3(v), seq3(og), g3.reshape(2 * SUBLANES, b * s_pad),
        seq3(gates_c, "edge"), _conv_state_rows(conv0), c0, n0, _max_state_rows(m0),
        p["w_conv"], p["b_conv"], p["g_mh"], chunk=chunk, valid=seg)
    out_a = out_a[:, :s_len].reshape(t, D_A)

    out_b, sre_o, sim_o = _s5(
        u.reshape(b, s_len, D_B), sre0.reshape(b, S5_N), sim0.reshape(b, S5_N),
        p["lbre"], p["lbim"], p["wb"], p["wcre"], p["wcim"], p["s5_d"],
        p["perm"], p["permt"], p["w_glu"], p["b_glu"], tb=tb)
    states = _state_outputs(conv_o, c_o, n_o, m_o, sre_o, sim_o)
    return out_a, out_b.reshape(t, D_B), states


def kernel(x_prompt, x_sample, state_conv, state_mlstm_c, state_mlstm_n, state_mlstm_m, state_s5_re, state_s5_im, g_pre_mix, w_in, w_conv, b_conv, b_igate, b_fgate, g_mh, s5_lam_re, s5_lam_im, s5_log_dt, s5_b_re, s5_b_im, s5_c_re, s5_c_im, s5_d, w_glu, b_glu, w_out, g_post_mix, g_pre_ffn, w_gate, w_up, w_down, g_post_ffn):
    l = 0
    bp = x_prompt.shape[0]
    lbre, lbim, bbt_re, bbt_im = _discretise(
        s5_lam_re[l], s5_lam_im[l], s5_log_dt[l],
        jnp.transpose(s5_b_re[l], (2, 0, 1)), jnp.transpose(s5_b_im[l], (2, 0, 1)))

    o1 = QK_W
    o2 = o1 + D_A
    o3 = o2 + D_A
    o4 = o3 + A_HEADS
    o5 = o4 + A_HEADS
    wt = jnp.swapaxes(w_in[l], 0, 1)
    perm, permt = _row_permutation(bp)
    row = lambda a: a.reshape(1, -1)
    p = {
        "g_pre_mix": row(g_pre_mix[l]),
        "w_in_t": wt[:o3].astype(BF16),
        "w_in_gt": jnp.pad(wt[o3:o5], ((0, 2 * SUBLANES - 2 * A_HEADS), (0, 0))).astype(BF16),
        "w_in_ut": wt[o5:].astype(BF16),
        "gbias": jnp.concatenate(
            [b_igate[l], b_fgate[l], jnp.zeros((SUBLANES,), F32)]).reshape(2 * SUBLANES, 1),
        "w_conv": w_conv[l],
        "b_conv": row(b_conv[l]),
        "g_mh": row(g_mh[l]),
        "lbre": lbre.reshape(1, S5_N),
        "lbim": lbim.reshape(1, S5_N),
        "wb": jnp.concatenate([_block_diag_in(bbt_re), _block_diag_in(bbt_im)], axis=-1).astype(BF16),
        "wcre": _block_diag_out(s5_c_re[l]).astype(BF16),
        "wcim": _block_diag_out(s5_c_im[l]).astype(BF16),
        "s5_d": row(s5_d[l]),
        "perm": perm,
        "permt": permt,
        "b_glu": row(b_glu[l]),
        "g_post_mix": row(g_post_mix[l]),
        "g_pre_ffn": row(g_pre_ffn[l]),
    }

    bs, s_sample = x_sample.shape[0], x_sample.shape[1]
    xp2d = x_prompt.reshape(-1, D_MODEL)
    xs2d = x_sample.reshape(-1, D_MODEL)
    chunk = 128
    proj = _in_proj(xp2d, xs2d, p, tm=512, n_sub=2, seg_p=chunk, seg_s=min(s_sample, chunk),
                    cast=(w_glu[l], w_out[l]))
    proj_p, proj_s = proj[:N_PROJ_OUT], proj[N_PROJ_OUT:2 * N_PROJ_OUT]
    p["w_glu"], p["w_out"] = proj[2 * N_PROJ_OUT:]

    qk_p, v_p, og_p, u_p, gates_p, gates_cp = proj_p
    zero_s5 = jnp.zeros((bp, S5_N), F32)
    b_p, sre_p, sim_p = _s5(
        u_p.reshape(bp, -1, D_B), zero_s5, zero_s5,
        p["lbre"], p["lbim"], p["wb"], p["wcre"], p["wcim"], p["s5_d"],
        p["perm"], p["permt"], p["w_glu"], p["b_glu"], tb=64)
    x1_p, hf_p, conv_p, c_p, n_p, m_p, wd = _mlstm_out_proj(
        qk_p, v_p, og_p, gates_p, gates_cp,
        _conv_state_rows(jnp.zeros((bp, CONV_W - 1, QK_W), F32)),
        jnp.zeros((bp, A_HEADS, A_DV, A_DQK), F32),
        jnp.zeros((bp, A_HEADS, A_DQK), F32),
        _max_state_rows(jnp.zeros((bp, A_HEADS), F32)),
        p["w_conv"], p["b_conv"], p["g_mh"],
        b_p.reshape(-1, D_B), xp2d, p["w_out"], p["g_post_mix"], p["g_pre_ffn"],
        b=bp, chunk=chunk, tm=512, cast=(w_down[l],))
    st_p = _state_outputs(conv_p, c_p, n_p, m_p, sre_p, sim_p)

    a_s, b_s, st_s = _mixers(
        proj_s, state_conv[l], state_mlstm_c[l], state_mlstm_n[l], state_mlstm_m[l],
        state_s5_re[l], state_s5_im[l], p, b=bs, chunk=chunk, tb=s_sample)
    x1_s, hf_s = _out_proj(a_s, b_s, xs2d, p["w_out"], p["g_post_mix"], p["g_pre_ffn"])

    act_p, act_s = _ffn_up(hf_p, hf_s, w_gate[l], w_up[l], tm=2048, tf=512, n_sub=2)
    g_post = row(g_post_ffn[l])
    yp = _ffn_down(act_p, wd, x1_p, g_post, tm=512, n_sub=2).reshape(x_prompt.shape)
    ys = _ffn_down(act_s, wd, x1_s, g_post, tm=128, n_sub=1).reshape(x_sample.shape)
    return (yp, ys) + tuple(a[None] for a in st_p) + tuple(a[None] for a in st_s)
```

```python
import functools

import numpy as np
import jax
import jax.numpy as jnp
from jax import lax
from jax.experimental import pallas as pl
from jax.experimental.pallas import tpu as pltpu

F32 = jnp.float32
BF16 = jnp.bfloat16

D_MODEL = 2048
D_A = 1024
D_B = 1024
A_HEADS = 4
A_DV = 256
A_DQK = 128
QK_W = 1024
CONV_W = 4
B_GROUP = 16
B_GROUPS = 64
B_STATE = 64
S5_N = B_GROUPS * B_STATE
D_FF = 5632
EPS = 1e-6

SUBLANES = 8
LANES = 128
GROUPS_PER_BLOCK = LANES // B_GROUP
N_BLOCKS = B_GROUPS // GROUPS_PER_BLOCK
STATE_PER_BLOCK = GROUPS_PER_BLOCK * B_STATE
PERM_T = 16
VMEM_LIMIT = 56 * 1024 * 1024


def _sigmoid(x):
    return 1.0 / (1.0 + jnp.exp(-x))


def _gelu_tanh(x):
    c = np.sqrt(2 / np.pi).astype(np.float32)
    return x * (0.5 * (1.0 + jnp.tanh(c * (x + 0.044715 * (x ** 3)))))


def _rms(x, g):
    return x * lax.rsqrt(jnp.mean(x * x, axis=-1, keepdims=True) + EPS) * g


def _dot(a, b):
    return jnp.dot(a, b, preferred_element_type=F32)


def _log_sigmoid(z):
    return jnp.minimum(z, 0.0) - jnp.log1p(jnp.exp(-jnp.abs(z)))


def _split3(x):
    hi = x.astype(BF16)
    r = x - hi.astype(F32)
    mid = r.astype(BF16)
    lo = (r - mid.astype(F32)).astype(BF16)
    return hi, mid, lo


def _dot3_right(x, sel):
    hi, mid, lo = _split3(x)
    return _dot(hi, sel) + _dot(mid, sel) + _dot(lo, sel)


def _dot3_left(sel, x):
    hi, mid, lo = _split3(x)
    return _dot(sel, hi) + _dot(sel, mid) + _dot(sel, lo)


def _row_blocks(rows, n_sub):
    assert rows % n_sub == 0
    size = rows // n_sub
    return [slice(size * r, size * (r + 1)) for r in range(n_sub)]


def _resident(shape):
    zeros = (0,) * len(shape)
    return pl.BlockSpec(shape, lambda *_: zeros, pipeline_mode=pl.Buffered(1))


def _disc_kernel(lam_re_ref, lam_im_ref, logdt_ref, bre_ref, bim_ref,
                 lbre_ref, lbim_ref, bbre_ref, bbim_ref):
    lam_re = lam_re_ref[...]
    lam_im = lam_im_ref[...]
    dt = jnp.exp(logdt_ref[...])
    mag = jnp.exp(lam_re * dt)
    ang = lam_im * dt
    lb_re = mag * jnp.cos(ang)
    lb_im = mag * jnp.sin(ang)
    den = lam_re * lam_re + lam_im * lam_im
    nr = lb_re - 1.0
    f_re = (nr * lam_re + lb_im * lam_im) / den
    f_im = (lb_im * lam_re - nr * lam_im) / den
    lbre_ref[...] = lb_re
    lbim_ref[...] = lb_im
    b_re = bre_ref[...]
    b_im = bim_ref[...]
    bbre_ref[...] = f_re[None] * b_re - f_im[None] * b_im
    bbim_ref[...] = f_re[None] * b_im + f_im[None] * b_re


def _discretise(lam_re, lam_im, log_dt, bt_re, bt_im):
    gp = jax.ShapeDtypeStruct((B_GROUPS, B_STATE), F32)
    igp = jax.ShapeDtypeStruct((B_GROUP, B_GROUPS, B_STATE), F32)
    return pl.pallas_call(
        _disc_kernel, out_shape=(gp, gp, igp, igp), name="s5_discretise",
    )(lam_re, lam_im, log_dt.reshape(B_GROUPS, 1), bt_re, bt_im)


GATE_COL = QK_W + 2 * D_A
GATE_LANE_CMX = A_HEADS
GATE_LANE_FC = 2 * A_HEADS


def _dot_t(a, bt):
    return lax.dot_general(a, bt, (((1,), (1,)), ((), ())), preferred_element_type=F32)


def _inproj_tile(x_ref, gpre_ref, wt_ref, wut_ref, wgt_ref, gb_ref, triu_ref, tril_ref, outs,
                 n_sub, seg):
    qk_out, v_out, og_out, u_out, g_out, c_out = outs
    for rows in _row_blocks(x_ref.shape[0], n_sub):
        hm = _rms(x_ref[rows, :], gpre_ref[...]).astype(BF16)

        n = rows.stop - rows.start
        z = _dot_t(wgt_ref[...], hm) + gb_ref[...]
        qk_out[rows, :] = _dot_t(hm, wt_ref[0:QK_W, :])

        row = lax.broadcasted_iota(jnp.int32, z.shape, 0)
        zf = jnp.where((row >= A_HEADS) & (row < 2 * A_HEADS), _log_sigmoid(z), z)
        zf8 = zf[0:SUBLANES]
        zc = jnp.concatenate([zf8, jnp.zeros((LANES - SUBLANES, n), F32)], axis=0).T
        lane = lax.broadcasted_iota(jnp.int32, zc.shape, 1)
        lf_c = jnp.where((lane >= A_HEADS) & (lane < 2 * A_HEADS), zc, 0.0)
        fc = pltpu.roll(_dot3_right(zf, triu_ref[...])[0:SUBLANES], A_HEADS, axis=0)
        fc_c = _dot3_left(tril_ref[...], lf_c)
        v_out[rows, :] = _dot_t(hm, wt_ref[QK_W:QK_W + D_A, :]).astype(BF16)
        og_out[rows, :] = _dot_t(hm, wt_ref[QK_W + D_A:QK_W + 2 * D_A, :])
        u_out[rows, :] = _dot_t(hm, wut_ref[...])

        g_out[0:SUBLANES, rows] = zf8 - fc
        g_out[SUBLANES:2 * SUBLANES, rows] = fc
        cm = pltpu.roll(zc, A_HEADS, axis=1) - fc_c
        t_in = lax.broadcasted_iota(jnp.int32, zc.shape, 0) & (seg - 1)
        sh = 1
        while sh < seg:
            cm = jnp.where(t_in >= sh, jnp.maximum(cm, pltpu.roll(cm, sh, axis=0)), cm)
            sh *= 2
        c_out[rows, :] = jnp.where(lane < GATE_LANE_FC, cm, pltpu.roll(fc_c, A_HEADS, axis=1))


N_PROJ_OUT = 6


def _inproj_kernel(xp_ref, xs_ref, gpre_ref, wt_ref, wut_ref, wgt_ref, gb_ref,
                   triu_p, tril_p, triu_s, tril_s, *rest, cfg_p, cfg_s, n_cast, n_p):
    cast_in = rest[:n_cast]
    outs_p = rest[n_cast:n_cast + N_PROJ_OUT]
    outs_s = rest[n_cast + N_PROJ_OUT:n_cast + 2 * N_PROJ_OUT]
    cast_out = rest[n_cast + 2 * N_PROJ_OUT:]
    shared = (gpre_ref, wt_ref, wut_ref, wgt_ref, gb_ref)
    i = pl.program_id(0)

    @pl.when(i < n_p)
    def _():
        for src, dst in zip(cast_in, cast_out):
            dst[...] = src[...].astype(BF16)
        _inproj_tile(xp_ref, *shared, triu_p, tril_p, outs_p, *cfg_p)

    @pl.when(i == n_p)
    def _():
        _inproj_tile(xs_ref, *shared, triu_s, tril_s, outs_s, *cfg_s)


def _segment_prefix_matrices(n, seg):
    idx = np.arange(n)
    u = ((idx[:, None] // seg == idx[None, :] // seg) & (idx[:, None] <= idx[None, :]))
    u = u.astype(np.float32)
    return jnp.asarray(u, BF16), jnp.asarray(u.T, BF16)


def _in_proj(xp2d, xs2d, p, *, tm, n_sub, seg_p, seg_s, cast=()):
    t_p, t_s = xp2d.shape[0], xs2d.shape[0]
    n_p = t_p // tm
    assert seg_p & (seg_p - 1) == 0 and seg_s & (seg_s - 1) == 0
    triu_p, tril_p = _segment_prefix_matrices(tm // n_sub, seg_p)
    triu_s, tril_s = _segment_prefix_matrices(t_s, seg_s)
    tile = lambda i: jnp.minimum(i, n_p - 1)
    tok = lambda i: (tile(i), 0)
    once = lambda i: (0, 0)
    cast_specs = [pl.BlockSpec((w.shape[0] // n_p, w.shape[1]), tok) for w in cast]

    def out_specs(rows, row_map, gate_map):
        return [
            pl.BlockSpec((rows, QK_W), row_map),
            pl.BlockSpec((rows, D_A), row_map),
            pl.BlockSpec((rows, D_A), row_map),
            pl.BlockSpec((rows, D_B), row_map),
            pl.BlockSpec((2 * SUBLANES, rows), gate_map),
            pl.BlockSpec((rows, LANES), row_map),
        ]

    def out_shapes(t):
        return (
            jax.ShapeDtypeStruct((t, QK_W), F32),
            jax.ShapeDtypeStruct((t, D_A), BF16),
            jax.ShapeDtypeStruct((t, D_A), F32),
            jax.ShapeDtypeStruct((t, D_B), F32),
            jax.ShapeDtypeStruct((2 * SUBLANES, t), F32),
            jax.ShapeDtypeStruct((t, LANES), F32),
        )

    kernel = functools.partial(_inproj_kernel, cfg_p=(n_sub, seg_p), cfg_s=(1, seg_s),
                               n_cast=len(cast), n_p=n_p)
    return pl.pallas_call(
        kernel,
        grid=(n_p + 1,),
        in_specs=[
            pl.BlockSpec((tm, D_MODEL), tok),
            pl.BlockSpec((t_s, D_MODEL), once),
            _resident((1, D_MODEL)),
            _resident((GATE_COL, D_MODEL)),
            _resident((D_B, D_MODEL)),
            _resident((2 * SUBLANES, D_MODEL)),
            _resident((2 * SUBLANES, 1)),
            _resident(triu_p.shape),
            _resident(tril_p.shape),
            _resident(triu_s.shape),
            _resident(tril_s.shape),
        ] + cast_specs,
        out_specs=(out_specs(tm, tok, lambda i: (0, tile(i))) + out_specs(t_s, once, once)
                   + cast_specs),
        out_shape=(out_shapes(t_p) + out_shapes(t_s)
                   + tuple(jax.ShapeDtypeStruct(w.shape, BF16) for w in cast)),
        compiler_params=pltpu.CompilerParams(
            dimension_semantics=("arbitrary",), vmem_limit_bytes=VMEM_LIMIT),
        name="in_proj",
    )(xp2d, xs2d, p["g_pre_mix"], p["w_in_t"], p["w_in_ut"], p["w_in_gt"], p["gbias"],
      triu_p, tril_p, triu_s, tril_s, *cast)


def _head_cols(h):
    return slice(A_DV * h, A_DV * (h + 1))


def _mlstm_chunk(qk_raw, g, gc, v_at, og_at, out_at, xp, ct, ncr, m_s,
                 rep_ref, wconv_ref, bconv_ref, gmh_ref):
    L = qk_raw.shape[0]
    reps = L // LANES

    def lanes(x, n):
        return x if n == 1 else jnp.concatenate([x] * n, axis=1)

    row_i = lax.broadcasted_iota(jnp.int32, (L, L), 0)
    col_i = lax.broadcasted_iota(jnp.int32, (L, L), 1)
    tri = col_i <= row_i
    lane = lax.broadcasted_iota(jnp.int32, (SUBLANES, L), 1)

    xp[SUBLANES:SUBLANES + L, :] = qk_raw
    xs = xp[...]
    assert CONV_W == 4
    x1 = pltpu.roll(xs, 1, axis=0)
    near = xs * wconv_ref[3:4, :] + x1 * wconv_ref[2:3, :]
    far = xs * wconv_ref[1:2, :] + x1 * wconv_ref[0:1, :]
    acc = (near + pltpu.roll(far, 2, axis=0))[SUBLANES:, :] + bconv_ref[...]
    qk = acc * _sigmoid(acc)
    q_all = (qk[:, :QK_W // 2] * (A_DQK ** -0.5)).astype(BF16)
    k_all = qk[:, QK_W // 2:].astype(BF16)

    rep = _dot3_right(gc, rep_ref[...])

    gr = g[0:SUBLANES]
    fc = g[SUBLANES:2 * SUBLANES]
    f_last = jnp.sum(jnp.where(lane == L - 1, fc, 0.0), axis=1, keepdims=True)
    m_prev = m_s[...]
    m0 = jnp.max(m_prev, axis=1, keepdims=True)
    lw = f_last + gr
    m_new = jnp.maximum(f_last + m0, jnp.max(lw, axis=1, keepdims=True))
    ws = jnp.exp(lw - m_new)
    decay = jnp.broadcast_to(jnp.exp(f_last + m0 - m_new), (SUBLANES, LANES))

    for h in range(A_HEADS):
        qb = q_all[:, A_DQK * h:A_DQK * (h + 1)]
        kb = k_all[:, A_DQK * h:A_DQK * (h + 1)]
        vb = v_at(h)
        m0_h = m_prev[h:h + 1, :]
        mx = jnp.maximum(rep[:, LANES * h:LANES * (h + 1)], m0_h)
        fc_t = rep[:, LANES * (A_HEADS + h):LANES * (A_HEADS + h + 1)]
        e = jnp.exp(jnp.where(tri, gr[h:h + 1, :] - lanes(mx, reps), -jnp.inf))
        s = lax.dot_general(qb, kb, (((1,), (1,)), ((), ())),
                            preferred_element_type=F32) * e
        g_int = jnp.exp(m0_h - mx)
        ct_h = ct[h]
        ncr_h = ncr[h]
        num = (_dot(s.astype(BF16), vb)
               + lanes(g_int, A_DV // LANES) * _dot(qb, ct_h.astype(BF16)))
        den = jnp.sum(s, axis=1, keepdims=True) + g_int * _dot(qb, ncr_h.astype(BF16))
        scale = jnp.maximum(jnp.abs(den), jnp.exp(-(fc_t + mx)))
        hh = num / lanes(scale, A_DV // LANES)
        hn = _rms(hh, gmh_ref[:, _head_cols(h)])
        out_at(h, (hn * _sigmoid(og_at(h))).astype(BF16))
        kws = kb.astype(F32).T * ws[h:h + 1, :]
        dec_h = decay[h:h + 1, :]
        ct[h] = lanes(dec_h, A_DV // LANES) * ct_h + _dot(kws.astype(BF16), vb)
        ncr[h] = dec_h * ncr_h + jnp.sum(kws, axis=1, keepdims=True)
    m_s[...] = jnp.broadcast_to(m_new, (SUBLANES, LANES))


def _mlstm_load_state(xp, ct, ncr, m_s, tail0, c0_ref, n0_ref, m0):
    xp[0:SUBLANES, :] = tail0
    for h in range(A_HEADS):
        ct[h] = c0_ref[h].T
        ncr[h] = jnp.broadcast_to(n0_ref[h:h + 1, :], (A_DQK, A_DQK)).T
    m_s[...] = m0


def _mlstm_store_state(ct, ncr, m_s, co_ref, no_ref, mo_ref):
    for h in range(A_HEADS):
        co_ref[h] = ct[h].T
        no_ref[h:h + 1, :] = ncr[h].T[0:1, :]
    mo_ref[...] = m_s[...]


def _mlstm_kernel(qk_ref, v_ref, og_ref, g_ref, gc_ref, rep_ref, tail0_ref, c0_ref, n0_ref, m0_ref,
                  wconv_ref, bconv_ref, gmh_ref, outa_ref, convo_ref, co_ref, no_ref, mo_ref,
                  xp, ct, ncr, m_s, *, chunk, valid, tiles_per_stream):
    L = chunk
    n_chunks = qk_ref.shape[0] // L
    assert SUBLANES <= valid <= L
    tile = lax.rem(pl.program_id(0), tiles_per_stream)

    @pl.when(tile == 0)
    def _():
        _mlstm_load_state(xp, ct, ncr, m_s, tail0_ref[0], c0_ref.at[0], n0_ref.at[0], m0_ref[0])

    for c in range(n_chunks):
        rows = slice(L * c, L * (c + 1))

        def store(h, x, rows=rows):
            outa_ref[rows, _head_cols(h)] = x

        _mlstm_chunk(qk_ref[rows, :], g_ref[:, rows], gc_ref[rows, :],
                     lambda h, rows=rows: v_ref[rows, _head_cols(h)],
                     lambda h, rows=rows: og_ref[rows, _head_cols(h)],
                     store, xp, ct, ncr, m_s, rep_ref, wconv_ref, bconv_ref, gmh_ref)
        xp[0:SUBLANES, :] = xp[L:L + SUBLANES, :]

    @pl.when(tile == tiles_per_stream - 1)
    def _():
        convo_ref[0] = xp[valid:valid + SUBLANES, :]
        _mlstm_store_state(ct, ncr, m_s, co_ref.at[0], no_ref.at[0], mo_ref.at[0])


def _gate_replication_matrix():
    r = np.zeros((LANES, 2 * A_HEADS * LANES), np.float32)
    for h in range(A_HEADS):
        r[GATE_LANE_CMX + h, LANES * h:LANES * (h + 1)] = 1.0
        r[GATE_LANE_FC + h, LANES * (A_HEADS + h):LANES * (A_HEADS + h + 1)] = 1.0
    return jnp.asarray(r, BF16)


def _mlstm(qk, v, og, gates, gates_c, tail0, c0, n0, m0, wconv, bconv, gmh, *, b, chunk, tm, valid):
    t = qk.shape[0]
    n_steps = t // tm
    tiles_per_stream = n_steps // b
    assert tiles_per_stream * b * tm == t and tm % chunk == 0
    rep = _gate_replication_matrix()
    tok = lambda i: (i, 0)
    const2 = lambda i: (0, 0)
    per_b3 = lambda i: (i // tiles_per_stream, 0, 0)
    per_b4 = lambda i: (i // tiles_per_stream, 0, 0, 0)
    kernel = functools.partial(_mlstm_kernel, chunk=chunk, valid=valid,
                               tiles_per_stream=tiles_per_stream)
    return pl.pallas_call(
        kernel,
        grid=(n_steps,),
        in_specs=[
            pl.BlockSpec((tm, QK_W), tok),
            pl.BlockSpec((tm, D_A), tok),
            pl.BlockSpec((tm, D_A), tok),
            pl.BlockSpec((2 * SUBLANES, tm), lambda i: (0, i)),
            pl.BlockSpec((tm, LANES), tok),
            pl.BlockSpec(rep.shape, const2),
            pl.BlockSpec((1, SUBLANES, QK_W), per_b3),
            pl.BlockSpec((1, A_HEADS, A_DV, A_DQK), per_b4),
            pl.BlockSpec((1, A_HEADS, A_DQK), per_b3),
            pl.BlockSpec((1, SUBLANES, LANES), per_b3),
            pl.BlockSpec((CONV_W, QK_W), const2),
            pl.BlockSpec((1, QK_W), const2),
            pl.BlockSpec((1, D_A), const2),
        ],
        out_specs=[
            pl.BlockSpec((tm, D_A), tok),
            pl.BlockSpec((1, SUBLANES, QK_W), per_b3),
            pl.BlockSpec((1, A_HEADS, A_DV, A_DQK), per_b4),
            pl.BlockSpec((1, A_HEADS, A_DQK), per_b3),
            pl.BlockSpec((1, SUBLANES, LANES), per_b3),
        ],
        out_shape=(
            jax.ShapeDtypeStruct((t, D_A), BF16),
            jax.ShapeDtypeStruct((b, SUBLANES, QK_W), F32),
            jax.ShapeDtypeStruct((b, A_HEADS, A_DV, A_DQK), F32),
            jax.ShapeDtypeStruct((b, A_HEADS, A_DQK), F32),
            jax.ShapeDtypeStruct((b, SUBLANES, LANES), F32),
        ),
        scratch_shapes=[
            pltpu.VMEM((chunk + SUBLANES, QK_W), F32),
            pltpu.VMEM((A_HEADS, A_DQK, A_DV), F32),
            pltpu.VMEM((A_HEADS, A_DQK, LANES), F32),
            pltpu.VMEM((SUBLANES, LANES), F32),
        ],
        compiler_params=pltpu.CompilerParams(
            dimension_semantics=("arbitrary",), vmem_limit_bytes=VMEM_LIMIT),
        name="mlstm",
    )(qk, v, og, gates, gates_c, rep, tail0, c0, n0, m0, wconv, bconv, gmh)


def _s5_kernel(u_ref, s0re_ref, s0im_ref, lbre_ref, lbim_ref, wb_ref, wcre_ref, wcim_ref,
               d_ref, perm_ref, permt_ref, wglu_ref, bglu_ref,
               outb_ref, sre_out, sim_out,
               bure, buim, sre, sim, utb, xtb, ytb, *, tb, n_steps):
    step = pl.program_id(0)
    batch = u_ref.shape[0]
    rows_j = batch * PERM_T
    n_j = tb // PERM_T

    @pl.when(step == 0)
    def _():
        sre[...] = s0re_ref[...]
        sim[...] = s0im_ref[...]

    strip = 8 * LANES
    blocks_per_strip = strip // STATE_PER_BLOCK
    n_strips = S5_N // strip

    def permute_in():
        for j in range(n_j):
            xj = jnp.concatenate(
                [u_ref[b, PERM_T * j:PERM_T * (j + 1), :] for b in range(batch)], axis=0)
            hi = xj.astype(BF16)
            lo = (xj - hi.astype(F32)).astype(BF16)
            uh = _dot(perm_ref[...], hi)
            rows = slice(rows_j * j, rows_j * (j + 1))
            xtb[rows, :] = uh.astype(BF16)
            utb[rows, :] = uh + _dot(perm_ref[...], lo)

    def input_map(st):
        for k in range(blocks_per_strip * st, blocks_per_strip * (st + 1)):
            r = _dot(xtb[:, LANES * k:LANES * (k + 1)], wb_ref[k])
            kc = slice(STATE_PER_BLOCK * k, STATE_PER_BLOCK * (k + 1))
            bure[:, kc] = r[:, :STATE_PER_BLOCK]
            buim[:, kc] = r[:, STATE_PER_BLOCK:]

    def recurrence(st):
        cols = slice(strip * st, strip * (st + 1))
        a_re = jnp.broadcast_to(lbre_ref[:, cols], (batch, strip))
        a_im = jnp.broadcast_to(lbim_ref[:, cols], (batch, strip))
        s_r = sre[:, cols]
        s_i = sim[:, cols]
        for t in range(tb):
            tr = slice(t * batch, (t + 1) * batch)
            n_r = a_re * s_r - a_im * s_i + bure[tr, cols]
            n_i = a_re * s_i + a_im * s_r + buim[tr, cols]
            bure[tr, cols] = n_r
            buim[tr, cols] = n_i
            s_r, s_i = n_r, n_i
        sre[:, cols] = s_r
        sim[:, cols] = s_i

    def output_map(st):
        for k in range(blocks_per_strip * st, blocks_per_strip * (st + 1)):
            kc = slice(STATE_PER_BLOCK * k, STATE_PER_BLOCK * (k + 1))
            ytb[:, LANES * k:LANES * (k + 1)] = (
                _dot(bure[:, kc].astype(BF16), wcre_ref[k])
                - _dot(buim[:, kc].astype(BF16), wcim_ref[k]))

    permute_in()
    input_map(0)
    for st in range(n_strips):
        if st + 1 < n_strips:
            input_map(st + 1)
        recurrence(st)
        output_map(st)

    z = _gelu_tanh(ytb[...] + d_ref[...] * utb[...])
    gate = _sigmoid(_dot(z.astype(BF16), wglu_ref[...]) + bglu_ref[...])
    res = (z * gate).astype(BF16)
    for j in range(n_j):
        rb = _dot(permt_ref[...], res[rows_j * j:rows_j * (j + 1), :]).astype(BF16)
        for b in range(batch):
            outb_ref[b, PERM_T * j:PERM_T * (j + 1), :] = rb[PERM_T * b:PERM_T * (b + 1), :]

    @pl.when(step == n_steps - 1)
    def _():
        sre_out[...] = sre[...]
        sim_out[...] = sim[...]


def _s5(u, s0re, s0im, lbre, lbim, wb, wcre, wcim, d_skip, perm, permt, wglu, bglu, *, tb):
    b, s_len, _ = u.shape
    n_steps = s_len // tb
    rows = b * tb
    kernel = functools.partial(_s5_kernel, tb=tb, n_steps=n_steps)
    state = jax.ShapeDtypeStruct((b, S5_N), F32)
    return pl.pallas_call(
        kernel,
        grid=(n_steps,),
        in_specs=[
            pl.BlockSpec((b, tb, D_B), lambda i: (0, i, 0)),
            _resident((b, S5_N)),
            _resident((b, S5_N)),
            _resident((1, S5_N)),
            _resident((1, S5_N)),
            _resident((N_BLOCKS, LANES, 2 * STATE_PER_BLOCK)),
            _resident((N_BLOCKS, STATE_PER_BLOCK, LANES)),
            _resident((N_BLOCKS, STATE_PER_BLOCK, LANES)),
            _resident((1, D_B)),
            _resident((b * PERM_T, b * PERM_T)),
            _resident((b * PERM_T, b * PERM_T)),
            _resident((D_B, D_B)),
            _resident((1, D_B)),
        ],
        out_specs=[
            pl.BlockSpec((b, tb, D_B), lambda i: (0, i, 0)),
            pl.BlockSpec((b, S5_N), lambda i: (0, 0)),
            pl.BlockSpec((b, S5_N), lambda i: (0, 0)),
        ],
        out_shape=(jax.ShapeDtypeStruct((b, s_len, D_B), BF16), state, state),
        scratch_shapes=[
            pltpu.VMEM((rows, S5_N), F32),
            pltpu.VMEM((rows, S5_N), F32),
            pltpu.VMEM((b, S5_N), F32),
            pltpu.VMEM((b, S5_N), F32),
            pltpu.VMEM((rows, D_B), F32),
            pltpu.VMEM((rows, D_B), BF16),
            pltpu.VMEM((rows, D_B), F32),
        ],
        compiler_params=pltpu.CompilerParams(
            dimension_semantics=("arbitrary",), vmem_limit_bytes=VMEM_LIMIT),
        name="s5",
    )(u, s0re, s0im, lbre, lbim, wb, wcre, wcim, d_skip, perm, permt, wglu, bglu)


def _outproj_tile(a_ref, b_ref, x_ref, wo_ref, gpost_ref, gffn_ref, x1_ref, hf_ref, n_sub):
    for rows in _row_blocks(a_ref.shape[0], n_sub):
        mix = (_dot(a_ref[rows, :], wo_ref[0:D_A, :])
               + _dot(b_ref[rows, :], wo_ref[D_A:D_A + D_B, :]))
        x1 = x_ref[rows, :] + _rms(mix, gpost_ref[...])
        x1_ref[rows, :] = x1
        hf_ref[rows, :] = _rms(x1, gffn_ref[...]).astype(BF16)


def _outproj_kernel(ap_ref, bp_ref, xp_ref, as_ref, bs_ref, xs_ref, wo_ref, gpost_ref, gffn_ref,
                    *rest, n_sub, n_cast, n_p):
    cast_in = rest[:n_cast]
    x1p_ref, hfp_ref, x1s_ref, hfs_ref = rest[n_cast:n_cast + 4]
    cast_out = rest[n_cast + 4:]
    shared = (wo_ref, gpost_ref, gffn_ref)
    i = pl.program_id(0)

    @pl.when(i < n_p)
    def _():
        for src, dst in zip(cast_in, cast_out):
            dst[...] = src[...].astype(BF16)
        _outproj_tile(ap_ref, bp_ref, xp_ref, *shared, x1p_ref, hfp_ref, n_sub)

    @pl.when(i == n_p)
    def _():
        _outproj_tile(as_ref, bs_ref, xs_ref, *shared, x1s_ref, hfs_ref, 1)


def _out_proj(prompt, sample, wo, g_post, g_ffn, tm, n_sub, cast=()):
    t_p, t_s = prompt[2].shape[0], sample[2].shape[0]
    n_p = t_p // tm
    tok = lambda i: (jnp.minimum(i, n_p - 1), 0)
    once = lambda i: (0, 0)
    cast_specs = [pl.BlockSpec((w.shape[0] // n_p, w.shape[1]), tok) for w in cast]
    rows_specs = lambda rows, m: [pl.BlockSpec((rows, D_A), m), pl.BlockSpec((rows, D_B), m),
                                  pl.BlockSpec((rows, D_MODEL), m)]
    out_specs = lambda rows, m: [pl.BlockSpec((rows, D_MODEL), m), pl.BlockSpec((rows, D_MODEL), m)]
    out_shapes = lambda t: (jax.ShapeDtypeStruct((t, D_MODEL), F32),
                            jax.ShapeDtypeStruct((t, D_MODEL), BF16))
    return pl.pallas_call(
        functools.partial(_outproj_kernel, n_sub=n_sub, n_cast=len(cast), n_p=n_p),
        grid=(n_p + 1,),
        in_specs=rows_specs(tm, tok) + rows_specs(t_s, once) + [
            _resident((D_A + D_B, D_MODEL)),
            _resident((1, D_MODEL)),
            _resident((1, D_MODEL)),
        ] + cast_specs,
        out_specs=out_specs(tm, tok) + out_specs(t_s, once) + cast_specs,
        out_shape=(out_shapes(t_p) + out_shapes(t_s)
                   + tuple(jax.ShapeDtypeStruct(w.shape, BF16) for w in cast)),
        compiler_params=pltpu.CompilerParams(
            dimension_semantics=("arbitrary",), vmem_limit_bytes=VMEM_LIMIT),
        name="out_proj",
    )(*prompt, *sample, wo, g_post, g_ffn, *cast)


def _ffn_up_kernel(hfp_ref, hfs_ref, wg_ref, wu_ref, actp_ref, acts_ref, wgb, wub, *, n_sub):
    m = pl.program_id(1)

    @pl.when(m == 0)
    def _():
        wgb[...] = wg_ref[...].astype(BF16)
        wub[...] = wu_ref[...].astype(BF16)

    def body(hf_ref, act_ref, n_sub):
        for rows in _row_blocks(hf_ref.shape[0], n_sub):
            hf = hf_ref[rows, :]
            a = _dot(hf, wgb[...])
            act_ref[rows, :] = (a * _sigmoid(a) * _dot(hf, wub[...])).astype(BF16)

    @pl.when(m == 0)
    def _():
        body(hfs_ref, acts_ref, 1)

    @pl.when(m > 0)
    def _():
        body(hfp_ref, actp_ref, n_sub)


def _ffn_up(hf_p, hf_s, wg, wu, tm, tf, n_sub):
    t_p, t_s = hf_p.shape[0], hf_s.shape[0]
    n_p = t_p // tm
    n_f = D_FF // tf
    prompt_tile = lambda m: jnp.maximum(m - 1, 0)
    return pl.pallas_call(
        functools.partial(_ffn_up_kernel, n_sub=n_sub),
        grid=(n_f, n_p + 1),
        in_specs=[
            pl.BlockSpec((tm, D_MODEL), lambda f, m: (prompt_tile(m), 0)),
            pl.BlockSpec((t_s, D_MODEL), lambda f, m: (0, 0)),
            pl.BlockSpec((D_MODEL, tf), lambda f, m: (0, f)),
            pl.BlockSpec((D_MODEL, tf), lambda f, m: (0, f)),
        ],
        out_specs=[
            pl.BlockSpec((tm, tf), lambda f, m: (prompt_tile(m), f)),
            pl.BlockSpec((t_s, tf), lambda f, m: (0, f)),
        ],
        out_shape=(jax.ShapeDtypeStruct((t_p, D_FF), BF16),
                   jax.ShapeDtypeStruct((t_s, D_FF), BF16)),
        scratch_shapes=[pltpu.VMEM((D_MODEL, tf), BF16), pltpu.VMEM((D_MODEL, tf), BF16)],
        compiler_params=pltpu.CompilerParams(
            dimension_semantics=("arbitrary", "arbitrary"), vmem_limit_bytes=VMEM_LIMIT),
        name="ffn_up",
    )(hf_p, hf_s, wg, wu)


def _ffn_down_kernel(act_ref, wd_ref, x1_ref, gp_ref, out_ref, *, n_sub):
    for rows in _row_blocks(act_ref.shape[0], n_sub):
        ff = _dot(act_ref[rows, :], wd_ref[...])
        out_ref[rows, :] = x1_ref[rows, :] + _rms(ff, gp_ref[...])


def _ffn_down(act, wd, x1, g_post, tm, n_sub):
    t = act.shape[0]
    tok = lambda i: (i, 0)
    return pl.pallas_call(
        functools.partial(_ffn_down_kernel, n_sub=n_sub),
        grid=(t // tm,),
        in_specs=[
            pl.BlockSpec((tm, D_FF), tok),
            _resident((D_FF, D_MODEL)),
            pl.BlockSpec((tm, D_MODEL), tok),
            _resident((1, D_MODEL)),
        ],
        out_specs=pl.BlockSpec((tm, D_MODEL), tok),
        out_shape=jax.ShapeDtypeStruct((t, D_MODEL), F32),
        compiler_params=pltpu.CompilerParams(
            dimension_semantics=("arbitrary",), vmem_limit_bytes=VMEM_LIMIT),
        name="ffn_down",
    )(act, wd, x1, g_post)


def _block_diag_in(bbt):
    x = bbt.reshape(B_GROUP, N_BLOCKS, GROUPS_PER_BLOCK, B_STATE)
    x = jnp.transpose(x, (1, 2, 0, 3))
    eye = jnp.eye(GROUPS_PER_BLOCK, dtype=bool)[None, :, None, :, None]
    y = jnp.where(eye, x[:, :, :, None, :], 0.0)
    return y.reshape(N_BLOCKS, LANES, STATE_PER_BLOCK)


def _block_diag_out(c):
    x = c.reshape(N_BLOCKS, GROUPS_PER_BLOCK, B_GROUP, B_STATE)
    x = jnp.transpose(x, (0, 1, 3, 2))
    eye = jnp.eye(GROUPS_PER_BLOCK, dtype=bool)[None, :, None, :, None]
    y = jnp.where(eye, x[:, :, :, None, :], 0.0)
    return y.reshape(N_BLOCKS, STATE_PER_BLOCK, LANES)


def _row_permutation(batch):
    n = batch * PERM_T
    src = np.arange(n)
    b, t = src // PERM_T, src % PERM_T
    p = np.zeros((n, n), np.float32)
    p[t * batch + b, src] = 1.0
    return jnp.asarray(p, BF16), jnp.asarray(p.T, BF16)


def _pad_rows_front(x, rows):
    pad = [(0, 0)] * x.ndim
    pad[-2] = (rows - x.shape[-2], 0)
    return jnp.pad(x, pad)


def _pad_rows_back(x, rows):
    pad = [(0, 0)] * x.ndim
    pad[-2] = (0, rows - x.shape[-2])
    return jnp.pad(x, pad)


def _mixers(proj, conv0, c0, n0, m0, sre0, sim0, p, *, b, chunk, tm, tb):
    qk, v, og, u, gates, gates_c = proj
    t = qk.shape[0]
    s_len = t // b
    assert s_len % chunk == 0 or s_len < chunk
    seg = min(s_len, chunk)
    s_pad = -(-s_len // chunk) * chunk
    extra = s_pad - s_len
    def rows(a, mode="constant"):
        if extra == 0:
            return a
        a = jnp.pad(a.reshape(b, s_len, a.shape[-1]), ((0, 0), (0, extra), (0, 0)), mode=mode)
        return a.reshape(b * s_pad, a.shape[-1])
    g3 = gates.reshape(2 * SUBLANES, b, s_len)
    if extra:
        pad_t = ((0, 0), (0, 0), (0, extra))
        g3 = jnp.concatenate([
            jnp.pad(g3[:SUBLANES], pad_t, constant_values=-jnp.inf),
            jnp.pad(g3[SUBLANES:], pad_t, mode="edge")], axis=0)
    out_a, conv_o, c_o, n_o, m_o = _mlstm(
        rows(qk), rows(v), rows(og), g3.reshape(2 * SUBLANES, b * s_pad),
        rows(gates_c, "edge"), _pad_rows_front(conv0, SUBLANES), c0, n0,
        jnp.broadcast_to(_pad_rows_back(m0[..., None], SUBLANES), (b, SUBLANES, LANES)),
        p["w_conv"], p["b_conv"], p["g_mh"], b=b, chunk=chunk, tm=min(s_pad, tm), valid=seg)
    out_a = out_a.reshape(b, s_pad, D_A)[:, :s_len].reshape(t, D_A)

    out_b, sre_o, sim_o = _s5(
        u.reshape(b, s_len, D_B), sre0.reshape(b, S5_N), sim0.reshape(b, S5_N),
        p["lbre"], p["lbim"], p["wb"], p["wcre"], p["wcim"], p["s5_d"],
        p["perm"], p["permt"], p["w_glu"], p["b_glu"], tb=tb)
    states = (
        conv_o[:, SUBLANES - (CONV_W - 1):, :],
        c_o,
        n_o,
        m_o[:, :A_HEADS, 0],
        sre_o.reshape(b, B_GROUPS, B_STATE),
        sim_o.reshape(b, B_GROUPS, B_STATE),
    )
    return out_a, out_b.reshape(t, D_B), states


def kernel(x_prompt, x_sample, state_conv, state_mlstm_c, state_mlstm_n, state_mlstm_m, state_s5_re, state_s5_im, g_pre_mix, w_in, w_conv, b_conv, b_igate, b_fgate, g_mh, s5_lam_re, s5_lam_im, s5_log_dt, s5_b_re, s5_b_im, s5_c_re, s5_c_im, s5_d, w_glu, b_glu, w_out, g_post_mix, g_pre_ffn, w_gate, w_up, w_down, g_post_ffn):
    l = 0
    bp = x_prompt.shape[0]
    lbre, lbim, bbt_re, bbt_im = _discretise(
        s5_lam_re[l], s5_lam_im[l], s5_log_dt[l],
        jnp.transpose(s5_b_re[l], (2, 0, 1)), jnp.transpose(s5_b_im[l], (2, 0, 1)))

    o1 = QK_W
    o2 = o1 + D_A
    o3 = o2 + D_A
    o4 = o3 + A_HEADS
    o5 = o4 + A_HEADS
    wt = jnp.swapaxes(w_in[l], 0, 1)
    perm, permt = _row_permutation(bp)
    row = lambda a: a.reshape(1, -1)
    p = {
        "g_pre_mix": row(g_pre_mix[l]),
        "w_in_t": wt[:o3].astype(BF16),
        "w_in_gt": jnp.pad(wt[o3:o5], ((0, 2 * SUBLANES - 2 * A_HEADS), (0, 0))).astype(BF16),
        "w_in_ut": wt[o5:].astype(BF16),
        "gbias": jnp.concatenate(
            [b_igate[l], b_fgate[l], jnp.zeros((SUBLANES,), F32)]).reshape(2 * SUBLANES, 1),
        "w_conv": w_conv[l],
        "b_conv": row(b_conv[l]),
        "g_mh": row(g_mh[l]),
        "lbre": lbre.reshape(1, S5_N),
        "lbim": lbim.reshape(1, S5_N),
        "wb": jnp.concatenate([_block_diag_in(bbt_re), _block_diag_in(bbt_im)], axis=-1).astype(BF16),
        "wcre": _block_diag_out(s5_c_re[l]).astype(BF16),
        "wcim": _block_diag_out(s5_c_im[l]).astype(BF16),
        "s5_d": row(s5_d[l]),
        "perm": perm,
        "permt": permt,
        "b_glu": row(b_glu[l]),
        "g_post_mix": row(g_post_mix[l]),
        "g_pre_ffn": row(g_pre_ffn[l]),
    }

    bs, s_sample = x_sample.shape[0], x_sample.shape[1]
    xp2d = x_prompt.reshape(-1, D_MODEL)
    xs2d = x_sample.reshape(-1, D_MODEL)
    chunk = 128
    proj = _in_proj(xp2d, xs2d, p, tm=512, n_sub=2, seg_p=chunk, seg_s=min(s_sample, chunk),
                    cast=(w_glu[l], w_out[l]))
    proj_p, proj_s = proj[:N_PROJ_OUT], proj[N_PROJ_OUT:2 * N_PROJ_OUT]
    p["w_glu"], p["w_out"] = proj[2 * N_PROJ_OUT:]

    a_p, b_p, st_p = _mixers(
        proj_p,
        jnp.zeros((bp, CONV_W - 1, QK_W), F32),
        jnp.zeros((bp, A_HEADS, A_DV, A_DQK), F32),
        jnp.zeros((bp, A_HEADS, A_DQK), F32),
        jnp.zeros((bp, A_HEADS), F32),
        jnp.zeros((bp, B_GROUPS, B_STATE), F32),
        jnp.zeros((bp, B_GROUPS, B_STATE), F32),
        p, b=bp, chunk=chunk, tm=512, tb=64)
    a_s, b_s, st_s = _mixers(
        proj_s, state_conv[l], state_mlstm_c[l], state_mlstm_n[l], state_mlstm_m[l],
        state_s5_re[l], state_s5_im[l], p, b=bs, chunk=chunk, tm=512, tb=s_sample)

    x1_p, hf_p, x1_s, hf_s, wd = _out_proj(
        (a_p, b_p, xp2d), (a_s, b_s, xs2d), p["w_out"], p["g_post_mix"], p["g_pre_ffn"],
        tm=512, n_sub=4, cast=(w_down[l],))

    act_p, act_s = _ffn_up(hf_p, hf_s, w_gate[l], w_up[l], tm=2048, tf=512, n_sub=2)
    g_post = row(g_post_ffn[l])
    yp = _ffn_down(act_p, wd, x1_p, g_post, tm=512, n_sub=2).reshape(x_prompt.shape)
    ys = _ffn_down(act_s, wd, x1_s, g_post, tm=128, n_sub=1).reshape(x_sample.shape)
    return (yp, ys) + tuple(a[None] for a in st_p) + tuple(a[None] for a in st_s)
```

```python
import functools

import numpy as np
import jax
import jax.numpy as jnp
from jax import lax
from jax.experimental import pallas as pl
from jax.experimental.pallas import tpu as pltpu

F32 = jnp.float32
BF16 = jnp.bfloat16

D_MODEL = 2048
D_A = 1024
D_B = 1024
A_HEADS = 4
A_DV = 256
A_DQK = 128
QK_W = 1024
CONV_W = 4
B_GROUP = 16
B_GROUPS = 64
B_STATE = 64
S5_N = B_GROUPS * B_STATE
D_FF = 5632
EPS = 1e-6

SUBLANES = 8
LANES = 128
GROUPS_PER_BLOCK = LANES // B_GROUP
N_BLOCKS = B_GROUPS // GROUPS_PER_BLOCK
STATE_PER_BLOCK = GROUPS_PER_BLOCK * B_STATE
PERM_T = 16
VMEM_LIMIT = 56 * 1024 * 1024


def _sigmoid(x):
    return 1.0 / (1.0 + jnp.exp(-x))


def _gelu_tanh(x):
    c = np.sqrt(2 / np.pi).astype(np.float32)
    return x * (0.5 * (1.0 + jnp.tanh(c * (x + 0.044715 * (x ** 3)))))


def _rms(x, g):
    return x * lax.rsqrt(jnp.mean(x * x, axis=-1, keepdims=True) + EPS) * g


def _dot(a, b):
    return jnp.dot(a, b, preferred_element_type=F32)


def _log_sigmoid(z):
    return jnp.minimum(z, 0.0) - jnp.log1p(jnp.exp(-jnp.abs(z)))


def _split3(x):
    hi = x.astype(BF16)
    r = x - hi.astype(F32)
    mid = r.astype(BF16)
    lo = (r - mid.astype(F32)).astype(BF16)
    return hi, mid, lo


def _dot3_right(x, sel):
    hi, mid, lo = _split3(x)
    return _dot(hi, sel) + _dot(mid, sel) + _dot(lo, sel)


def _dot3_left(sel, x):
    hi, mid, lo = _split3(x)
    return _dot(sel, hi) + _dot(sel, mid) + _dot(sel, lo)


def _row_blocks(rows, n_sub):
    assert rows % n_sub == 0
    size = rows // n_sub
    return [slice(size * r, size * (r + 1)) for r in range(n_sub)]


def _resident(shape):
    zeros = (0,) * len(shape)
    return pl.BlockSpec(shape, lambda *_: zeros, pipeline_mode=pl.Buffered(1))


def _disc_kernel(lam_re_ref, lam_im_ref, logdt_ref, bre_ref, bim_ref,
                 lbre_ref, lbim_ref, bbre_ref, bbim_ref):
    lam_re = lam_re_ref[...]
    lam_im = lam_im_ref[...]
    dt = jnp.exp(logdt_ref[...])
    mag = jnp.exp(lam_re * dt)
    ang = lam_im * dt
    lb_re = mag * jnp.cos(ang)
    lb_im = mag * jnp.sin(ang)
    den = lam_re * lam_re + lam_im * lam_im
    nr = lb_re - 1.0
    f_re = (nr * lam_re + lb_im * lam_im) / den
    f_im = (lb_im * lam_re - nr * lam_im) / den
    lbre_ref[...] = lb_re
    lbim_ref[...] = lb_im
    b_re = bre_ref[...]
    b_im = bim_ref[...]
    bbre_ref[...] = f_re[None] * b_re - f_im[None] * b_im
    bbim_ref[...] = f_re[None] * b_im + f_im[None] * b_re


def _discretise(lam_re, lam_im, log_dt, bt_re, bt_im):
    gp = jax.ShapeDtypeStruct((B_GROUPS, B_STATE), F32)
    igp = jax.ShapeDtypeStruct((B_GROUP, B_GROUPS, B_STATE), F32)
    return pl.pallas_call(
        _disc_kernel, out_shape=(gp, gp, igp, igp), name="s5_discretise",
    )(lam_re, lam_im, log_dt.reshape(B_GROUPS, 1), bt_re, bt_im)


GATE_COL = QK_W + 2 * D_A
GATE_LANE_CMX = A_HEADS
GATE_LANE_FC = 2 * A_HEADS


def _dot_t(a, bt):
    return lax.dot_general(a, bt, (((1,), (1,)), ((), ())), preferred_element_type=F32)


def _inproj_tile(x_ref, gpre_ref, wt_ref, wut_ref, wgt_ref, gb_ref, triu_ref, tril_ref, outs,
                 n_sub, seg):
    qk_out, v_out, og_out, u_out, g_out, c_out = outs
    for rows in _row_blocks(x_ref.shape[0], n_sub):
        hm = _rms(x_ref[rows, :], gpre_ref[...]).astype(BF16)

        n = rows.stop - rows.start
        z = _dot_t(wgt_ref[...], hm) + gb_ref[...]
        qk_out[rows, :] = _dot_t(hm, wt_ref[0:QK_W, :])

        row = lax.broadcasted_iota(jnp.int32, z.shape, 0)
        zf = jnp.where((row >= A_HEADS) & (row < 2 * A_HEADS), _log_sigmoid(z), z)
        zf8 = zf[0:SUBLANES]
        zc = jnp.concatenate([zf8, jnp.zeros((LANES - SUBLANES, n), F32)], axis=0).T
        lane = lax.broadcasted_iota(jnp.int32, zc.shape, 1)
        lf_c = jnp.where((lane >= A_HEADS) & (lane < 2 * A_HEADS), zc, 0.0)
        fc = pltpu.roll(_dot3_right(zf, triu_ref[...])[0:SUBLANES], A_HEADS, axis=0)
        fc_c = _dot3_left(tril_ref[...], lf_c)
        v_out[rows, :] = _dot_t(hm, wt_ref[QK_W:QK_W + D_A, :]).astype(BF16)
        og_out[rows, :] = _dot_t(hm, wt_ref[QK_W + D_A:QK_W + 2 * D_A, :])
        u_out[rows, :] = _dot_t(hm, wut_ref[...])

        g_out[0:SUBLANES, rows] = zf8 - fc
        g_out[SUBLANES:2 * SUBLANES, rows] = fc
        cm = pltpu.roll(zc, A_HEADS, axis=1) - fc_c
        t_in = lax.broadcasted_iota(jnp.int32, zc.shape, 0) & (seg - 1)
        sh = 1
        while sh < seg:
            cm = jnp.where(t_in >= sh, jnp.maximum(cm, pltpu.roll(cm, sh, axis=0)), cm)
            sh *= 2
        c_out[rows, :] = jnp.where(lane < GATE_LANE_FC, cm, pltpu.roll(fc_c, A_HEADS, axis=1))


N_PROJ_OUT = 6


def _inproj_kernel(xp_ref, xs_ref, gpre_ref, wt_ref, wut_ref, wgt_ref, gb_ref,
                   triu_p, tril_p, triu_s, tril_s, *rest, cfg_p, cfg_s, n_cast, n_p):
    cast_in = rest[:n_cast]
    outs_p = rest[n_cast:n_cast + N_PROJ_OUT]
    outs_s = rest[n_cast + N_PROJ_OUT:n_cast + 2 * N_PROJ_OUT]
    cast_out = rest[n_cast + 2 * N_PROJ_OUT:]
    shared = (gpre_ref, wt_ref, wut_ref, wgt_ref, gb_ref)
    i = pl.program_id(0)

    @pl.when(i < n_p)
    def _():
        for src, dst in zip(cast_in, cast_out):
            dst[...] = src[...].astype(BF16)
        _inproj_tile(xp_ref, *shared, triu_p, tril_p, outs_p, *cfg_p)

    @pl.when(i == n_p)
    def _():
        _inproj_tile(xs_ref, *shared, triu_s, tril_s, outs_s, *cfg_s)


def _segment_prefix_matrices(n, seg):
    idx = np.arange(n)
    u = ((idx[:, None] // seg == idx[None, :] // seg) & (idx[:, None] <= idx[None, :]))
    u = u.astype(np.float32)
    return jnp.asarray(u, BF16), jnp.asarray(u.T, BF16)


def _in_proj(xp2d, xs2d, p, *, tm, n_sub, seg_p, seg_s, cast=()):
    t_p, t_s = xp2d.shape[0], xs2d.shape[0]
    n_p = t_p // tm
    assert seg_p & (seg_p - 1) == 0 and seg_s & (seg_s - 1) == 0
    triu_p, tril_p = _segment_prefix_matrices(tm // n_sub, seg_p)
    triu_s, tril_s = _segment_prefix_matrices(t_s, seg_s)
    tile = lambda i: jnp.minimum(i, n_p - 1)
    tok = lambda i: (tile(i), 0)
    once = lambda i: (0, 0)
    cast_specs = [pl.BlockSpec((w.shape[0] // n_p, w.shape[1]), tok) for w in cast]

    def out_specs(rows, row_map, gate_map):
        return [
            pl.BlockSpec((rows, QK_W), row_map),
            pl.BlockSpec((rows, D_A), row_map),
            pl.BlockSpec((rows, D_A), row_map),
            pl.BlockSpec((rows, D_B), row_map),
            pl.BlockSpec((2 * SUBLANES, rows), gate_map),
            pl.BlockSpec((rows, LANES), row_map),
        ]

    def out_shapes(t):
        return (
            jax.ShapeDtypeStruct((t, QK_W), F32),
            jax.ShapeDtypeStruct((t, D_A), BF16),
            jax.ShapeDtypeStruct((t, D_A), F32),
            jax.ShapeDtypeStruct((t, D_B), F32),
            jax.ShapeDtypeStruct((2 * SUBLANES, t), F32),
            jax.ShapeDtypeStruct((t, LANES), F32),
        )

    kernel = functools.partial(_inproj_kernel, cfg_p=(n_sub, seg_p), cfg_s=(1, seg_s),
                               n_cast=len(cast), n_p=n_p)
    return pl.pallas_call(
        kernel,
        grid=(n_p + 1,),
        in_specs=[
            pl.BlockSpec((tm, D_MODEL), tok),
            pl.BlockSpec((t_s, D_MODEL), once),
            _resident((1, D_MODEL)),
            _resident((GATE_COL, D_MODEL)),
            _resident((D_B, D_MODEL)),
            _resident((2 * SUBLANES, D_MODEL)),
            _resident((2 * SUBLANES, 1)),
            _resident(triu_p.shape),
            _resident(tril_p.shape),
            _resident(triu_s.shape),
            _resident(tril_s.shape),
        ] + cast_specs,
        out_specs=(out_specs(tm, tok, lambda i: (0, tile(i))) + out_specs(t_s, once, once)
                   + cast_specs),
        out_shape=(out_shapes(t_p) + out_shapes(t_s)
                   + tuple(jax.ShapeDtypeStruct(w.shape, BF16) for w in cast)),
        compiler_params=pltpu.CompilerParams(
            dimension_semantics=("arbitrary",), vmem_limit_bytes=VMEM_LIMIT),
        name="in_proj",
    )(xp2d, xs2d, p["g_pre_mix"], p["w_in_t"], p["w_in_ut"], p["w_in_gt"], p["gbias"],
      triu_p, tril_p, triu_s, tril_s, *cast)


def _head_cols(h):
    return slice(A_DV * h, A_DV * (h + 1))


def _mlstm_chunk(qk_raw, g, gc, v_at, og_at, out_at, xp, ct, ncr, m_s,
                 rep_ref, wconv_ref, bconv_ref, gmh_ref):
    L = qk_raw.shape[0]
    reps = L // LANES

    def lanes(x, n):
        return x if n == 1 else jnp.concatenate([x] * n, axis=1)

    row_i = lax.broadcasted_iota(jnp.int32, (L, L), 0)
    col_i = lax.broadcasted_iota(jnp.int32, (L, L), 1)
    tri = col_i <= row_i
    lane = lax.broadcasted_iota(jnp.int32, (SUBLANES, L), 1)

    xp[SUBLANES:SUBLANES + L, :] = qk_raw
    xs = xp[...]
    assert CONV_W == 4
    x1 = pltpu.roll(xs, 1, axis=0)
    near = xs * wconv_ref[3:4, :] + x1 * wconv_ref[2:3, :]
    far = xs * wconv_ref[1:2, :] + x1 * wconv_ref[0:1, :]
    acc = (near + pltpu.roll(far, 2, axis=0))[SUBLANES:, :] + bconv_ref[...]
    qk = acc * _sigmoid(acc)
    q_all = (qk[:, :QK_W // 2] * (A_DQK ** -0.5)).astype(BF16)
    k_all = qk[:, QK_W // 2:].astype(BF16)

    rep = _dot3_right(gc, rep_ref[...])

    gr = g[0:SUBLANES]
    fc = g[SUBLANES:2 * SUBLANES]
    f_last = jnp.sum(jnp.where(lane == L - 1, fc, 0.0), axis=1, keepdims=True)
    m_prev = m_s[...]
    m0 = jnp.max(m_prev, axis=1, keepdims=True)
    lw = f_last + gr
    m_new = jnp.maximum(f_last + m0, jnp.max(lw, axis=1, keepdims=True))
    ws = jnp.exp(lw - m_new)
    decay = jnp.broadcast_to(jnp.exp(f_last + m0 - m_new), (SUBLANES, LANES))

    for h in range(A_HEADS):
        qb = q_all[:, A_DQK * h:A_DQK * (h + 1)]
        kb = k_all[:, A_DQK * h:A_DQK * (h + 1)]
        vb = v_at(h)
        m0_h = m_prev[h:h + 1, :]
        mx = jnp.maximum(rep[:, LANES * h:LANES * (h + 1)], m0_h)
        fc_t = rep[:, LANES * (A_HEADS + h):LANES * (A_HEADS + h + 1)]
        e = jnp.exp(jnp.where(tri, gr[h:h + 1, :] - lanes(mx, reps), -jnp.inf))
        s = lax.dot_general(qb, kb, (((1,), (1,)), ((), ())),
                            preferred_element_type=F32) * e
        g_int = jnp.exp(m0_h - mx)
        ct_h = ct[h]
        ncr_h = ncr[h]
        num = (_dot(s.astype(BF16), vb)
               + lanes(g_int, A_DV // LANES) * _dot(qb, ct_h.astype(BF16)))
        den = jnp.sum(s, axis=1, keepdims=True) + g_int * _dot(qb, ncr_h.astype(BF16))
        scale = jnp.maximum(jnp.abs(den), jnp.exp(-(fc_t + mx)))
        hh = num / lanes(scale, A_DV // LANES)
        hn = _rms(hh, gmh_ref[:, _head_cols(h)])
        out_at(h, (hn * _sigmoid(og_at(h))).astype(BF16))
        kws = kb.astype(F32).T * ws[h:h + 1, :]
        dec_h = decay[h:h + 1, :]
        ct[h] = lanes(dec_h, A_DV // LANES) * ct_h + _dot(kws.astype(BF16), vb)
        ncr[h] = dec_h * ncr_h + jnp.sum(kws, axis=1, keepdims=True)
    m_s[...] = jnp.broadcast_to(m_new, (SUBLANES, LANES))


def _mlstm_load_state(xp, ct, ncr, m_s, tail0, c0_ref, n0_ref, m0):
    xp[0:SUBLANES, :] = tail0
    for h in range(A_HEADS):
        ct[h] = c0_ref[h].T
        ncr[h] = jnp.broadcast_to(n0_ref[h:h + 1, :], (A_DQK, A_DQK)).T
    m_s[...] = m0


def _mlstm_store_state(ct, ncr, m_s, co_ref, no_ref, mo_ref):
    for h in range(A_HEADS):
        co_ref[h] = ct[h].T
        no_ref[h:h + 1, :] = ncr[h].T[0:1, :]
    mo_ref[...] = m_s[...]


def _mlstm_kernel(qk_ref, v_ref, og_ref, g_ref, gc_ref, rep_ref, tail0_ref, c0_ref, n0_ref, m0_ref,
                  wconv_ref, bconv_ref, gmh_ref, outa_ref, convo_ref, co_ref, no_ref, mo_ref,
                  xp, ct, ncr, m_s, *, chunk, valid, tiles_per_stream):
    L = chunk
    n_chunks = qk_ref.shape[0] // L
    assert SUBLANES <= valid <= L
    tile = lax.rem(pl.program_id(0), tiles_per_stream)

    @pl.when(tile == 0)
    def _():
        _mlstm_load_state(xp, ct, ncr, m_s, tail0_ref[0], c0_ref.at[0], n0_ref.at[0], m0_ref[0])

    for c in range(n_chunks):
        rows = slice(L * c, L * (c + 1))

        def store(h, x, rows=rows):
            outa_ref[rows, _head_cols(h)] = x

        _mlstm_chunk(qk_ref[rows, :], g_ref[:, rows], gc_ref[rows, :],
                     lambda h, rows=rows: v_ref[rows, _head_cols(h)],
                     lambda h, rows=rows: og_ref[rows, _head_cols(h)],
                     store, xp, ct, ncr, m_s, rep_ref, wconv_ref, bconv_ref, gmh_ref)
        xp[0:SUBLANES, :] = xp[L:L + SUBLANES, :]

    @pl.when(tile == tiles_per_stream - 1)
    def _():
        convo_ref[0] = xp[valid:valid + SUBLANES, :]
        _mlstm_store_state(ct, ncr, m_s, co_ref.at[0], no_ref.at[0], mo_ref.at[0])


def _gate_replication_matrix():
    r = np.zeros((LANES, 2 * A_HEADS * LANES), np.float32)
    for h in range(A_HEADS):
        r[GATE_LANE_CMX + h, LANES * h:LANES * (h + 1)] = 1.0
        r[GATE_LANE_FC + h, LANES * (A_HEADS + h):LANES * (A_HEADS + h + 1)] = 1.0
    return jnp.asarray(r, BF16)


def _mlstm(qk, v, og, gates, gates_c, tail0, c0, n0, m0, wconv, bconv, gmh, *, b, chunk, tm, valid):
    t = qk.shape[0]
    n_steps = t // tm
    tiles_per_stream = n_steps // b
    assert tiles_per_stream * b * tm == t and tm % chunk == 0
    rep = _gate_replication_matrix()
    tok = lambda i: (i, 0)
    const2 = lambda i: (0, 0)
    per_b3 = lambda i: (i // tiles_per_stream, 0, 0)
    per_b4 = lambda i: (i // tiles_per_stream, 0, 0, 0)
    kernel = functools.partial(_mlstm_kernel, chunk=chunk, valid=valid,
                               tiles_per_stream=tiles_per_stream)
    return pl.pallas_call(
        kernel,
        grid=(n_steps,),
        in_specs=[
            pl.BlockSpec((tm, QK_W), tok),
            pl.BlockSpec((tm, D_A), tok),
            pl.BlockSpec((tm, D_A), tok),
            pl.BlockSpec((2 * SUBLANES, tm), lambda i: (0, i)),
            pl.BlockSpec((tm, LANES), tok),
            pl.BlockSpec(rep.shape, const2),
            pl.BlockSpec((1, SUBLANES, QK_W), per_b3),
            pl.BlockSpec((1, A_HEADS, A_DV, A_DQK), per_b4),
            pl.BlockSpec((1, A_HEADS, A_DQK), per_b3),
            pl.BlockSpec((1, SUBLANES, LANES), per_b3),
            pl.BlockSpec((CONV_W, QK_W), const2),
            pl.BlockSpec((1, QK_W), const2),
            pl.BlockSpec((1, D_A), const2),
        ],
        out_specs=[
            pl.BlockSpec((tm, D_A), tok),
            pl.BlockSpec((1, SUBLANES, QK_W), per_b3),
            pl.BlockSpec((1, A_HEADS, A_DV, A_DQK), per_b4),
            pl.BlockSpec((1, A_HEADS, A_DQK), per_b3),
            pl.BlockSpec((1, SUBLANES, LANES), per_b3),
        ],
        out_shape=(
            jax.ShapeDtypeStruct((t, D_A), BF16),
            jax.ShapeDtypeStruct((b, SUBLANES, QK_W), F32),
            jax.ShapeDtypeStruct((b, A_HEADS, A_DV, A_DQK), F32),
            jax.ShapeDtypeStruct((b, A_HEADS, A_DQK), F32),
            jax.ShapeDtypeStruct((b, SUBLANES, LANES), F32),
        ),
        scratch_shapes=[
            pltpu.VMEM((chunk + SUBLANES, QK_W), F32),
            pltpu.VMEM((A_HEADS, A_DQK, A_DV), F32),
            pltpu.VMEM((A_HEADS, A_DQK, LANES), F32),
            pltpu.VMEM((SUBLANES, LANES), F32),
        ],
        compiler_params=pltpu.CompilerParams(
            dimension_semantics=("arbitrary",), vmem_limit_bytes=VMEM_LIMIT),
        name="mlstm",
    )(qk, v, og, gates, gates_c, rep, tail0, c0, n0, m0, wconv, bconv, gmh)


def _s5_kernel(u_ref, s0re_ref, s0im_ref, lbre_ref, lbim_ref, wb_ref, wc_ref,
               d_ref, perm_ref, permt_ref, wglu_ref, bglu_ref,
               outb_ref, sre_out, sim_out,
               bu, sre, sim, utb, xtb, ytb, *, tb, n_steps):
    step = pl.program_id(0)
    batch = u_ref.shape[0]
    rows_j = batch * PERM_T
    n_j = tb // PERM_T

    @pl.when(step == 0)
    def _():
        sre[...] = s0re_ref[...]
        sim[...] = s0im_ref[...]

    spb = STATE_PER_BLOCK

    def permute_in():
        for j in range(n_j):
            xj = jnp.concatenate(
                [u_ref[b, PERM_T * j:PERM_T * (j + 1), :] for b in range(batch)], axis=0)
            hi = xj.astype(BF16)
            lo = (xj - hi.astype(F32)).astype(BF16)
            uh = _dot(perm_ref[...], hi)
            rows = slice(rows_j * j, rows_j * (j + 1))
            xtb[rows, :] = uh.astype(BF16)
            utb[rows, :] = uh + _dot(perm_ref[...], lo)

    def input_map(k):
        bu[:, 2 * spb * k:2 * spb * (k + 1)] = _dot(xtb[:, LANES * k:LANES * (k + 1)], wb_ref[k])

    def recurrence(k):
        cols = slice(spb * k, spb * (k + 1))
        re = slice(2 * spb * k, 2 * spb * k + spb)
        im = slice(2 * spb * k + spb, 2 * spb * (k + 1))
        a_re = jnp.broadcast_to(lbre_ref[:, cols], (batch, spb))
        a_im = jnp.broadcast_to(lbim_ref[:, cols], (batch, spb))
        s_r = sre[:, cols]
        s_i = sim[:, cols]
        for t in range(tb):
            tr = slice(t * batch, (t + 1) * batch)
            n_r = a_re * s_r - a_im * s_i + bu[tr, re]
            n_i = a_re * s_i + a_im * s_r + bu[tr, im]
            bu[tr, re] = n_r
            bu[tr, im] = n_i
            s_r, s_i = n_r, n_i
        sre[:, cols] = s_r
        sim[:, cols] = s_i

    def output_map(k):
        ytb[:, LANES * k:LANES * (k + 1)] = _dot(
            bu[:, 2 * spb * k:2 * spb * (k + 1)].astype(BF16), wc_ref[k])

    permute_in()
    input_map(0)
    for k in range(N_BLOCKS):
        if k + 1 < N_BLOCKS:
            input_map(k + 1)
        recurrence(k)
        output_map(k)

    z = _gelu_tanh(ytb[...] + d_ref[...] * utb[...])
    gate = _sigmoid(_dot(z.astype(BF16), wglu_ref[...]) + bglu_ref[...])
    res = (z * gate).astype(BF16)
    for j in range(n_j):
        rb = _dot(permt_ref[...], res[rows_j * j:rows_j * (j + 1), :]).astype(BF16)
        for b in range(batch):
            outb_ref[b, PERM_T * j:PERM_T * (j + 1), :] = rb[PERM_T * b:PERM_T * (b + 1), :]

    @pl.when(step == n_steps - 1)
    def _():
        sre_out[...] = sre[...]
        sim_out[...] = sim[...]


def _s5(u, s0re, s0im, lbre, lbim, wb, wc, d_skip, perm, permt, wglu, bglu, *, tb):
    b, s_len, _ = u.shape
    n_steps = s_len // tb
    rows = b * tb
    kernel = functools.partial(_s5_kernel, tb=tb, n_steps=n_steps)
    state = jax.ShapeDtypeStruct((b, S5_N), F32)
    return pl.pallas_call(
        kernel,
        grid=(n_steps,),
        in_specs=[
            pl.BlockSpec((b, tb, D_B), lambda i: (0, i, 0)),
            _resident((b, S5_N)),
            _resident((b, S5_N)),
            _resident((1, S5_N)),
            _resident((1, S5_N)),
            _resident((N_BLOCKS, LANES, 2 * STATE_PER_BLOCK)),
            _resident((N_BLOCKS, 2 * STATE_PER_BLOCK, LANES)),
            _resident((1, D_B)),
            _resident((b * PERM_T, b * PERM_T)),
            _resident((b * PERM_T, b * PERM_T)),
            _resident((D_B, D_B)),
            _resident((1, D_B)),
        ],
        out_specs=[
            pl.BlockSpec((b, tb, D_B), lambda i: (0, i, 0)),
            pl.BlockSpec((b, S5_N), lambda i: (0, 0)),
            pl.BlockSpec((b, S5_N), lambda i: (0, 0)),
        ],
        out_shape=(jax.ShapeDtypeStruct((b, s_len, D_B), BF16), state, state),
        scratch_shapes=[
            pltpu.VMEM((rows, 2 * S5_N), F32),
            pltpu.VMEM((b, S5_N), F32),
            pltpu.VMEM((b, S5_N), F32),
            pltpu.VMEM((rows, D_B), F32),
            pltpu.VMEM((rows, D_B), BF16),
            pltpu.VMEM((rows, D_B), F32),
        ],
        compiler_params=pltpu.CompilerParams(
            dimension_semantics=("arbitrary",), vmem_limit_bytes=VMEM_LIMIT),
        name="s5",
    )(u, s0re, s0im, lbre, lbim, wb, wc, d_skip, perm, permt, wglu, bglu)


def _outproj_tile(a_ref, b_ref, x_ref, wo_ref, gpost_ref, gffn_ref, x1_ref, hf_ref, n_sub):
    for rows in _row_blocks(a_ref.shape[0], n_sub):
        mix = (_dot(a_ref[rows, :], wo_ref[0:D_A, :])
               + _dot(b_ref[rows, :], wo_ref[D_A:D_A + D_B, :]))
        x1 = x_ref[rows, :] + _rms(mix, gpost_ref[...])
        x1_ref[rows, :] = x1
        hf_ref[rows, :] = _rms(x1, gffn_ref[...]).astype(BF16)


def _outproj_kernel(ap_ref, bp_ref, xp_ref, as_ref, bs_ref, xs_ref, wo_ref, gpost_ref, gffn_ref,
                    *rest, n_sub, n_cast, n_p):
    cast_in = rest[:n_cast]
    x1p_ref, hfp_ref, x1s_ref, hfs_ref = rest[n_cast:n_cast + 4]
    cast_out = rest[n_cast + 4:]
    shared = (wo_ref, gpost_ref, gffn_ref)
    i = pl.program_id(0)

    @pl.when(i < n_p)
    def _():
        for src, dst in zip(cast_in, cast_out):
            dst[...] = src[...].astype(BF16)
        _outproj_tile(ap_ref, bp_ref, xp_ref, *shared, x1p_ref, hfp_ref, n_sub)

    @pl.when(i == n_p)
    def _():
        _outproj_tile(as_ref, bs_ref, xs_ref, *shared, x1s_ref, hfs_ref, 1)


def _out_proj(prompt, sample, wo, g_post, g_ffn, tm, n_sub, cast=()):
    t_p, t_s = prompt[2].shape[0], sample[2].shape[0]
    n_p = t_p // tm
    tok = lambda i: (jnp.minimum(i, n_p - 1), 0)
    once = lambda i: (0, 0)
    cast_specs = [pl.BlockSpec((w.shape[0] // n_p, w.shape[1]), tok) for w in cast]
    rows_specs = lambda rows, m: [pl.BlockSpec((rows, D_A), m), pl.BlockSpec((rows, D_B), m),
                                  pl.BlockSpec((rows, D_MODEL), m)]
    out_specs = lambda rows, m: [pl.BlockSpec((rows, D_MODEL), m), pl.BlockSpec((rows, D_MODEL), m)]
    out_shapes = lambda t: (jax.ShapeDtypeStruct((t, D_MODEL), F32),
                            jax.ShapeDtypeStruct((t, D_MODEL), BF16))
    return pl.pallas_call(
        functools.partial(_outproj_kernel, n_sub=n_sub, n_cast=len(cast), n_p=n_p),
        grid=(n_p + 1,),
        in_specs=rows_specs(tm, tok) + rows_specs(t_s, once) + [
            _resident((D_A + D_B, D_MODEL)),
            _resident((1, D_MODEL)),
            _resident((1, D_MODEL)),
        ] + cast_specs,
        out_specs=out_specs(tm, tok) + out_specs(t_s, once) + cast_specs,
        out_shape=(out_shapes(t_p) + out_shapes(t_s)
                   + tuple(jax.ShapeDtypeStruct(w.shape, BF16) for w in cast)),
        compiler_params=pltpu.CompilerParams(
            dimension_semantics=("arbitrary",), vmem_limit_bytes=VMEM_LIMIT),
        name="out_proj",
    )(*prompt, *sample, wo, g_post, g_ffn, *cast)


def _ffn_up_kernel(hfp_ref, hfs_ref, wg_ref, wu_ref, actp_ref, acts_ref, wgb, wub, *, n_sub):
    m = pl.program_id(1)

    @pl.when(m == 0)
    def _():
        wgb[...] = wg_ref[...].astype(BF16)
        wub[...] = wu_ref[...].astype(BF16)

    def body(hf_ref, act_ref, n_sub):
        for rows in _row_blocks(hf_ref.shape[0], n_sub):
            hf = hf_ref[rows, :]
            a = _dot(hf, wgb[...])
            act_ref[rows, :] = (a * _sigmoid(a) * _dot(hf, wub[...])).astype(BF16)

    @pl.when(m == 0)
    def _():
        body(hfs_ref, acts_ref, 1)

    @pl.when(m > 0)
    def _():
        body(hfp_ref, actp_ref, n_sub)


def _ffn_up(hf_p, hf_s, wg, wu, tm, tf, n_sub):
    t_p, t_s = hf_p.shape[0], hf_s.shape[0]
    n_p = t_p // tm
    n_f = D_FF // tf
    prompt_tile = lambda m: jnp.maximum(m - 1, 0)
    return pl.pallas_call(
        functools.partial(_ffn_up_kernel, n_sub=n_sub),
        grid=(n_f, n_p + 1),
        in_specs=[
            pl.BlockSpec((tm, D_MODEL), lambda f, m: (prompt_tile(m), 0)),
            pl.BlockSpec((t_s, D_MODEL), lambda f, m: (0, 0)),
            pl.BlockSpec((D_MODEL, tf), lambda f, m: (0, f)),
            pl.BlockSpec((D_MODEL, tf), lambda f, m: (0, f)),
        ],
        out_specs=[
            pl.BlockSpec((tm, tf), lambda f, m: (prompt_tile(m), f)),
            pl.BlockSpec((t_s, tf), lambda f, m: (0, f)),
        ],
        out_shape=(jax.ShapeDtypeStruct((t_p, D_FF), BF16),
                   jax.ShapeDtypeStruct((t_s, D_FF), BF16)),
        scratch_shapes=[pltpu.VMEM((D_MODEL, tf), BF16), pltpu.VMEM((D_MODEL, tf), BF16)],
        compiler_params=pltpu.CompilerParams(
            dimension_semantics=("arbitrary", "arbitrary"), vmem_limit_bytes=VMEM_LIMIT),
        name="ffn_up",
    )(hf_p, hf_s, wg, wu)


def _ffn_down_kernel(act_ref, wd_ref, x1_ref, gp_ref, out_ref, *, n_sub):
    for rows in _row_blocks(act_ref.shape[0], n_sub):
        ff = _dot(act_ref[rows, :], wd_ref[...])
        out_ref[rows, :] = x1_ref[rows, :] + _rms(ff, gp_ref[...])


def _ffn_down(act, wd, x1, g_post, tm, n_sub):
    t = act.shape[0]
    tok = lambda i: (i, 0)
    return pl.pallas_call(
        functools.partial(_ffn_down_kernel, n_sub=n_sub),
        grid=(t // tm,),
        in_specs=[
            pl.BlockSpec((tm, D_FF), tok),
            _resident((D_FF, D_MODEL)),
            pl.BlockSpec((tm, D_MODEL), tok),
            _resident((1, D_MODEL)),
        ],
        out_specs=pl.BlockSpec((tm, D_MODEL), tok),
        out_shape=jax.ShapeDtypeStruct((t, D_MODEL), F32),
        compiler_params=pltpu.CompilerParams(
            dimension_semantics=("arbitrary",), vmem_limit_bytes=VMEM_LIMIT),
        name="ffn_down",
    )(act, wd, x1, g_post)


def _block_diag_in(bbt):
    x = bbt.reshape(B_GROUP, N_BLOCKS, GROUPS_PER_BLOCK, B_STATE)
    x = jnp.transpose(x, (1, 2, 0, 3))
    eye = jnp.eye(GROUPS_PER_BLOCK, dtype=bool)[None, :, None, :, None]
    y = jnp.where(eye, x[:, :, :, None, :], 0.0)
    return y.reshape(N_BLOCKS, LANES, STATE_PER_BLOCK)


def _block_diag_out(c):
    x = c.reshape(N_BLOCKS, GROUPS_PER_BLOCK, B_GROUP, B_STATE)
    x = jnp.transpose(x, (0, 1, 3, 2))
    eye = jnp.eye(GROUPS_PER_BLOCK, dtype=bool)[None, :, None, :, None]
    y = jnp.where(eye, x[:, :, :, None, :], 0.0)
    return y.reshape(N_BLOCKS, STATE_PER_BLOCK, LANES)


def _row_permutation(batch):
    n = batch * PERM_T
    src = np.arange(n)
    b, t = src // PERM_T, src % PERM_T
    p = np.zeros((n, n), np.float32)
    p[t * batch + b, src] = 1.0
    return jnp.asarray(p, BF16), jnp.asarray(p.T, BF16)


def _pad_rows_front(x, rows):
    pad = [(0, 0)] * x.ndim
    pad[-2] = (rows - x.shape[-2], 0)
    return jnp.pad(x, pad)


def _pad_rows_back(x, rows):
    pad = [(0, 0)] * x.ndim
    pad[-2] = (0, rows - x.shape[-2])
    return jnp.pad(x, pad)


def _mixers(proj, conv0, c0, n0, m0, sre0, sim0, p, *, b, chunk, tm, tb):
    qk, v, og, u, gates, gates_c = proj
    t = qk.shape[0]
    s_len = t // b
    assert s_len % chunk == 0 or s_len < chunk
    seg = min(s_len, chunk)
    s_pad = -(-s_len // chunk) * chunk
    extra = s_pad - s_len
    def rows(a, mode="constant"):
        if extra == 0:
            return a
        a = jnp.pad(a.reshape(b, s_len, a.shape[-1]), ((0, 0), (0, extra), (0, 0)), mode=mode)
        return a.reshape(b * s_pad, a.shape[-1])
    g3 = gates.reshape(2 * SUBLANES, b, s_len)
    if extra:
        pad_t = ((0, 0), (0, 0), (0, extra))
        g3 = jnp.concatenate([
            jnp.pad(g3[:SUBLANES], pad_t, constant_values=-jnp.inf),
            jnp.pad(g3[SUBLANES:], pad_t, mode="edge")], axis=0)
    out_a, conv_o, c_o, n_o, m_o = _mlstm(
        rows(qk), rows(v), rows(og), g3.reshape(2 * SUBLANES, b * s_pad),
        rows(gates_c, "edge"), _pad_rows_front(conv0, SUBLANES), c0, n0,
        jnp.broadcast_to(_pad_rows_back(m0[..., None], SUBLANES), (b, SUBLANES, LANES)),
        p["w_conv"], p["b_conv"], p["g_mh"], b=b, chunk=chunk, tm=min(s_pad, tm), valid=seg)
    out_a = out_a.reshape(b, s_pad, D_A)[:, :s_len].reshape(t, D_A)

    out_b, sre_o, sim_o = _s5(
        u.reshape(b, s_len, D_B), sre0.reshape(b, S5_N), sim0.reshape(b, S5_N),
        p["lbre"], p["lbim"], p["wb"], p["wc"], p["s5_d"],
        p["perm"], p["permt"], p["w_glu"], p["b_glu"], tb=tb)
    states = (
        conv_o[:, SUBLANES - (CONV_W - 1):, :],
        c_o,
        n_o,
        m_o[:, :A_HEADS, 0],
        sre_o.reshape(b, B_GROUPS, B_STATE),
        sim_o.reshape(b, B_GROUPS, B_STATE),
    )
    return out_a, out_b.reshape(t, D_B), states


def kernel(x_prompt, x_sample, state_conv, state_mlstm_c, state_mlstm_n, state_mlstm_m, state_s5_re, state_s5_im, g_pre_mix, w_in, w_conv, b_conv, b_igate, b_fgate, g_mh, s5_lam_re, s5_lam_im, s5_log_dt, s5_b_re, s5_b_im, s5_c_re, s5_c_im, s5_d, w_glu, b_glu, w_out, g_post_mix, g_pre_ffn, w_gate, w_up, w_down, g_post_ffn):
    l = 0
    bp = x_prompt.shape[0]
    lbre, lbim, bbt_re, bbt_im = _discretise(
        s5_lam_re[l], s5_lam_im[l], s5_log_dt[l],
        jnp.transpose(s5_b_re[l], (2, 0, 1)), jnp.transpose(s5_b_im[l], (2, 0, 1)))

    o1 = QK_W
    o2 = o1 + D_A
    o3 = o2 + D_A
    o4 = o3 + A_HEADS
    o5 = o4 + A_HEADS
    wt = jnp.swapaxes(w_in[l], 0, 1)
    perm, permt = _row_permutation(bp)
    row = lambda a: a.reshape(1, -1)
    p = {
        "g_pre_mix": row(g_pre_mix[l]),
        "w_in_t": wt[:o3].astype(BF16),
        "w_in_gt": jnp.pad(wt[o3:o5], ((0, 2 * SUBLANES - 2 * A_HEADS), (0, 0))).astype(BF16),
        "w_in_ut": wt[o5:].astype(BF16),
        "gbias": jnp.concatenate(
            [b_igate[l], b_fgate[l], jnp.zeros((SUBLANES,), F32)]).reshape(2 * SUBLANES, 1),
        "w_conv": w_conv[l],
        "b_conv": row(b_conv[l]),
        "g_mh": row(g_mh[l]),
        "lbre": lbre.reshape(1, S5_N),
        "lbim": lbim.reshape(1, S5_N),
        "wb": jnp.concatenate([_block_diag_in(bbt_re), _block_diag_in(bbt_im)], axis=-1).astype(BF16),
        "wc": jnp.concatenate([_block_diag_out(s5_c_re[l]), -_block_diag_out(s5_c_im[l])],
                              axis=1).astype(BF16),
        "s5_d": row(s5_d[l]),
        "perm": perm,
        "permt": permt,
        "b_glu": row(b_glu[l]),
        "g_post_mix": row(g_post_mix[l]),
        "g_pre_ffn": row(g_pre_ffn[l]),
    }

    bs, s_sample = x_sample.shape[0], x_sample.shape[1]
    xp2d = x_prompt.reshape(-1, D_MODEL)
    xs2d = x_sample.reshape(-1, D_MODEL)
    chunk = 128
    proj = _in_proj(xp2d, xs2d, p, tm=512, n_sub=2, seg_p=chunk, seg_s=min(s_sample, chunk),
                    cast=(w_glu[l], w_out[l]))
    proj_p, proj_s = proj[:N_PROJ_OUT], proj[N_PROJ_OUT:2 * N_PROJ_OUT]
    p["w_glu"], p["w_out"] = proj[2 * N_PROJ_OUT:]

    a_p, b_p, st_p = _mixers(
        proj_p,
        jnp.zeros((bp, CONV_W - 1, QK_W), F32),
        jnp.zeros((bp, A_HEADS, A_DV, A_DQK), F32),
        jnp.zeros((bp, A_HEADS, A_DQK), F32),
        jnp.zeros((bp, A_HEADS), F32),
        jnp.zeros((bp, B_GROUPS, B_STATE), F32),
        jnp.zeros((bp, B_GROUPS, B_STATE), F32),
        p, b=bp, chunk=chunk, tm=512, tb=64)
    a_s, b_s, st_s = _mixers(
        proj_s, state_conv[l], state_mlstm_c[l], state_mlstm_n[l], state_mlstm_m[l],
        state_s5_re[l], state_s5_im[l], p, b=bs, chunk=chunk, tm=512, tb=s_sample)

    x1_p, hf_p, x1_s, hf_s, wd = _out_proj(
        (a_p, b_p, xp2d), (a_s, b_s, xs2d), p["w_out"], p["g_post_mix"], p["g_pre_ffn"],
        tm=512, n_sub=4, cast=(w_down[l],))

    act_p, act_s = _ffn_up(hf_p, hf_s, w_gate[l], w_up[l], tm=2048, tf=512, n_sub=2)
    g_post = row(g_post_ffn[l])
    yp = _ffn_down(act_p, wd, x1_p, g_post, tm=512, n_sub=2).reshape(x_prompt.shape)
    ys = _ffn_down(act_s, wd, x1_s, g_post, tm=128, n_sub=1).reshape(x_sample.shape)
    return (yp, ys) + tuple(a[None] for a in st_p) + tuple(a[None] for a in st_s)
```
